```python
import jax, jax.numpy as jnp
from jax import lax
import numpy as np

D_MODEL = 1024
BATCH = 2
SEQ = 8192
DEPTH = 2

N_MIXERS = 2
N_GLA = (DEPTH + 1) // 2
N_MLA = DEPTH // 2
EPS = 1e-6

GLA_HEADS = 4
GLA_DK = D_MODEL // 2 // GLA_HEADS
GLA_DV = D_MODEL // GLA_HEADS
GLA_GATE_RANK = 16
GLA_TAU = 16.0
GLA_CHUNK = 64

MLA_HEADS = 8
MLA_NOPE = 128
MLA_ROPE = 64
MLA_VDIM = 128
MLA_Q_RANK = 384
MLA_KV_RANK = 256
ROPE_THETA = 10000.0
Q_BLOCK = 128

D_FF = -(-8 * D_MODEL // (3 * 256)) * 256

kernel_name = "hybrid_gla_mla_adaln_trunk"


def rmsnorm(t, g):
    tf = t.astype(jnp.float32)
    y = tf * lax.rsqrt(jnp.mean(tf * tf, axis=-1, keepdims=True) + EPS)
    return (y * g.astype(jnp.float32)).astype(t.dtype)


def modulate(t, g, shift, scale):
    return rmsnorm(t, g) * (1.0 + scale[:, None, :]) + shift[:, None, :]


def gla_chunked(q, k, v, log_a):
    B, S, H, DK = q.shape
    DV = v.shape[-1]
    C = GLA_CHUNK
    n = S // C

    def to_chunks(t):
        return t.astype(jnp.float32).reshape(B, n, C, H, t.shape[-1]).transpose(1, 0, 3, 2, 4)

    qc, kc, vc, gc = to_chunks(q), to_chunks(k), to_chunks(v), to_chunks(log_a)
    causal = jnp.tril(jnp.ones((C, C), dtype=bool))[:, :, None]

    def step(state, inp):
        qi, ki, vi, gi = inp
        b = jnp.cumsum(gi, axis=-2)
        o_inter = jnp.einsum('bhck,bhkv->bhcv', qi * jnp.exp(b), state)
        diff = b[:, :, :, None, :] - b[:, :, None, :, :]
        decay = jnp.exp(jnp.where(causal, diff, -jnp.inf))
        attn = jnp.einsum('bhik,bhjk,bhijk->bhij', qi, ki, decay)
        o_intra = jnp.einsum('bhij,bhjv->bhiv', attn, vi)
        b_last = b[:, :, -1:, :]
        new_state = state * jnp.exp(b_last[:, :, 0, :])[..., None] + jnp.einsum(
            'bhck,bhcv->bhkv', ki * jnp.exp(b_last - b), vi)
        return new_state, o_inter + o_intra

    state0 = jnp.zeros((B, H, DK, DV), jnp.float32)
    _, o = lax.scan(step, state0, (qc, kc, vc, gc))
    return o.transpose(1, 0, 3, 2, 4).reshape(B, S, H, DV)


def gla_mixer(h, w_in, w_gate, b_gate, g_out, w_out):
    B, S, _ = h.shape
    H, DK, DV = GLA_HEADS, GLA_DK, GLA_DV
    proj = h @ w_in
    q, k, v, r, glr = jnp.split(proj, [H * DK, 2 * H * DK, 2 * H * DK + H * DV,
                                       2 * H * DK + 2 * H * DV], axis=-1)
    q = q.reshape(B, S, H, DK) * (DK ** -0.5)
    k = k.reshape(B, S, H, DK)
    v = v.reshape(B, S, H, DV)
    r = r.reshape(B, S, H, DV)
    log_a = jax.nn.log_sigmoid((glr @ w_gate + b_gate).astype(jnp.float32)) / GLA_TAU
    o = gla_chunked(q, k, v, log_a.reshape(B, S, H, DK)).astype(h.dtype)
    o = rmsnorm(o, g_out) * jax.nn.silu(r)
    return o.reshape(B, S, H * DV) @ w_out


def apply_rope(t, cos, sin):
    tf = t.astype(jnp.float32).reshape(*t.shape[:-1], -1, 2)
    t1, t2 = tf[..., 0], tf[..., 1]
    out = jnp.stack([t1 * cos - t2 * sin, t1 * sin + t2 * cos], axis=-1).reshape(t.shape)
    return out.astype(t.dtype)


def mla_mixer(h, cos, sin, w_in, g_q, w_q_up, g_kv, w_kv_up, w_out):
    B, S, _ = h.shape
    H = MLA_HEADS
    proj = h @ w_in
    cq, ckv, k_rope = jnp.split(proj, [MLA_Q_RANK, MLA_Q_RANK + MLA_KV_RANK], axis=-1)
    q = (rmsnorm(cq, g_q) @ w_q_up).reshape(B, S, H, MLA_NOPE + MLA_ROPE)
    q_nope, q_rope = jnp.split(q, [MLA_NOPE], axis=-1)
    kv = (rmsnorm(ckv, g_kv) @ w_kv_up).reshape(B, S, H, MLA_NOPE + MLA_VDIM)
    k_nope, v = jnp.split(kv, [MLA_NOPE], axis=-1)
    q_rope = apply_rope(q_rope, cos, sin)
    k_rope = apply_rope(k_rope[:, :, None, :], cos, sin)[:, :, 0, :]

    scale = (MLA_NOPE + MLA_ROPE) ** -0.5
    nb = S // Q_BLOCK
    qn = q_nope.reshape(B, nb, Q_BLOCK, H, MLA_NOPE).transpose(1, 0, 2, 3, 4)
    qr = q_rope.reshape(B, nb, Q_BLOCK, H, MLA_ROPE).transpose(1, 0, 2, 3, 4)
    kpos = jnp.arange(S)

    def block(args):
        qn_b, qr_b, idx = args
        s = (jnp.einsum('bqhd,bkhd->bhqk', qn_b, k_nope)
             + jnp.einsum('bqhd,bkd->bhqk', qr_b, k_rope)).astype(jnp.float32) * scale
        qpos = idx * Q_BLOCK + jnp.arange(Q_BLOCK)
        mask = kpos[None, :] <= qpos[:, None]
        p = jax.nn.softmax(jnp.where(mask, s, -jnp.inf), axis=-1)
        return jnp.einsum('bhqk,bkhd->bqhd', p.astype(v.dtype), v)

    o = lax.map(block, (qn, qr, jnp.arange(nb)))
    o = o.transpose(1, 0, 2, 3, 4).reshape(B, S, H * MLA_VDIM)
    return o @ w_out


def swiglu(h, w_in, w_out):
    gate, up = jnp.split(h @ w_in, 2, axis=-1)
    return (jax.nn.silu(gate) * up) @ w_out


def setup_inputs(seed: int = 0) -> dict:
    key = jax.random.key(seed)
    ks = jax.random.split(key, 24)
    D = D_MODEL
    f32 = jnp.float32

    def w(k, shape, fan_in):
        return jax.random.normal(k, shape, f32) * (fan_in ** -0.5)

    def gain(k, shape):
        return 1.0 + 0.02 * jax.random.normal(k, shape, f32)

    x = jax.random.normal(ks[0], (BATCH, SEQ, D), f32)
    c = jax.random.normal(ks[1], (BATCH, D), f32)
    offsets = jax.random.randint(ks[2], (BATCH, 1), 0, 4096, dtype=jnp.int32)
    positions = offsets + jnp.arange(SEQ, dtype=jnp.int32)[None, :]

    gla_in = 2 * GLA_HEADS * GLA_DK + 2 * GLA_HEADS * GLA_DV + GLA_GATE_RANK
    mla_in = MLA_Q_RANK + MLA_KV_RANK + MLA_ROPE
    return {
        "x": x,
        "c": c,
        "positions": positions,
        "ada_w": w(ks[3], (DEPTH, D, 6 * D), D),
        "ada_b": 0.02 * jax.random.normal(ks[4], (DEPTH, 6 * D), f32),
        "norm_mix": gain(ks[5], (DEPTH, D)),
        "norm_ffn": gain(ks[6], (DEPTH, D)),
        "gla_w_in": w(ks[7], (N_GLA, D, gla_in), D),
        "gla_w_gate": w(ks[8], (N_GLA, GLA_GATE_RANK, GLA_HEADS * GLA_DK), GLA_GATE_RANK),
        "gla_b_gate": 0.1 * jax.random.normal(ks[9], (N_GLA, GLA_HEADS * GLA_DK), f32),
        "gla_g_out": gain(ks[10], (N_GLA, GLA_DV)),
        "gla_w_out": w(ks[11], (N_GLA, GLA_HEADS * GLA_DV, D), GLA_HEADS * GLA_DV),
        "mla_w_in": w(ks[12], (N_MLA, D, mla_in), D),
        "mla_g_q": gain(ks[13], (N_MLA, MLA_Q_RANK)),
        "mla_w_q_up": w(ks[14], (N_MLA, MLA_Q_RANK, MLA_HEADS * (MLA_NOPE + MLA_ROPE)), MLA_Q_RANK),
        "mla_g_kv": gain(ks[15], (N_MLA, MLA_KV_RANK)),
        "mla_w_kv_up": w(ks[16], (N_MLA, MLA_KV_RANK, MLA_HEADS * (MLA_NOPE + MLA_VDIM)), MLA_KV_RANK),
        "mla_w_out": w(ks[17], (N_MLA, MLA_HEADS * MLA_VDIM, D), MLA_HEADS * MLA_VDIM),
        "ffn_w_in": w(ks[18], (DEPTH, D, 2 * D_FF), D),
        "ffn_w_out": w(ks[19], (DEPTH, D_FF, D), D_FF),
        "final_norm": gain(ks[20], (D,)),
    }


def reference(x, c, positions, ada_w, ada_b, norm_mix, norm_ffn,
              gla_w_in, gla_w_gate, gla_b_gate, gla_g_out, gla_w_out,
              mla_w_in, mla_g_q, mla_w_q_up, mla_g_kv, mla_w_kv_up, mla_w_out,
              ffn_w_in, ffn_w_out, final_norm):
    inv_freq = ROPE_THETA ** (-jnp.arange(0, MLA_ROPE, 2, dtype=jnp.float32) / MLA_ROPE)
    ang = positions.astype(jnp.float32)[..., None] * inv_freq
    cos = jnp.cos(ang)[:, :, None, :]
    sin = jnp.sin(ang)[:, :, None, :]
    c_act = jax.nn.silu(c)

    for i in range(DEPTH):
        mod = c_act @ ada_w[i] + ada_b[i]
        sh1, sc1, g1, sh2, sc2, g2 = jnp.split(mod, 6, axis=-1)
        h = modulate(x, norm_mix[i], sh1, sc1)
        j = i // N_MIXERS
        if i % N_MIXERS == 0:
            y = gla_mixer(h, gla_w_in[j], gla_w_gate[j], gla_b_gate[j], gla_g_out[j], gla_w_out[j])
        else:
            y = mla_mixer(h, cos, sin, mla_w_in[j], mla_g_q[j], mla_w_q_up[j],
                          mla_g_kv[j], mla_w_kv_up[j], mla_w_out[j])
        x = x + g1[:, None, :] * y
        h = modulate(x, norm_ffn[i], sh2, sc2)
        x = x + g2[:, None, :] * swiglu(h, ffn_w_in[i], ffn_w_out[i])

    return rmsnorm(x, final_norm)
```

```python
import functools

import jax
import jax.numpy as jnp
from jax import lax
from jax.experimental import pallas as pl
from jax.experimental.pallas import tpu as pltpu

F32 = jnp.float32
BF16 = jnp.bfloat16

D_MODEL = 1024
EPS = 1e-6

GLA_HEADS = 4
GLA_DK = 128
GLA_DV = 256
GLA_GATE_RANK = 16
GLA_TAU = 16.0
GLA_CHUNK = 64

MLA_HEADS = 8
MLA_NOPE = 128
MLA_ROPE = 64
MLA_VDIM = 128
MLA_Q_RANK = 384
MLA_KV_RANK = 256
ROPE_THETA = 10000.0
MLA_QK_PAD = 256

D_FF = 2816

LANE = 128
ROW_TILE = 512
GLA_TIME_BLOCK = 512
ATT_BLOCK = 512
FFN_COL_CHUNK = 256
VMEM_LIMIT = 56 * 1024 * 1024


def _cparams(sem):
    return pltpu.CompilerParams(dimension_semantics=sem, vmem_limit_bytes=VMEM_LIMIT)


def _dot(a, b):
    return jnp.dot(a, b, preferred_element_type=F32)


def _dot_nt(a, b):
    return lax.dot_general(a, b, (((1,), (1,)), ((), ())), preferred_element_type=F32)


def _sigmoid(x):
    return 1.0 / (1.0 + jnp.exp(-x))


def _rms(x, g):
    return x * lax.rsqrt(jnp.mean(x * x, axis=-1, keepdims=True) + EPS) * g


def _norm_mod(x, g, shift, scale):
    return _rms(x, g) * (1.0 + scale) + shift


def _adaln_kernel(c_ref, w_ref, b_ref, o_ref):
    c = c_ref[...]
    o_ref[...] = _dot(c * _sigmoid(c), w_ref[...]) + b_ref[...]


def _adaln(c_pad, ada_w, ada_b):
    depth, d, n = ada_w.shape
    tn = 1536
    return pl.pallas_call(
        _adaln_kernel,
        grid=(depth, n // tn),
        in_specs=[
            pl.BlockSpec((8, d), lambda l, j: (0, 0)),
            pl.BlockSpec((None, d, tn), lambda l, j: (l, 0, j)),
            pl.BlockSpec((None, 1, tn), lambda l, j: (l, 0, j)),
        ],
        out_specs=pl.BlockSpec((None, 8, tn), lambda l, j: (l, 0, j)),
        out_shape=jax.ShapeDtypeStruct((depth, 8, n), F32),
        compiler_params=_cparams(("arbitrary", "arbitrary")),
        name="adaln",
    )(c_pad, ada_w, ada_b.reshape(depth, 1, n))


def _rope_table_kernel(pos_ref, f_ref, cos_ref, sin_ref):
    ang = pos_ref[...].astype(F32) * f_ref[...]
    cos_ref[...] = jnp.cos(ang)
    sin_ref[...] = jnp.sin(ang)


def _rope_tables(positions):
    b, s = positions.shape
    half = MLA_ROPE // 2
    per_row = LANE // half
    inv_freq = ROPE_THETA ** (-jnp.arange(0, MLA_ROPE, 2, dtype=F32) / MLA_ROPE)
    pos_dense = jnp.repeat(positions, half, axis=-1).reshape(b, s // per_row, LANE)
    f_dense = jnp.tile(inv_freq, per_row).reshape(1, LANE)
    spec = pl.BlockSpec((None, s // per_row, LANE), lambda i: (i, 0, 0))
    cos, sin = pl.pallas_call(
        _rope_table_kernel,
        grid=(b,),
        in_specs=[spec, pl.BlockSpec((1, LANE), lambda i: (0, 0))],
        out_specs=[spec, spec],
        out_shape=[jax.ShapeDtypeStruct((b, s // per_row, LANE), F32)] * 2,
        compiler_params=_cparams(("arbitrary",)),
        name="rope_tables",
    )(pos_dense, f_dense)
    cos = cos.reshape(b, s, half)
    sin = sin.reshape(b, s, half)
    zero = jnp.zeros((b, s, LANE - 2 * half), F32)
    return (jnp.concatenate([cos, cos, zero], axis=-1),
            jnp.concatenate([-sin, sin, zero], axis=-1))


def _row_spec(width):
    return pl.BlockSpec((None, ROW_TILE, width), lambda b, i: (b, i, 0))


def _mod_spec(k):
    return pl.BlockSpec((None, None, 1, D_MODEL), lambda b, i, k=k: (b, k, 0, 0))


def _const_spec(shape):
    return pl.BlockSpec(shape, lambda b, i: (0,) * len(shape))


def _gla_in_kernel(x_ref, g_ref, sh_ref, sc_ref, wq_ref, wk_ref, wv_ref, wr_ref,
                   wl_ref, wgate_ref, bgate_ref, q_ref, k_ref, v_ref, r_ref, la_ref):
    h = _norm_mod(x_ref[...], g_ref[...], sh_ref[...], sc_ref[...]).astype(BF16)
    q_ref[...] = (_dot(h, wq_ref[...]) * (GLA_DK ** -0.5)).astype(BF16)
    k_ref[...] = _dot(h, wk_ref[...]).astype(BF16)
    v_ref[...] = _dot(h, wv_ref[...]).astype(BF16)
    r_ref[...] = _dot(h, wr_ref[...]).astype(BF16)
    low_rank = _dot(h, wl_ref[...]).astype(BF16)
    z = _dot(low_rank, wgate_ref[...]) + bgate_ref[...]
    log_sig = jnp.minimum(z, 0.0) - jnp.log(1.0 + jnp.exp(-jnp.abs(z)))
    la_ref[...] = log_sig / GLA_TAU


def _gla_in(x, mod, gain, w_in, w_gate, b_gate):
    b, s, d = x.shape
    hk = GLA_HEADS * GLA_DK
    hv = GLA_HEADS * GLA_DV
    w = w_in.astype(BF16)
    wq, wk = w[:, :hk], w[:, hk:2 * hk]
    wv, wr = w[:, 2 * hk:2 * hk + hv], w[:, 2 * hk + hv:2 * hk + 2 * hv]
    wl = jnp.pad(w[:, 2 * hk + 2 * hv:], ((0, 0), (0, LANE - GLA_GATE_RANK)))
    wgate = jnp.pad(w_gate.astype(BF16), ((0, LANE - GLA_GATE_RANK), (0, 0)))
    return pl.pallas_call(
        _gla_in_kernel,
        grid=(b, s // ROW_TILE),
        in_specs=[
            _row_spec(d), _const_spec((1, d)), _mod_spec(0), _mod_spec(1),
            _const_spec((d, hk)), _const_spec((d, hk)), _const_spec((d, hv)),
            _const_spec((d, hv)), _const_spec((d, LANE)), _const_spec((LANE, hk)),
            _const_spec((1, hk)),
        ],
        out_specs=[_row_spec(hk), _row_spec(hk), _row_spec(hv), _row_spec(hv), _row_spec(hk)],
        out_shape=[
            jax.ShapeDtypeStruct((b, s, hk), BF16), jax.ShapeDtypeStruct((b, s, hk), BF16),
            jax.ShapeDtypeStruct((b, s, hv), BF16), jax.ShapeDtypeStruct((b, s, hv), BF16),
            jax.ShapeDtypeStruct((b, s, hk), F32),
        ],
        compiler_params=_cparams(("arbitrary", "arbitrary")),
        name="gla_in",
    )(x, gain.reshape(1, d), mod, mod, wq, wk, wv, wr, wl, wgate, b_gate.reshape(1, hk))


def _gla_kernel(q_ref, k_ref, v_ref, r_ref, la_ref, go_ref, o_ref, state_ref):
    c = GLA_CHUNK

    @pl.when(pl.program_id(2) == 0)
    def _():
        state_ref[...] = jnp.zeros_like(state_ref)

    row = lax.broadcasted_iota(jnp.int32, (c, c), 0)
    col = lax.broadcasted_iota(jnp.int32, (c, c), 1)
    causal = row >= col
    tri = jnp.where(causal, 1.0, 0.0).astype(BF16)

    for ci in range(GLA_TIME_BLOCK // c):
        sl = slice(ci * c, (ci + 1) * c)
        la = la_ref[sl, :]
        la_hi = la.astype(BF16)
        la_lo = (la - la_hi.astype(F32)).astype(BF16)
        b = _dot(tri, la_hi) + _dot(tri, la_lo)
        q = q_ref[sl, :].astype(F32)
        k = k_ref[sl, :].astype(F32)
        v = v_ref[sl, :]
        qe = (q * jnp.exp(b)).astype(BF16)
        ke = (k * jnp.exp(-b)).astype(BF16)
        state = state_ref[...]
        attn = jnp.where(causal, _dot_nt(qe, ke), 0.0)
        o = _dot(qe, state.astype(BF16)) + _dot(attn.astype(BF16), v)
        b_t = b.T
        b_last = b_t[:, c - 1:c]
        kd_t = (k.T * jnp.exp(b_last - b_t)).astype(BF16)
        state_ref[...] = state * jnp.exp(b_last) + _dot(kd_t, v)
        r = r_ref[sl, :].astype(F32)
        o_ref[sl, :] = (_rms(o, go_ref[...]) * (r * _sigmoid(r))).astype(BF16)


def _gla(q, k, v, r, la, g_out):
    b, s, _ = q.shape
    t = GLA_TIME_BLOCK

    def spec(width):
        return pl.BlockSpec((None, t, width), lambda bi, h, ti: (bi, ti, h))

    return pl.pallas_call(
        _gla_kernel,
        grid=(b, GLA_HEADS, s // t),
        in_specs=[spec(GLA_DK), spec(GLA_DK), spec(GLA_DV), spec(GLA_DV), spec(GLA_DK),
                  pl.BlockSpec((1, GLA_DV), lambda bi, h, ti: (0, 0))],
        out_specs=spec(GLA_DV),
        out_shape=jax.ShapeDtypeStruct((b, s, GLA_HEADS * GLA_DV), BF16),
        scratch_shapes=[pltpu.VMEM((GLA_DK, GLA_DV), F32)],
        compiler_params=_cparams(("arbitrary", "arbitrary", "arbitrary")),
        name="gla_scan",
    )(q, k, v, r, la, g_out.reshape(1, GLA_DV))


def _out_res_kernel(a_ref, w_ref, x_ref, gate_ref, *rest, final_norm):
    o_ref = rest[-1]
    y = x_ref[...] + gate_ref[...] * _dot(a_ref[...], w_ref[...])
    if final_norm:
        y = _rms(y, rest[0][...])
    o_ref[...] = y


def _out_res(a, w, x, mod, gate_idx, final_gain=None):
    b, s, d = x.shape
    kdim = a.shape[-1]
    in_specs = [_row_spec(kdim), _const_spec((kdim, d)), _row_spec(d), _mod_spec(gate_idx)]
    args = [a, w.astype(BF16), x, mod]
    if final_gain is not None:
        in_specs.append(_const_spec((1, d)))
        args.append(final_gain.reshape(1, d))
    return pl.pallas_call(
        functools.partial(_out_res_kernel, final_norm=final_gain is not None),
        grid=(b, s // ROW_TILE),
        in_specs=in_specs,
        out_specs=_row_spec(d),
        out_shape=jax.ShapeDtypeStruct((b, s, d), F32),
        compiler_params=_cparams(("arbitrary", "arbitrary")),
        name="out_res",
    )(*args)


def _ffn_in_kernel(x_ref, g_ref, sh_ref, sc_ref, wg_ref, wu_ref, a_ref):
    h = _norm_mod(x_ref[...], g_ref[...], sh_ref[...], sc_ref[...]).astype(BF16)
    for j in range(0, D_FF, FFN_COL_CHUNK):
        cols = slice(j, j + FFN_COL_CHUNK)
        gate = _dot(h, wg_ref[:, cols])
        up = _dot(h, wu_ref[:, cols])
        a_ref[:, cols] = (gate * _sigmoid(gate) * up).astype(BF16)


def _ffn_in(x, mod, gain, w_in):
    b, s, d = x.shape
    w = w_in.astype(BF16)
    return pl.pallas_call(
        _ffn_in_kernel,
        grid=(b, s // ROW_TILE),
        in_specs=[_row_spec(d), _const_spec((1, d)), _mod_spec(3), _mod_spec(4),
                  _const_spec((d, D_FF)), _const_spec((d, D_FF))],
        out_specs=_row_spec(D_FF),
        out_shape=jax.ShapeDtypeStruct((b, s, D_FF), BF16),
        compiler_params=_cparams(("arbitrary", "arbitrary")),
        name="ffn_in",
    )(x, gain.reshape(1, d), mod, mod, w[:, :D_FF], w[:, D_FF:])


def _mla_in_kernel(x_ref, g_ref, sh_ref, sc_ref, win_ref, gq_ref, wq_ref, gkv_ref,
                   wkn_ref, wv_ref, cos_ref, sin_ref, q_ref, kc_ref, v_ref, *, q_scale):
    h = _norm_mod(x_ref[...], g_ref[...], sh_ref[...], sc_ref[...]).astype(BF16)
    proj = _dot(h, win_ref[...])
    cq = _rms(proj[:, :MLA_Q_RANK], gq_ref[...]).astype(BF16)
    ckv = _rms(proj[:, MLA_Q_RANK:MLA_Q_RANK + MLA_KV_RANK], gkv_ref[...]).astype(BF16)
    cos = cos_ref[...]
    sin = sin_ref[...]

    def rope(t):
        return t * cos + pltpu.roll(t, MLA_ROPE, axis=1) * sin

    k_rope = rope(proj[:, MLA_Q_RANK + MLA_KV_RANK:]).astype(BF16)
    v_ref[...] = _dot(ckv, wv_ref[...]).astype(BF16)
    k_nope = _dot(ckv, wkn_ref[...])
    q_all = _dot(cq, wq_ref[...])
    for hd in range(MLA_HEADS):
        lo = hd * MLA_QK_PAD
        mid = lo + MLA_NOPE
        hi = lo + MLA_QK_PAD
        q_ref[:, lo:mid] = (q_all[:, lo:mid] * q_scale).astype(BF16)
        q_ref[:, mid:hi] = (rope(q_all[:, mid:hi]) * q_scale).astype(BF16)
        kc_ref[:, lo:mid] = k_nope[:, hd * MLA_NOPE:(hd + 1) * MLA_NOPE].astype(BF16)
        kc_ref[:, mid:hi] = k_rope


def _half_split(w_rope):
    even, odd = w_rope[..., 0::2], w_rope[..., 1::2]
    return jnp.concatenate([even, odd, odd, even], axis=-1)


def _mla_in(x, mod, gain, cos_t, sin_t, w_in, g_q, w_q_up, g_kv, w_kv_up):
    b, s, d = x.shape
    hq = MLA_HEADS * MLA_QK_PAD
    hn = MLA_HEADS * MLA_NOPE
    hv = MLA_HEADS * MLA_VDIM
    lat = MLA_Q_RANK + MLA_KV_RANK
    win = jnp.concatenate([w_in[:, :lat], _half_split(w_in[:, lat:])], axis=-1).astype(BF16)
    wq3 = w_q_up.reshape(MLA_Q_RANK, MLA_HEADS, MLA_NOPE + MLA_ROPE)
    wq = jnp.concatenate([wq3[..., :MLA_NOPE], _half_split(wq3[..., MLA_NOPE:])], axis=-1)
    wq = wq.reshape(MLA_Q_RANK, hq).astype(BF16)
    wkv3 = w_kv_up.reshape(MLA_KV_RANK, MLA_HEADS, MLA_NOPE + MLA_VDIM)
    wkn = wkv3[..., :MLA_NOPE].reshape(MLA_KV_RANK, hn).astype(BF16)
    wv = wkv3[..., MLA_NOPE:].reshape(MLA_KV_RANK, hv).astype(BF16)
    win_w = win.shape[-1]
    q_scale = (MLA_NOPE + MLA_ROPE) ** -0.5
    return pl.pallas_call(
        functools.partial(_mla_in_kernel, q_scale=q_scale),
        grid=(b, s // ROW_TILE),
        in_specs=[
            _row_spec(d), _const_spec((1, d)), _mod_spec(0), _mod_spec(1),
            _const_spec((d, win_w)), _const_spec((1, MLA_Q_RANK)), _const_spec((MLA_Q_RANK, hq)),
            _const_spec((1, MLA_KV_RANK)), _const_spec((MLA_KV_RANK, hn)),
            _const_spec((MLA_KV_RANK, hv)), _row_spec(LANE), _row_spec(LANE),
        ],
        out_specs=[_row_spec(hq), _row_spec(hq), _row_spec(hv)],
        out_shape=[jax.ShapeDtypeStruct((b, s, hq), BF16), jax.ShapeDtypeStruct((b, s, hq), BF16),
                   jax.ShapeDtypeStruct((b, s, hv), BF16)],
        compiler_params=_cparams(("arbitrary", "arbitrary")),
        name="mla_in",
    )(x, gain.reshape(1, d), mod, mod, win, g_q.reshape(1, -1), wq, g_kv.reshape(1, -1),
      wkn, wv, cos_t, sin_t)


def _attn_kernel(q_ref, k_ref, v_ref, o_ref, m_ref, l_ref, acc_ref):
    t = ATT_BLOCK
    qi = pl.program_id(2)
    q = q_ref[...]
    m_ref[...] = jnp.full_like(m_ref, -jnp.inf)
    l_ref[...] = jnp.zeros_like(l_ref)
    acc_ref[...] = jnp.zeros_like(acc_ref)

    def step(kj, masked):
        start = pl.multiple_of(kj * t, t)
        s = _dot_nt(q, k_ref[pl.ds(start, t), :])
        if masked:
            row = lax.broadcasted_iota(jnp.int32, (t, t), 0)
            col = lax.broadcasted_iota(jnp.int32, (t, t), 1)
            s = jnp.where(row >= col, s, -jnp.inf)
        m_prev = m_ref[...]
        m_new = jnp.maximum(m_prev, jnp.max(s, axis=-1, keepdims=True))
        p = jnp.exp(s - m_new)
        alpha = jnp.exp(m_prev - m_new)
        l_ref[...] = alpha * l_ref[...] + jnp.sum(p, axis=-1, keepdims=True)
        acc_ref[...] = alpha * acc_ref[...] + _dot(p.astype(BF16), v_ref[pl.ds(start, t), :])
        m_ref[...] = m_new

    def body(kj, carry):
        step(kj, masked=False)
        return carry

    lax.fori_loop(0, qi, body, 0)
    step(qi, masked=True)
    o_ref[...] = (acc_ref[...] / l_ref[...]).astype(BF16)


def _attention(q, kc, v):
    b, s, _ = v.shape
    t = ATT_BLOCK
    return pl.pallas_call(
        _attn_kernel,
        grid=(b, MLA_HEADS, s // t),
        in_specs=[
            pl.BlockSpec((None, t, MLA_QK_PAD), lambda bi, h, i: (bi, i, h)),
            pl.BlockSpec((None, s, MLA_QK_PAD), lambda bi, h, i: (bi, 0, h)),
            pl.BlockSpec((None, s, MLA_VDIM), lambda bi, h, i: (bi, 0, h)),
        ],
        out_specs=pl.BlockSpec((None, t, MLA_VDIM), lambda bi, h, i: (bi, i, h)),
        out_shape=jax.ShapeDtypeStruct((b, s, MLA_HEADS * MLA_VDIM), BF16),
        scratch_shapes=[pltpu.VMEM((t, 1), F32), pltpu.VMEM((t, 1), F32),
                        pltpu.VMEM((t, MLA_VDIM), F32)],
        compiler_params=_cparams(("arbitrary", "arbitrary", "arbitrary")),
        name="mla_attention",
    )(q, kc, v)


def kernel(x, c, positions, ada_w, ada_b, norm_mix, norm_ffn, gla_w_in, gla_w_gate, gla_b_gate,
           gla_g_out, gla_w_out, mla_w_in, mla_g_q, mla_w_q_up, mla_g_kv, mla_w_kv_up, mla_w_out,
           ffn_w_in, ffn_w_out, final_norm):
    batch, _, d = x.shape
    depth = ada_w.shape[0]
    assert batch <= 8
    c_pad = jnp.pad(c, ((0, 8 - batch), (0, 0)))
    mod_all = _adaln(c_pad, ada_w, ada_b)[:, :batch].reshape(depth, batch, 6, 1, d)
    cos_t, sin_t = _rope_tables(positions)

    for i in range(depth):
        mod = mod_all[i]
        j = i // 2
        if i % 2 == 0:
            q, k, v, r, la = _gla_in(x, mod, norm_mix[i], gla_w_in[j], gla_w_gate[j], gla_b_gate[j])
            mixed = _gla(q, k, v, r, la, gla_g_out[j])
            x = _out_res(mixed, gla_w_out[j], x, mod, 2)
        else:
            q, kc, v = _mla_in(x, mod, norm_mix[i], cos_t, sin_t, mla_w_in[j], mla_g_q[j],
                               mla_w_q_up[j], mla_g_kv[j], mla_w_kv_up[j])
            mixed = _attention(q, kc, v)
            x = _out_res(mixed, mla_w_out[j], x, mod, 2)
        act = _ffn_in(x, mod, norm_ffn[i], ffn_w_in[i])
        last = i == depth - 1
        x = _out_res(act, ffn_w_out[i], x, mod, 5, final_gain=final_norm if last else None)
    return x
```

```python
import functools
import math

import jax
import jax.numpy as jnp
from jax import lax
from jax.experimental import pallas as pl
from jax.experimental.pallas import tpu as pltpu

F32 = jnp.float32
BF16 = jnp.bfloat16

D_MODEL = 1024
EPS = 1e-6

GLA_HEADS = 4
GLA_DK = 128
GLA_DV = 256
GLA_GATE_RANK = 16
GLA_TAU = 16.0
GLA_CHUNK = 64

MLA_HEADS = 8
MLA_NOPE = 128
MLA_ROPE = 64
MLA_VDIM = 128
MLA_Q_RANK = 384
MLA_KV_RANK = 256
ROPE_THETA = 10000.0
MLA_QK_PAD = 256
MLA_VT_ROWS = MLA_VDIM + 16

D_FF = 2816

LANE = 128
ROW_TILE = 512
GLA_TIME_BLOCK = 512
ATT_BLOCK = 512
ATT_Q_STRIP = 512
FFN_COL_CHUNK = 256
VMEM_LIMIT = 56 * 1024 * 1024


def _cparams(sem):
    return pltpu.CompilerParams(dimension_semantics=sem, vmem_limit_bytes=VMEM_LIMIT)


def _dot(a, b):
    return jnp.dot(a, b, preferred_element_type=F32)


def _dot_nt(a, b):
    return lax.dot_general(a, b, (((1,), (1,)), ((), ())), preferred_element_type=F32)


def _sigmoid(x):
    return 1.0 / (1.0 + jnp.exp(-x))


def _rms(x, g):
    return x * lax.rsqrt(jnp.mean(x * x, axis=-1, keepdims=True) + EPS) * g


def _norm_mod(x, g, shift, scale):
    return _rms(x, g) * (1.0 + scale) + shift


def _adaln_kernel(c_ref, w_ref, b_ref, o_ref):
    c = c_ref[...]
    o_ref[...] = _dot(c * _sigmoid(c), w_ref[...]) + b_ref[...]


def _adaln(c_pad, ada_w, ada_b):
    depth, d, n = ada_w.shape
    tn = 1536
    return pl.pallas_call(
        _adaln_kernel,
        grid=(depth, n // tn),
        in_specs=[
            pl.BlockSpec((8, d), lambda l, j: (0, 0)),
            pl.BlockSpec((None, d, tn), lambda l, j: (l, 0, j)),
            pl.BlockSpec((None, 1, tn), lambda l, j: (l, 0, j)),
        ],
        out_specs=pl.BlockSpec((None, 8, tn), lambda l, j: (l, 0, j)),
        out_shape=jax.ShapeDtypeStruct((depth, 8, n), F32),
        compiler_params=_cparams(("arbitrary", "arbitrary")),
        name="adaln",
    )(c_pad, ada_w, ada_b.reshape(depth, 1, n))


def _rope_table_kernel(pos_ref, f_ref, cos_ref, sin_ref):
    ang = pos_ref[...].astype(F32) * f_ref[...]
    cos_ref[...] = jnp.cos(ang)
    sin_ref[...] = jnp.sin(ang)


def _rope_tables(positions):
    b, s = positions.shape
    half = MLA_ROPE // 2
    per_row = LANE // half
    inv_freq = ROPE_THETA ** (-jnp.arange(0, MLA_ROPE, 2, dtype=F32) / MLA_ROPE)
    pos_dense = jnp.repeat(positions, half, axis=-1).reshape(b, s // per_row, LANE)
    f_dense = jnp.tile(inv_freq, per_row).reshape(1, LANE)
    spec = pl.BlockSpec((None, s // per_row, LANE), lambda i: (i, 0, 0))
    cos, sin = pl.pallas_call(
        _rope_table_kernel,
        grid=(b,),
        in_specs=[spec, pl.BlockSpec((1, LANE), lambda i: (0, 0))],
        out_specs=[spec, spec],
        out_shape=[jax.ShapeDtypeStruct((b, s // per_row, LANE), F32)] * 2,
        compiler_params=_cparams(("arbitrary",)),
        name="rope_tables",
    )(pos_dense, f_dense)
    cos = cos.reshape(b, s, half)
    sin = sin.reshape(b, s, half)
    zero = jnp.zeros((b, s, LANE - 2 * half), F32)
    return (jnp.concatenate([cos, cos, zero], axis=-1),
            jnp.concatenate([-sin, sin, zero], axis=-1))


def _row_spec(width):
    return pl.BlockSpec((None, ROW_TILE, width), lambda b, i: (b, i, 0))


def _mod_spec(k):
    return pl.BlockSpec((None, None, 1, D_MODEL), lambda b, i, k=k: (b, k, 0, 0))


def _const_spec(shape):
    return pl.BlockSpec(shape, lambda b, i: (0,) * len(shape))


def _gla_in_kernel(x_ref, g_ref, sh_ref, sc_ref, wq_ref, wk_ref, wv_ref, wr_ref,
                   wl_ref, wgate_ref, bgate_ref, q_ref, k_ref, v_ref, r_ref, la_ref):
    h = _norm_mod(x_ref[...], g_ref[...], sh_ref[...], sc_ref[...]).astype(BF16)
    q_ref[...] = (_dot(h, wq_ref[...]) * (GLA_DK ** -0.5)).astype(BF16)
    k_ref[...] = _dot(h, wk_ref[...]).astype(BF16)
    v_ref[...] = _dot(h, wv_ref[...]).astype(BF16)
    r_ref[...] = _dot(h, wr_ref[...]).astype(BF16)
    low_rank = _dot(h, wl_ref[...]).astype(BF16)
    z = _dot(low_rank, wgate_ref[...]) + bgate_ref[...]
    log_sig = jnp.minimum(z, 0.0) - jnp.log(1.0 + jnp.exp(-jnp.abs(z)))
    la_ref[...] = log_sig / GLA_TAU


def _gla_in(x, mod, gain, w_in, w_gate, b_gate):
    b, s, d = x.shape
    hk = GLA_HEADS * GLA_DK
    hv = GLA_HEADS * GLA_DV
    w = w_in.astype(BF16)
    wq, wk = w[:, :hk], w[:, hk:2 * hk]
    wv, wr = w[:, 2 * hk:2 * hk + hv], w[:, 2 * hk + hv:2 * hk + 2 * hv]
    wl = jnp.pad(w[:, 2 * hk + 2 * hv:], ((0, 0), (0, LANE - GLA_GATE_RANK)))
    wgate = jnp.pad(w_gate.astype(BF16), ((0, LANE - GLA_GATE_RANK), (0, 0)))
    return pl.pallas_call(
        _gla_in_kernel,
        grid=(b, s // ROW_TILE),
        in_specs=[
            _row_spec(d), _const_spec((1, d)), _mod_spec(0), _mod_spec(1),
            _const_spec((d, hk)), _const_spec((d, hk)), _const_spec((d, hv)),
            _const_spec((d, hv)), _const_spec((d, LANE)), _const_spec((LANE, hk)),
            _const_spec((1, hk)),
        ],
        out_specs=[_row_spec(hk), _row_spec(hk), _row_spec(hv), _row_spec(hv), _row_spec(hk)],
        out_shape=[
            jax.ShapeDtypeStruct((b, s, hk), BF16), jax.ShapeDtypeStruct((b, s, hk), BF16),
            jax.ShapeDtypeStruct((b, s, hv), BF16), jax.ShapeDtypeStruct((b, s, hv), BF16),
            jax.ShapeDtypeStruct((b, s, hk), F32),
        ],
        compiler_params=_cparams(("arbitrary", "arbitrary")),
        name="gla_in",
    )(x, gain.reshape(1, d), mod, mod, wq, wk, wv, wr, wl, wgate, b_gate.reshape(1, hk))


def _gla_kernel(q_ref, k_ref, v_ref, r_ref, la_ref, go_ref, o_ref, state_ref):
    c = GLA_CHUNK

    @pl.when(pl.program_id(2) == 0)
    def _():
        state_ref[...] = jnp.zeros_like(state_ref)

    row = lax.broadcasted_iota(jnp.int32, (c, c), 0)
    col = lax.broadcasted_iota(jnp.int32, (c, c), 1)
    causal = row >= col
    tri = jnp.where(causal, 1.0, 0.0).astype(BF16)

    for ci in range(GLA_TIME_BLOCK // c):
        sl = slice(ci * c, (ci + 1) * c)
        la = la_ref[sl, :]
        la_hi = la.astype(BF16)
        la_lo = (la - la_hi.astype(F32)).astype(BF16)
        b = _dot(tri, la_hi) + _dot(tri, la_lo)
        q = q_ref[sl, :].astype(F32)
        k = k_ref[sl, :].astype(F32)
        v = v_ref[sl, :]
        qe = (q * jnp.exp(b)).astype(BF16)
        ke = (k * jnp.exp(-b)).astype(BF16)
        state = state_ref[...]
        attn = jnp.where(causal, _dot_nt(qe, ke), 0.0)
        o = _dot(qe, state.astype(BF16)) + _dot(attn.astype(BF16), v)
        b_t = b.T
        b_last = b_t[:, c - 1:c]
        kd_t = (k.T * jnp.exp(b_last - b_t)).astype(BF16)
        state_ref[...] = state * jnp.exp(b_last) + _dot(kd_t, v)
        r = r_ref[sl, :].astype(F32)
        o_ref[sl, :] = (_rms(o, go_ref[...]) * (r * _sigmoid(r))).astype(BF16)


def _gla(q, k, v, r, la, g_out):
    b, s, _ = q.shape
    t = GLA_TIME_BLOCK

    def spec(width):
        return pl.BlockSpec((None, t, width), lambda bi, h, ti: (bi, ti, h))

    return pl.pallas_call(
        _gla_kernel,
        grid=(b, GLA_HEADS, s // t),
        in_specs=[spec(GLA_DK), spec(GLA_DK), spec(GLA_DV), spec(GLA_DV), spec(GLA_DK),
                  pl.BlockSpec((1, GLA_DV), lambda bi, h, ti: (0, 0))],
        out_specs=spec(GLA_DV),
        out_shape=jax.ShapeDtypeStruct((b, s, GLA_HEADS * GLA_DV), BF16),
        scratch_shapes=[pltpu.VMEM((GLA_DK, GLA_DV), F32)],
        compiler_params=_cparams(("arbitrary", "arbitrary", "arbitrary")),
        name="gla_scan",
    )(q, k, v, r, la, g_out.reshape(1, GLA_DV))


def _out_res_kernel(a_ref, w_ref, x_ref, gate_ref, *rest, final_norm):
    o_ref = rest[-1]
    y = x_ref[...] + gate_ref[...] * _dot(a_ref[...], w_ref[...])
    if final_norm:
        y = _rms(y, rest[0][...])
    o_ref[...] = y


def _out_res(a, w, x, mod, gate_idx, final_gain=None):
    b, s, d = x.shape
    kdim = a.shape[-1]
    in_specs = [_row_spec(kdim), _const_spec((kdim, d)), _row_spec(d), _mod_spec(gate_idx)]
    args = [a, w.astype(BF16), x, mod]
    if final_gain is not None:
        in_specs.append(_const_spec((1, d)))
        args.append(final_gain.reshape(1, d))
    return pl.pallas_call(
        functools.partial(_out_res_kernel, final_norm=final_gain is not None),
        grid=(b, s // ROW_TILE),
        in_specs=in_specs,
        out_specs=_row_spec(d),
        out_shape=jax.ShapeDtypeStruct((b, s, d), F32),
        compiler_params=_cparams(("arbitrary", "arbitrary")),
        name="out_res",
    )(*args)


def _ffn_in_kernel(x_ref, g_ref, sh_ref, sc_ref, wg_ref, wu_ref, a_ref):
    h = _norm_mod(x_ref[...], g_ref[...], sh_ref[...], sc_ref[...]).astype(BF16)
    for j in range(0, D_FF, FFN_COL_CHUNK):
        cols = slice(j, j + FFN_COL_CHUNK)
        gate = _dot(h, wg_ref[:, cols])
        up = _dot(h, wu_ref[:, cols])
        a_ref[:, cols] = (gate * _sigmoid(gate) * up).astype(BF16)


def _ffn_in(x, mod, gain, w_in):
    b, s, d = x.shape
    w = w_in.astype(BF16)
    return pl.pallas_call(
        _ffn_in_kernel,
        grid=(b, s // ROW_TILE),
        in_specs=[_row_spec(d), _const_spec((1, d)), _mod_spec(3), _mod_spec(4),
                  _const_spec((d, D_FF)), _const_spec((d, D_FF))],
        out_specs=_row_spec(D_FF),
        out_shape=jax.ShapeDtypeStruct((b, s, D_FF), BF16),
        compiler_params=_cparams(("arbitrary", "arbitrary")),
        name="ffn_in",
    )(x, gain.reshape(1, d), mod, mod, w[:, :D_FF], w[:, D_FF:])


def _mla_in_kernel(x_ref, g_ref, sh_ref, sc_ref, win_ref, gq_ref, wq_ref, gkv_ref,
                   wkn_ref, wvt_ref, cos_ref, sin_ref, q_ref, kc_ref, vt_ref, *, q_scale):
    h = _norm_mod(x_ref[...], g_ref[...], sh_ref[...], sc_ref[...]).astype(BF16)
    proj = _dot(h, win_ref[...])
    cq = _rms(proj[:, :MLA_Q_RANK], gq_ref[...]).astype(BF16)
    ckv = _rms(proj[:, MLA_Q_RANK:MLA_Q_RANK + MLA_KV_RANK], gkv_ref[...]).astype(BF16)
    cos = cos_ref[...]
    sin = sin_ref[...]

    def rope(t):
        return t * cos + pltpu.roll(t, MLA_ROPE, axis=1) * sin

    k_rope = rope(proj[:, MLA_Q_RANK + MLA_KV_RANK:]).astype(BF16)
    v_t = _dot_nt(wvt_ref[...], ckv)
    ones = jnp.ones((MLA_VT_ROWS - MLA_VDIM, v_t.shape[1]), BF16)
    k_nope = _dot(ckv, wkn_ref[...])
    q_all = _dot(cq, wq_ref[...])
    for hd in range(MLA_HEADS):
        lo = hd * MLA_QK_PAD
        mid = lo + MLA_NOPE
        hi = lo + MLA_QK_PAD
        q_ref[:, lo:mid] = (q_all[:, lo:mid] * q_scale).astype(BF16)
        q_ref[:, mid:hi] = (rope(q_all[:, mid:hi]) * q_scale).astype(BF16)
        kc_ref[:, lo:mid] = k_nope[:, hd * MLA_NOPE:(hd + 1) * MLA_NOPE].astype(BF16)
        kc_ref[:, mid:hi] = k_rope
        vt_ref[hd, :MLA_VDIM, :] = v_t[hd * MLA_VDIM:(hd + 1) * MLA_VDIM, :].astype(BF16)
        vt_ref[hd, MLA_VDIM:, :] = ones


def _half_split(w_rope):
    even, odd = w_rope[..., 0::2], w_rope[..., 1::2]
    return jnp.concatenate([even, odd, odd, even], axis=-1)


def _mla_in(x, mod, gain, cos_t, sin_t, w_in, g_q, w_q_up, g_kv, w_kv_up):
    b, s, d = x.shape
    hq = MLA_HEADS * MLA_QK_PAD
    hn = MLA_HEADS * MLA_NOPE
    hv = MLA_HEADS * MLA_VDIM
    lat = MLA_Q_RANK + MLA_KV_RANK
    win = jnp.concatenate([w_in[:, :lat], _half_split(w_in[:, lat:])], axis=-1).astype(BF16)
    wq3 = w_q_up.reshape(MLA_Q_RANK, MLA_HEADS, MLA_NOPE + MLA_ROPE)
    wq = jnp.concatenate([wq3[..., :MLA_NOPE], _half_split(wq3[..., MLA_NOPE:])], axis=-1)
    wq = wq.reshape(MLA_Q_RANK, hq).astype(BF16)
    wkv3 = w_kv_up.reshape(MLA_KV_RANK, MLA_HEADS, MLA_NOPE + MLA_VDIM)
    wkn = wkv3[..., :MLA_NOPE].reshape(MLA_KV_RANK, hn).astype(BF16)
    wvt = wkv3[..., MLA_NOPE:].reshape(MLA_KV_RANK, hv).T.astype(BF16)
    win_w = win.shape[-1]
    q_scale = (MLA_NOPE + MLA_ROPE) ** -0.5 * math.log2(math.e)
    return pl.pallas_call(
        functools.partial(_mla_in_kernel, q_scale=q_scale),
        grid=(b, s // ROW_TILE),
        in_specs=[
            _row_spec(d), _const_spec((1, d)), _mod_spec(0), _mod_spec(1),
            _const_spec((d, win_w)), _const_spec((1, MLA_Q_RANK)), _const_spec((MLA_Q_RANK, hq)),
            _const_spec((1, MLA_KV_RANK)), _const_spec((MLA_KV_RANK, hn)),
            _const_spec((hv, MLA_KV_RANK)), _row_spec(LANE), _row_spec(LANE),
        ],
        out_specs=[_row_spec(hq), _row_spec(hq),
                   pl.BlockSpec((None, MLA_HEADS, MLA_VT_ROWS, ROW_TILE), lambda bi, i: (bi, 0, 0, i))],
        out_shape=[jax.ShapeDtypeStruct((b, s, hq), BF16), jax.ShapeDtypeStruct((b, s, hq), BF16),
                   jax.ShapeDtypeStruct((b, MLA_HEADS, MLA_VT_ROWS, s), BF16)],
        compiler_params=_cparams(("arbitrary", "arbitrary")),
        name="mla_in",
    )(x, gain.reshape(1, d), mod, mod, win, g_q.reshape(1, -1), wq, g_kv.reshape(1, -1),
      wkn, wvt, cos_t, sin_t)


def _attn_kernel(q_ref, k_ref, vt_ref, o_ref, m_ref, acc_ref):
    t = ATT_BLOCK
    qi = pl.program_id(2)
    q = q_ref[...]
    m_ref[...] = jnp.full_like(m_ref, -jnp.inf)
    acc_ref[...] = jnp.zeros_like(acc_ref)

    def step(kj, masked):
        start = pl.multiple_of(kj * t, t)
        k_blk = k_ref[pl.ds(start, t), :]
        vt_blk = vt_ref[:, pl.ds(start, t)]
        for c0 in range(0, t, ATT_Q_STRIP):
            cols = slice(c0, c0 + ATT_Q_STRIP)
            s_t = _dot_nt(k_blk, q[cols, :])
            if masked:
                key = lax.broadcasted_iota(jnp.int32, (t, ATT_Q_STRIP), 0)
                qry = lax.broadcasted_iota(jnp.int32, (t, ATT_Q_STRIP), 1) + c0
                s_t = jnp.where(key <= qry, s_t, -jnp.inf)
            m_prev = m_ref[:, cols]
            m_new = jnp.maximum(m_prev, jnp.max(s_t, axis=0, keepdims=True))
            p_t = jnp.exp2((s_t - m_new).astype(BF16))
            alpha = jnp.exp2(m_prev - m_new)
            acc_ref[:, cols] = alpha * acc_ref[:, cols] + _dot(vt_blk, p_t)
            m_ref[:, cols] = m_new

    def body(kj, carry):
        step(kj, masked=False)
        return carry

    lax.fori_loop(0, qi, body, 0)
    step(qi, masked=True)
    acc = acc_ref[...]
    o_t = acc[:MLA_VDIM, :] / acc[MLA_VDIM:MLA_VDIM + 1, :]
    o_ref[...] = o_t.T.astype(BF16)


def _attention(q, kc, vt):
    b, _, _, s = vt.shape
    t = ATT_BLOCK
    return pl.pallas_call(
        _attn_kernel,
        grid=(b, MLA_HEADS, s // t),
        in_specs=[
            pl.BlockSpec((None, t, MLA_QK_PAD), lambda bi, h, i: (bi, i, h)),
            pl.BlockSpec((None, s, MLA_QK_PAD), lambda bi, h, i: (bi, 0, h)),
            pl.BlockSpec((None, None, MLA_VT_ROWS, s), lambda bi, h, i: (bi, h, 0, 0)),
        ],
        out_specs=pl.BlockSpec((None, t, MLA_VDIM), lambda bi, h, i: (bi, i, h)),
        out_shape=jax.ShapeDtypeStruct((b, s, MLA_HEADS * MLA_VDIM), BF16),
        scratch_shapes=[pltpu.VMEM((1, t), F32), pltpu.VMEM((MLA_VT_ROWS, t), F32)],
        compiler_params=_cparams(("arbitrary", "arbitrary", "arbitrary")),
        name="mla_attention",
    )(q, kc, vt)


def kernel(x, c, positions, ada_w, ada_b, norm_mix, norm_ffn, gla_w_in, gla_w_gate, gla_b_gate,
           gla_g_out, gla_w_out, mla_w_in, mla_g_q, mla_w_q_up, mla_g_kv, mla_w_kv_up, mla_w_out,
           ffn_w_in, ffn_w_out, final_norm):
    batch, _, d = x.shape
    depth = ada_w.shape[0]
    assert batch <= 8
    c_pad = jnp.pad(c, ((0, 8 - batch), (0, 0)))
    mod_all = _adaln(c_pad, ada_w, ada_b)[:, :batch].reshape(depth, batch, 6, 1, d)
    cos_t, sin_t = _rope_tables(positions)

    for i in range(depth):
        mod = mod_all[i]
        j = i // 2
        if i % 2 == 0:
            q, k, v, r, la = _gla_in(x, mod, norm_mix[i], gla_w_in[j], gla_w_gate[j], gla_b_gate[j])
            mixed = _gla(q, k, v, r, la, gla_g_out[j])
            x = _out_res(mixed, gla_w_out[j], x, mod, 2)
        else:
            q, kc, vt = _mla_in(x, mod, norm_mix[i], cos_t, sin_t, mla_w_in[j], mla_g_q[j],
                                mla_w_q_up[j], mla_g_kv[j], mla_w_kv_up[j])
            mixed = _attention(q, kc, vt)
            x = _out_res(mixed, mla_w_out[j], x, mod, 2)
        act = _ffn_in(x, mod, norm_ffn[i], ffn_w_in[i])
        last = i == depth - 1
        x = _out_res(act, ffn_w_out[i], x, mod, 5, final_gain=final_norm if last else None)
    return x
```

```python
import functools
import math

import jax
import jax.numpy as jnp
from jax import lax
from jax.experimental import pallas as pl
from jax.experimental.pallas import tpu as pltpu

F32 = jnp.float32
BF16 = jnp.bfloat16

D_MODEL = 1024
EPS = 1e-6

GLA_HEADS = 4
GLA_DK = 128
GLA_DV = 256
GLA_GATE_RANK = 16
GLA_TAU = 16.0
GLA_CHUNK = 64

MLA_HEADS = 8
MLA_NOPE = 128
MLA_ROPE = 64
MLA_VDIM = 128
MLA_Q_RANK = 384
MLA_KV_RANK = 256
ROPE_THETA = 10000.0
MLA_QK_PAD = 256
MLA_VT_ROWS = MLA_VDIM + 16

D_FF = 2816

LANE = 128
ROW_TILE = 512
GLA_TIME_BLOCK = 512
ATT_BLOCK = 512
FFN_COL_CHUNK = 256
VMEM_LIMIT = 56 * 1024 * 1024


def _cparams(sem):
    return pltpu.CompilerParams(dimension_semantics=sem, vmem_limit_bytes=VMEM_LIMIT)


def _dot(a, b):
    return jnp.dot(a, b, preferred_element_type=F32)


def _dot_nt(a, b):
    return lax.dot_general(a, b, (((1,), (1,)), ((), ())), preferred_element_type=F32)


def _sigmoid(x):
    return 1.0 / (1.0 + jnp.exp(-x))


def _rms(x, g):
    return x * lax.rsqrt(jnp.mean(x * x, axis=-1, keepdims=True) + EPS) * g


def _norm_mod(x, g, shift, scale):
    return _rms(x, g) * (1.0 + scale) + shift


def _adaln_kernel(c_ref, w_ref, b_ref, o_ref):
    c = c_ref[...]
    o_ref[...] = _dot(c * _sigmoid(c), w_ref[...]) + b_ref[...]


def _adaln(c_pad, ada_w, ada_b):
    depth, d, n = ada_w.shape
    tn = 1536
    return pl.pallas_call(
        _adaln_kernel,
        grid=(depth, n // tn),
        in_specs=[
            pl.BlockSpec((8, d), lambda l, j: (0, 0)),
            pl.BlockSpec((None, d, tn), lambda l, j: (l, 0, j)),
            pl.BlockSpec((None, 1, tn), lambda l, j: (l, 0, j)),
        ],
        out_specs=pl.BlockSpec((None, 8, tn), lambda l, j: (l, 0, j)),
        out_shape=jax.ShapeDtypeStruct((depth, 8, n), F32),
        compiler_params=_cparams(("arbitrary", "arbitrary")),
        name="adaln",
    )(c_pad, ada_w, ada_b.reshape(depth, 1, n))


def _rope_table_kernel(pos_ref, f_ref, cos_ref, sin_ref):
    ang = pos_ref[...].astype(F32) * f_ref[...]
    cos_ref[...] = jnp.cos(ang)
    sin_ref[...] = jnp.sin(ang)


def _rope_tables(positions):
    b, s = positions.shape
    half = MLA_ROPE // 2
    per_row = LANE // half
    inv_freq = ROPE_THETA ** (-jnp.arange(0, MLA_ROPE, 2, dtype=F32) / MLA_ROPE)
    pos_dense = jnp.repeat(positions, half, axis=-1).reshape(b, s // per_row, LANE)
    f_dense = jnp.tile(inv_freq, per_row).reshape(1, LANE)
    spec = pl.BlockSpec((None, s // per_row, LANE), lambda i: (i, 0, 0))
    cos, sin = pl.pallas_call(
        _rope_table_kernel,
        grid=(b,),
        in_specs=[spec, pl.BlockSpec((1, LANE), lambda i: (0, 0))],
        out_specs=[spec, spec],
        out_shape=[jax.ShapeDtypeStruct((b, s // per_row, LANE), F32)] * 2,
        compiler_params=_cparams(("arbitrary",)),
        name="rope_tables",
    )(pos_dense, f_dense)
    cos = cos.reshape(b, s, half)
    sin = sin.reshape(b, s, half)
    zero = jnp.zeros((b, s, LANE - 2 * half), F32)
    return (jnp.concatenate([cos, cos, zero], axis=-1),
            jnp.concatenate([-sin, sin, zero], axis=-1))


def _row_spec(width):
    return pl.BlockSpec((None, ROW_TILE, width), lambda b, i: (b, i, 0))


def _mod_spec(k):
    return pl.BlockSpec((None, None, 1, D_MODEL), lambda b, i, k=k: (b, k, 0, 0))


def _const_spec(shape):
    return pl.BlockSpec(shape, lambda b, i: (0,) * len(shape))


def _gla_in_kernel(x_ref, g_ref, sh_ref, sc_ref, wq_ref, wk_ref, wv_ref, wr_ref,
                   wl_ref, wgate_ref, bgate_ref, q_ref, k_ref, v_ref, r_ref, la_ref):
    h = _norm_mod(x_ref[...], g_ref[...], sh_ref[...], sc_ref[...]).astype(BF16)
    q_ref[...] = (_dot(h, wq_ref[...]) * (GLA_DK ** -0.5)).astype(BF16)
    k_ref[...] = _dot(h, wk_ref[...]).astype(BF16)
    v_ref[...] = _dot(h, wv_ref[...]).astype(BF16)
    r_ref[...] = _dot(h, wr_ref[...]).astype(BF16)
    low_rank = _dot(h, wl_ref[...]).astype(BF16)
    z = _dot(low_rank, wgate_ref[...]) + bgate_ref[...]
    log_sig = jnp.minimum(z, 0.0) - jnp.log(1.0 + jnp.exp(-jnp.abs(z)))
    la_ref[...] = log_sig / GLA_TAU


def _gla_in(x, mod, gain, w_in, w_gate, b_gate):
    b, s, d = x.shape
    hk = GLA_HEADS * GLA_DK
    hv = GLA_HEADS * GLA_DV
    w = w_in.astype(BF16)
    wq, wk = w[:, :hk], w[:, hk:2 * hk]
    wv, wr = w[:, 2 * hk:2 * hk + hv], w[:, 2 * hk + hv:2 * hk + 2 * hv]
    wl = jnp.pad(w[:, 2 * hk + 2 * hv:], ((0, 0), (0, LANE - GLA_GATE_RANK)))
    wgate = jnp.pad(w_gate.astype(BF16), ((0, LANE - GLA_GATE_RANK), (0, 0)))
    return pl.pallas_call(
        _gla_in_kernel,
        grid=(b, s // ROW_TILE),
        in_specs=[
            _row_spec(d), _const_spec((1, d)), _mod_spec(0), _mod_spec(1),
            _const_spec((d, hk)), _const_spec((d, hk)), _const_spec((d, hv)),
            _const_spec((d, hv)), _const_spec((d, LANE)), _const_spec((LANE, hk)),
            _const_spec((1, hk)),
        ],
        out_specs=[_row_spec(hk), _row_spec(hk), _row_spec(hv), _row_spec(hv), _row_spec(hk)],
        out_shape=[
            jax.ShapeDtypeStruct((b, s, hk), BF16), jax.ShapeDtypeStruct((b, s, hk), BF16),
            jax.ShapeDtypeStruct((b, s, hv), BF16), jax.ShapeDtypeStruct((b, s, hv), BF16),
            jax.ShapeDtypeStruct((b, s, hk), F32),
        ],
        compiler_params=_cparams(("arbitrary", "arbitrary")),
        name="gla_in",
    )(x, gain.reshape(1, d), mod, mod, wq, wk, wv, wr, wl, wgate, b_gate.reshape(1, hk))


def _gla_kernel(q_ref, k_ref, v_ref, r_ref, la_ref, go_ref, o_ref, state_ref):
    c = GLA_CHUNK

    @pl.when(pl.program_id(2) == 0)
    def _():
        state_ref[...] = jnp.zeros_like(state_ref)

    row = lax.broadcasted_iota(jnp.int32, (c, c), 0)
    col = lax.broadcasted_iota(jnp.int32, (c, c), 1)
    causal = row >= col
    tri = jnp.where(causal, 1.0, 0.0).astype(BF16)

    for ci in range(GLA_TIME_BLOCK // c):
        sl = slice(ci * c, (ci + 1) * c)
        la = la_ref[sl, :]
        la_hi = la.astype(BF16)
        la_lo = (la - la_hi.astype(F32)).astype(BF16)
        b = _dot(tri, la_hi) + _dot(tri, la_lo)
        q = q_ref[sl, :].astype(F32)
        k = k_ref[sl, :].astype(F32)
        v = v_ref[sl, :]
        qe = (q * jnp.exp(b)).astype(BF16)
        ke = (k * jnp.exp(-b)).astype(BF16)
        state = state_ref[...]
        attn = jnp.where(causal, _dot_nt(qe, ke), 0.0)
        o = _dot(qe, state.astype(BF16)) + _dot(attn.astype(BF16), v)
        b_t = b.T
        b_last = b_t[:, c - 1:c]
        kd_t = (k.T * jnp.exp(b_last - b_t)).astype(BF16)
        state_ref[...] = state * jnp.exp(b_last) + _dot(kd_t, v)
        r = r_ref[sl, :].astype(F32)
        o_ref[sl, :] = (_rms(o, go_ref[...]) * (r * _sigmoid(r))).astype(BF16)


def _gla(q, k, v, r, la, g_out):
    b, s, _ = q.shape
    t = GLA_TIME_BLOCK

    def spec(width):
        return pl.BlockSpec((None, t, width), lambda bi, h, ti: (bi, ti, h))

    return pl.pallas_call(
        _gla_kernel,
        grid=(b, GLA_HEADS, s // t),
        in_specs=[spec(GLA_DK), spec(GLA_DK), spec(GLA_DV), spec(GLA_DV), spec(GLA_DK),
                  pl.BlockSpec((1, GLA_DV), lambda bi, h, ti: (0, 0))],
        out_specs=spec(GLA_DV),
        out_shape=jax.ShapeDtypeStruct((b, s, GLA_HEADS * GLA_DV), BF16),
        scratch_shapes=[pltpu.VMEM((GLA_DK, GLA_DV), F32)],
        compiler_params=_cparams(("arbitrary", "arbitrary", "arbitrary")),
        name="gla_scan",
    )(q, k, v, r, la, g_out.reshape(1, GLA_DV))


def _out_res_kernel(a_ref, w_ref, x_ref, gate_ref, *rest, final_norm):
    o_ref = rest[-1]
    y = x_ref[...] + gate_ref[...] * _dot(a_ref[...], w_ref[...])
    if final_norm:
        y = _rms(y, rest[0][...])
    o_ref[...] = y


def _out_res(a, w, x, mod, gate_idx, final_gain=None):
    b, s, d = x.shape
    kdim = a.shape[-1]
    in_specs = [_row_spec(kdim), _const_spec((kdim, d)), _row_spec(d), _mod_spec(gate_idx)]
    args = [a, w.astype(BF16), x, mod]
    if final_gain is not None:
        in_specs.append(_const_spec((1, d)))
        args.append(final_gain.reshape(1, d))
    return pl.pallas_call(
        functools.partial(_out_res_kernel, final_norm=final_gain is not None),
        grid=(b, s // ROW_TILE),
        in_specs=in_specs,
        out_specs=_row_spec(d),
        out_shape=jax.ShapeDtypeStruct((b, s, d), F32),
        compiler_params=_cparams(("arbitrary", "arbitrary")),
        name="out_res",
    )(*args)


def _ffn_in_kernel(x_ref, g_ref, sh_ref, sc_ref, wg_ref, wu_ref, a_ref):
    h = _norm_mod(x_ref[...], g_ref[...], sh_ref[...], sc_ref[...]).astype(BF16)
    for j in range(0, D_FF, FFN_COL_CHUNK):
        cols = slice(j, j + FFN_COL_CHUNK)
        gate = _dot(h, wg_ref[:, cols])
        up = _dot(h, wu_ref[:, cols])
        a_ref[:, cols] = (gate * _sigmoid(gate) * up).astype(BF16)


def _ffn_in(x, mod, gain, w_in):
    b, s, d = x.shape
    w = w_in.astype(BF16)
    return pl.pallas_call(
        _ffn_in_kernel,
        grid=(b, s // ROW_TILE),
        in_specs=[_row_spec(d), _const_spec((1, d)), _mod_spec(3), _mod_spec(4),
                  _const_spec((d, D_FF)), _const_spec((d, D_FF))],
        out_specs=_row_spec(D_FF),
        out_shape=jax.ShapeDtypeStruct((b, s, D_FF), BF16),
        compiler_params=_cparams(("arbitrary", "arbitrary")),
        name="ffn_in",
    )(x, gain.reshape(1, d), mod, mod, w[:, :D_FF], w[:, D_FF:])


def _mla_in_kernel(x_ref, g_ref, sh_ref, sc_ref, win_ref, gq_ref, wq_ref, gkv_ref,
                   wkn_ref, wvt_ref, cos_ref, sin_ref, q_ref, kc_ref, vt_ref, *, q_scale):
    h = _norm_mod(x_ref[...], g_ref[...], sh_ref[...], sc_ref[...]).astype(BF16)
    proj = _dot(h, win_ref[...])
    cq = _rms(proj[:, :MLA_Q_RANK], gq_ref[...]).astype(BF16)
    ckv = _rms(proj[:, MLA_Q_RANK:MLA_Q_RANK + MLA_KV_RANK], gkv_ref[...]).astype(BF16)
    cos = cos_ref[...]
    sin = sin_ref[...]

    def rope(t):
        return t * cos + pltpu.roll(t, MLA_ROPE, axis=1) * sin

    k_rope = rope(proj[:, MLA_Q_RANK + MLA_KV_RANK:]).astype(BF16)
    v_t = _dot_nt(wvt_ref[...], ckv)
    ones = jnp.ones((MLA_VT_ROWS - MLA_VDIM, v_t.shape[1]), BF16)
    k_nope = _dot(ckv, wkn_ref[...])
    q_all = _dot(cq, wq_ref[...])
    for hd in range(MLA_HEADS):
        lo = hd * MLA_QK_PAD
        mid = lo + MLA_NOPE
        hi = lo + MLA_QK_PAD
        q_ref[:, lo:mid] = (q_all[:, lo:mid] * q_scale).astype(BF16)
        q_ref[:, mid:hi] = (rope(q_all[:, mid:hi]) * q_scale).astype(BF16)
        kc_ref[:, lo:mid] = k_nope[:, hd * MLA_NOPE:(hd + 1) * MLA_NOPE].astype(BF16)
        kc_ref[:, mid:hi] = k_rope
        vt_ref[hd, :MLA_VDIM, :] = v_t[hd * MLA_VDIM:(hd + 1) * MLA_VDIM, :].astype(BF16)
        vt_ref[hd, MLA_VDIM:, :] = ones


def _half_split(w_rope):
    even, odd = w_rope[..., 0::2], w_rope[..., 1::2]
    return jnp.concatenate([even, odd, odd, even], axis=-1)


def _mla_in(x, mod, gain, cos_t, sin_t, w_in, g_q, w_q_up, g_kv, w_kv_up):
    b, s, d = x.shape
    hq = MLA_HEADS * MLA_QK_PAD
    hn = MLA_HEADS * MLA_NOPE
    hv = MLA_HEADS * MLA_VDIM
    lat = MLA_Q_RANK + MLA_KV_RANK
    win = jnp.concatenate([w_in[:, :lat], _half_split(w_in[:, lat:])], axis=-1).astype(BF16)
    wq3 = w_q_up.reshape(MLA_Q_RANK, MLA_HEADS, MLA_NOPE + MLA_ROPE)
    wq = jnp.concatenate([wq3[..., :MLA_NOPE], _half_split(wq3[..., MLA_NOPE:])], axis=-1)
    wq = wq.reshape(MLA_Q_RANK, hq).astype(BF16)
    wkv3 = w_kv_up.reshape(MLA_KV_RANK, MLA_HEADS, MLA_NOPE + MLA_VDIM)
    wkn = wkv3[..., :MLA_NOPE].reshape(MLA_KV_RANK, hn).astype(BF16)
    wvt = wkv3[..., MLA_NOPE:].reshape(MLA_KV_RANK, hv).T.astype(BF16)
    win_w = win.shape[-1]
    q_scale = (MLA_NOPE + MLA_ROPE) ** -0.5 * math.log2(math.e)
    return pl.pallas_call(
        functools.partial(_mla_in_kernel, q_scale=q_scale),
        grid=(b, s // ROW_TILE),
        in_specs=[
            _row_spec(d), _const_spec((1, d)), _mod_spec(0), _mod_spec(1),
            _const_spec((d, win_w)), _const_spec((1, MLA_Q_RANK)), _const_spec((MLA_Q_RANK, hq)),
            _const_spec((1, MLA_KV_RANK)), _const_spec((MLA_KV_RANK, hn)),
            _const_spec((hv, MLA_KV_RANK)), _row_spec(LANE), _row_spec(LANE),
        ],
        out_specs=[_row_spec(hq), _row_spec(hq),
                   pl.BlockSpec((None, MLA_HEADS, MLA_VT_ROWS, ROW_TILE), lambda bi, i: (bi, 0, 0, i))],
        out_shape=[jax.ShapeDtypeStruct((b, s, hq), BF16), jax.ShapeDtypeStruct((b, s, hq), BF16),
                   jax.ShapeDtypeStruct((b, MLA_HEADS, MLA_VT_ROWS, s), BF16)],
        compiler_params=_cparams(("arbitrary", "arbitrary")),
        name="mla_in",
    )(x, gain.reshape(1, d), mod, mod, win, g_q.reshape(1, -1), wq, g_kv.reshape(1, -1),
      wkn, wvt, cos_t, sin_t)


def _attn_kernel(q_ref, k_ref, vt_ref, o_ref, s0_ref, s1_ref, mb0_ref, mb1_ref, m_ref, acc_ref):
    t = ATT_BLOCK
    qi = pl.program_id(2)
    q = q_ref[...]
    m_ref[...] = jnp.full_like(m_ref, -jnp.inf)
    acc_ref[...] = jnp.zeros_like(acc_ref)

    def scores(kj, s_ref, mb_ref, masked=False):
        start = pl.multiple_of(kj * t, t)
        s_t = _dot_nt(k_ref[pl.ds(start, t), :], q)
        if masked:
            key = lax.broadcasted_iota(jnp.int32, (t, t), 0)
            qry = lax.broadcasted_iota(jnp.int32, (t, t), 1)
            s_t = jnp.where(key <= qry, s_t, -jnp.inf)
        s_ref[...] = s_t
        mb_ref[...] = jnp.max(s_t, axis=0, keepdims=True)

    def accumulate(kj, s_ref, mb_ref):
        start = pl.multiple_of(kj * t, t)
        m_prev = m_ref[...]
        m_new = jnp.maximum(m_prev, mb_ref[...])
        p_t = jnp.exp2((s_ref[...] - m_new).astype(BF16))
        alpha = jnp.exp2(m_prev - m_new)
        acc_ref[...] = alpha * acc_ref[...] + _dot(vt_ref[:, pl.ds(start, t)], p_t)
        m_ref[...] = m_new

    scores(qi, s0_ref, mb0_ref, masked=True)

    def pair(p, pending):
        scores(2 * p, s1_ref, mb1_ref)
        accumulate(pending, s0_ref, mb0_ref)
        scores(2 * p + 1, s0_ref, mb0_ref)
        accumulate(2 * p, s1_ref, mb1_ref)
        return 2 * p + 1

    pending = lax.fori_loop(0, qi // 2, pair, qi)

    @pl.when(qi % 2 == 1)
    def _():
        scores(qi - 1, s1_ref, mb1_ref)
        accumulate(pending, s0_ref, mb0_ref)
        accumulate(qi - 1, s1_ref, mb1_ref)

    @pl.when(qi % 2 == 0)
    def _():
        accumulate(pending, s0_ref, mb0_ref)

    acc = acc_ref[...]
    o_t = acc[:MLA_VDIM, :] / acc[MLA_VDIM:MLA_VDIM + 1, :]
    o_ref[...] = o_t.T.astype(BF16)


def _attention(q, kc, vt):
    b, _, _, s = vt.shape
    t = ATT_BLOCK
    return pl.pallas_call(
        _attn_kernel,
        grid=(b, MLA_HEADS, s // t),
        in_specs=[
            pl.BlockSpec((None, t, MLA_QK_PAD), lambda bi, h, i: (bi, i, h)),
            pl.BlockSpec((None, s, MLA_QK_PAD), lambda bi, h, i: (bi, 0, h)),
            pl.BlockSpec((None, None, MLA_VT_ROWS, s), lambda bi, h, i: (bi, h, 0, 0)),
        ],
        out_specs=pl.BlockSpec((None, t, MLA_VDIM), lambda bi, h, i: (bi, i, h)),
        out_shape=jax.ShapeDtypeStruct((b, s, MLA_HEADS * MLA_VDIM), BF16),
        scratch_shapes=[pltpu.VMEM((t, t), F32), pltpu.VMEM((t, t), F32),
                        pltpu.VMEM((1, t), F32), pltpu.VMEM((1, t), F32),
                        pltpu.VMEM((1, t), F32), pltpu.VMEM((MLA_VT_ROWS, t), F32)],
        compiler_params=_cparams(("arbitrary", "arbitrary", "arbitrary")),
        name="mla_attention",
    )(q, kc, vt)


def kernel(x, c, positions, ada_w, ada_b, norm_mix, norm_ffn, gla_w_in, gla_w_gate, gla_b_gate,
           gla_g_out, gla_w_out, mla_w_in, mla_g_q, mla_w_q_up, mla_g_kv, mla_w_kv_up, mla_w_out,
           ffn_w_in, ffn_w_out, final_norm):
    batch, _, d = x.shape
    depth = ada_w.shape[0]
    assert batch <= 8
    c_pad = jnp.pad(c, ((0, 8 - batch), (0, 0)))
    mod_all = _adaln(c_pad, ada_w, ada_b)[:, :batch].reshape(depth, batch, 6, 1, d)
    cos_t, sin_t = _rope_tables(positions)

    for i in range(depth):
        mod = mod_all[i]
        j = i // 2
        if i % 2 == 0:
            q, k, v, r, la = _gla_in(x, mod, norm_mix[i], gla_w_in[j], gla_w_gate[j], gla_b_gate[j])
            mixed = _gla(q, k, v, r, la, gla_g_out[j])
            x = _out_res(mixed, gla_w_out[j], x, mod, 2)
        else:
            q, kc, vt = _mla_in(x, mod, norm_mix[i], cos_t, sin_t, mla_w_in[j], mla_g_q[j],
                                mla_w_q_up[j], mla_g_kv[j], mla_w_kv_up[j])
            mixed = _attention(q, kc, vt)
            x = _out_res(mixed, mla_w_out[j], x, mod, 2)
        act = _ffn_in(x, mod, norm_ffn[i], ffn_w_in[i])
        last = i == depth - 1
        x = _out_res(act, ffn_w_out[i], x, mod, 5, final_gain=final_norm if last else None)
    return x
```

```python
import functools
import math

import jax
import jax.numpy as jnp
from jax import lax
from jax.experimental import pallas as pl
from jax.experimental.pallas import tpu as pltpu

F32 = jnp.float32
BF16 = jnp.bfloat16

D_MODEL = 1024
EPS = 1e-6

GLA_HEADS = 4
GLA_DK = 128
GLA_DV = 256
GLA_GATE_RANK = 16
GLA_TAU = 16.0
GLA_CHUNK = 64
GLA_SUB = 16

MLA_HEADS = 8
MLA_NOPE = 128
MLA_ROPE = 64
MLA_VDIM = 128
MLA_Q_RANK = 384
MLA_KV_RANK = 256
ROPE_THETA = 10000.0
MLA_QK_PAD = 256
MLA_VT_ROWS = MLA_VDIM + 16

D_FF = 2816

LANE = 128
ROW_TILE = 512
GLA_TIME_BLOCK = 512
ATT_BLOCK = 512
FFN_COL_CHUNK = 256
VMEM_LIMIT = 56 * 1024 * 1024


def _cparams(sem):
    return pltpu.CompilerParams(dimension_semantics=sem, vmem_limit_bytes=VMEM_LIMIT)


def _dot(a, b):
    return jnp.dot(a, b, preferred_element_type=F32)


def _dot_nt(a, b):
    return lax.dot_general(a, b, (((1,), (1,)), ((), ())), preferred_element_type=F32)


def _sigmoid(x):
    return 1.0 / (1.0 + jnp.exp(-x))


def _rms(x, g):
    return x * lax.rsqrt(jnp.mean(x * x, axis=-1, keepdims=True) + EPS) * g


def _norm_mod(x, g, shift, scale):
    return _rms(x, g) * (1.0 + scale) + shift


def _adaln_kernel(c_ref, w_ref, b_ref, o_ref):
    c = c_ref[...]
    o_ref[...] = _dot(c * _sigmoid(c), w_ref[...]) + b_ref[...]


def _adaln(c_pad, ada_w, ada_b):
    depth, d, n = ada_w.shape
    tn = 1536
    return pl.pallas_call(
        _adaln_kernel,
        grid=(depth, n // tn),
        in_specs=[
            pl.BlockSpec((8, d), lambda l, j: (0, 0)),
            pl.BlockSpec((None, d, tn), lambda l, j: (l, 0, j)),
            pl.BlockSpec((None, 1, tn), lambda l, j: (l, 0, j)),
        ],
        out_specs=pl.BlockSpec((None, 8, tn), lambda l, j: (l, 0, j)),
        out_shape=jax.ShapeDtypeStruct((depth, 8, n), F32),
        compiler_params=_cparams(("arbitrary", "arbitrary")),
        name="adaln",
    )(c_pad, ada_w, ada_b.reshape(depth, 1, n))


def _rope_table_kernel(pos_ref, f_ref, cos_ref, sin_ref):
    ang = pos_ref[...].astype(F32) * f_ref[...]
    cos_ref[...] = jnp.cos(ang)
    sin_ref[...] = jnp.sin(ang)


def _rope_tables(positions):
    b, s = positions.shape
    half = MLA_ROPE // 2
    per_row = LANE // half
    inv_freq = ROPE_THETA ** (-jnp.arange(0, MLA_ROPE, 2, dtype=F32) / MLA_ROPE)
    pos_dense = jnp.repeat(positions, half, axis=-1).reshape(b, s // per_row, LANE)
    f_dense = jnp.tile(inv_freq, per_row).reshape(1, LANE)
    spec = pl.BlockSpec((None, s // per_row, LANE), lambda i: (i, 0, 0))
    cos, sin = pl.pallas_call(
        _rope_table_kernel,
        grid=(b,),
        in_specs=[spec, pl.BlockSpec((1, LANE), lambda i: (0, 0))],
        out_specs=[spec, spec],
        out_shape=[jax.ShapeDtypeStruct((b, s // per_row, LANE), F32)] * 2,
        compiler_params=_cparams(("arbitrary",)),
        name="rope_tables",
    )(pos_dense, f_dense)
    cos = cos.reshape(b, s, half)
    sin = sin.reshape(b, s, half)
    zero = jnp.zeros((b, s, LANE - 2 * half), F32)
    return (jnp.concatenate([cos, cos, zero], axis=-1),
            jnp.concatenate([-sin, sin, zero], axis=-1))


def _row_spec(width):
    return pl.BlockSpec((None, ROW_TILE, width), lambda b, i: (b, i, 0))


def _mod_spec(k):
    return pl.BlockSpec((None, None, 1, D_MODEL), lambda b, i, k=k: (b, k, 0, 0))


def _const_spec(shape):
    return pl.BlockSpec(shape, lambda b, i: (0,) * len(shape))


def _gla_in_kernel(x_ref, g_ref, sh_ref, sc_ref, wq_ref, wk_ref, wv_ref, wr_ref,
                   wl_ref, wgate_ref, bgate_ref, q_ref, k_ref, v_ref, r_ref, la_ref):
    h = _norm_mod(x_ref[...], g_ref[...], sh_ref[...], sc_ref[...]).astype(BF16)
    q_ref[...] = (_dot(h, wq_ref[...]) * (GLA_DK ** -0.5)).astype(BF16)
    k_ref[...] = _dot(h, wk_ref[...]).astype(BF16)
    v_ref[...] = _dot(h, wv_ref[...]).astype(BF16)
    r_ref[...] = _dot(h, wr_ref[...]).astype(BF16)
    low_rank = _dot(h, wl_ref[...]).astype(BF16)
    z = _dot(low_rank, wgate_ref[...]) + bgate_ref[...]
    log_sig = jnp.minimum(z, 0.0) - jnp.log(1.0 + jnp.exp(-jnp.abs(z)))
    la_ref[...] = log_sig / GLA_TAU


def _gla_in(x, mod, gain, w_in, w_gate, b_gate):
    b, s, d = x.shape
    hk = GLA_HEADS * GLA_DK
    hv = GLA_HEADS * GLA_DV
    w = w_in.astype(BF16)
    wq, wk = w[:, :hk], w[:, hk:2 * hk]
    wv, wr = w[:, 2 * hk:2 * hk + hv], w[:, 2 * hk + hv:2 * hk + 2 * hv]
    wl = jnp.pad(w[:, 2 * hk + 2 * hv:], ((0, 0), (0, LANE - GLA_GATE_RANK)))
    wgate = jnp.pad(w_gate.astype(BF16), ((0, LANE - GLA_GATE_RANK), (0, 0)))
    return pl.pallas_call(
        _gla_in_kernel,
        grid=(b, s // ROW_TILE),
        in_specs=[
            _row_spec(d), _const_spec((1, d)), _mod_spec(0), _mod_spec(1),
            _const_spec((d, hk)), _const_spec((d, hk)), _const_spec((d, hv)),
            _const_spec((d, hv)), _const_spec((d, LANE)), _const_spec((LANE, hk)),
            _const_spec((1, hk)),
        ],
        out_specs=[_row_spec(hk), _row_spec(hk), _row_spec(hv), _row_spec(hv), _row_spec(hk)],
        out_shape=[
            jax.ShapeDtypeStruct((b, s, hk), BF16), jax.ShapeDtypeStruct((b, s, hk), BF16),
            jax.ShapeDtypeStruct((b, s, hv), BF16), jax.ShapeDtypeStruct((b, s, hv), BF16),
            jax.ShapeDtypeStruct((b, s, hk), F32),
        ],
        compiler_params=_cparams(("arbitrary", "arbitrary")),
        name="gla_in",
    )(x, gain.reshape(1, d), mod, mod, wq, wk, wv, wr, wl, wgate, b_gate.reshape(1, hk))


def _gla_kernel(q_ref, k_ref, v_ref, r_ref, la_ref, go_ref, o_ref, state_ref):
    c, sub = GLA_CHUNK, GLA_SUB
    nsub = c // sub
    assert nsub * c == 2 * LANE

    @pl.when(pl.program_id(2) == 0)
    def _():
        state_ref[...] = jnp.zeros_like(state_ref)

    row = lax.broadcasted_iota(jnp.int32, (c, c), 0)
    col = lax.broadcasted_iota(jnp.int32, (c, c), 1)
    tri = jnp.where(row >= col, 1.0, 0.0).astype(BF16)
    qrow = lax.broadcasted_iota(jnp.int32, (c, LANE), 0)
    lane = lax.broadcasted_iota(jnp.int32, (c, LANE), 1)
    causal = (lane % c) <= qrow
    keep_lo = causal & ((lane // c) == (qrow // sub))
    keep_hi = causal & ((lane // c) + LANE // c == (qrow // sub))

    n_chunks = GLA_TIME_BLOCK // c
    cs = [slice(ci * c, (ci + 1) * c) for ci in range(n_chunks)]

    bs = []
    for sl in cs:
        la = la_ref[sl, :]
        la_hi = la.astype(BF16)
        la_lo = (la - la_hi.astype(F32)).astype(BF16)
        bs.append(_dot(tri, la_hi) + _dot(tri, la_lo))

    qes, attns, kd_ts, decays = [], [], [], []
    for sl, b in zip(cs, bs):
        q = q_ref[sl, :].astype(F32)
        k = k_ref[sl, :].astype(F32)
        qes.append((q * jnp.exp(b)).astype(BF16))
        q_parts, k_parts = [], []
        for i in range(nsub):
            rows = slice(i * sub, (i + 1) * sub)
            live = (i + 1) * sub
            if i == 0:
                q_parts.append(q[rows] * jnp.exp(b[rows]))
                k_var = k[:live] * jnp.exp(-b[:live])
            else:
                anchor = b[i * sub - 1:i * sub, :]
                q_parts.append(q[rows] * jnp.exp(b[rows] - anchor))
                k_var = k[:live] * jnp.exp(anchor - b[:live])
            k_parts.append(k_var)
            if live < c:
                k_parts.append(jnp.zeros((c - live, GLA_DK), F32))
        q_anch = jnp.concatenate(q_parts, axis=0).astype(BF16)
        k_stack = jnp.concatenate(k_parts, axis=0).astype(BF16)
        prod = _dot_nt(q_anch, k_stack)
        attns.append(jnp.where(keep_lo, prod[:, :LANE],
                               jnp.where(keep_hi, prod[:, LANE:], 0.0)).astype(BF16))
        b_last = b[c - 1:c, :]
        kd_ts.append((k * jnp.exp(b_last - b)).T.astype(BF16))
        decays.append(jnp.broadcast_to(jnp.exp(b_last), (LANE, GLA_DK)).T)

    state = state_ref[...]
    gain = go_ref[...]
    for ci, sl in enumerate(cs):
        v = v_ref[sl, :]
        v2 = jnp.concatenate([v, v], axis=0)
        o = _dot(qes[ci], state.astype(BF16)) + _dot(attns[ci], v2)
        decay = jnp.concatenate([decays[ci]] * (GLA_DV // LANE), axis=1)
        state = state * decay + _dot(kd_ts[ci], v)
        r = r_ref[sl, :].astype(F32)
        o_ref[sl, :] = (_rms(o, gain) * (r * _sigmoid(r))).astype(BF16)
    state_ref[...] = state


def _gla(q, k, v, r, la, g_out):
    b, s, _ = q.shape
    t = GLA_TIME_BLOCK

    def spec(width):
        return pl.BlockSpec((None, t, width), lambda bi, h, ti: (bi, ti, h))

    return pl.pallas_call(
        _gla_kernel,
        grid=(b, GLA_HEADS, s // t),
        in_specs=[spec(GLA_DK), spec(GLA_DK), spec(GLA_DV), spec(GLA_DV), spec(GLA_DK),
                  pl.BlockSpec((1, GLA_DV), lambda bi, h, ti: (0, 0))],
        out_specs=spec(GLA_DV),
        out_shape=jax.ShapeDtypeStruct((b, s, GLA_HEADS * GLA_DV), BF16),
        scratch_shapes=[pltpu.VMEM((GLA_DK, GLA_DV), F32)],
        compiler_params=_cparams(("arbitrary", "arbitrary", "arbitrary")),
        name="gla_scan",
    )(q, k, v, r, la, g_out.reshape(1, GLA_DV))


def _out_res_kernel(a_ref, w_ref, x_ref, gate_ref, *rest, final_norm):
    o_ref = rest[-1]
    y = x_ref[...] + gate_ref[...] * _dot(a_ref[...], w_ref[...])
    if final_norm:
        y = _rms(y, rest[0][...])
    o_ref[...] = y


def _out_res(a, w, x, mod, gate_idx, final_gain=None):
    b, s, d = x.shape
    kdim = a.shape[-1]
    in_specs = [_row_spec(kdim), _const_spec((kdim, d)), _row_spec(d), _mod_spec(gate_idx)]
    args = [a, w.astype(BF16), x, mod]
    if final_gain is not None:
        in_specs.append(_const_spec((1, d)))
        args.append(final_gain.reshape(1, d))
    return pl.pallas_call(
        functools.partial(_out_res_kernel, final_norm=final_gain is not None),
        grid=(b, s // ROW_TILE),
        in_specs=in_specs,
        out_specs=_row_spec(d),
        out_shape=jax.ShapeDtypeStruct((b, s, d), F32),
        compiler_params=_cparams(("arbitrary", "arbitrary")),
        name="out_res",
    )(*args)


def _ffn_in_kernel(x_ref, g_ref, sh_ref, sc_ref, wg_ref, wu_ref, a_ref):
    h = _norm_mod(x_ref[...], g_ref[...], sh_ref[...], sc_ref[...]).astype(BF16)
    for j in range(0, D_FF, FFN_COL_CHUNK):
        cols = slice(j, j + FFN_COL_CHUNK)
        gate = _dot(h, wg_ref[:, cols])
        up = _dot(h, wu_ref[:, cols])
        a_ref[:, cols] = (gate * _sigmoid(gate) * up).astype(BF16)


def _ffn_in(x, mod, gain, w_in):
    b, s, d = x.shape
    w = w_in.astype(BF16)
    return pl.pallas_call(
        _ffn_in_kernel,
        grid=(b, s // ROW_TILE),
        in_specs=[_row_spec(d), _const_spec((1, d)), _mod_spec(3), _mod_spec(4),
                  _const_spec((d, D_FF)), _const_spec((d, D_FF))],
        out_specs=_row_spec(D_FF),
        out_shape=jax.ShapeDtypeStruct((b, s, D_FF), BF16),
        compiler_params=_cparams(("arbitrary", "arbitrary")),
        name="ffn_in",
    )(x, gain.reshape(1, d), mod, mod, w[:, :D_FF], w[:, D_FF:])


def _mla_in_kernel(x_ref, g_ref, sh_ref, sc_ref, win_ref, gq_ref, wq_ref, gkv_ref,
                   wkn_ref, wvt_ref, cos_ref, sin_ref, q_ref, kc_ref, vt_ref, *, q_scale):
    h = _norm_mod(x_ref[...], g_ref[...], sh_ref[...], sc_ref[...]).astype(BF16)
    proj = _dot(h, win_ref[...])
    cq = _rms(proj[:, :MLA_Q_RANK], gq_ref[...]).astype(BF16)
    ckv = _rms(proj[:, MLA_Q_RANK:MLA_Q_RANK + MLA_KV_RANK], gkv_ref[...]).astype(BF16)
    cos = cos_ref[...]
    sin = sin_ref[...]

    def rope(t):
        return t * cos + pltpu.roll(t, MLA_ROPE, axis=1) * sin

    k_rope = rope(proj[:, MLA_Q_RANK + MLA_KV_RANK:]).astype(BF16)
    v_t = _dot_nt(wvt_ref[...], ckv)
    ones = jnp.ones((MLA_VT_ROWS - MLA_VDIM, v_t.shape[1]), BF16)
    k_nope = _dot(ckv, wkn_ref[...])
    q_all = _dot(cq, wq_ref[...])
    for hd in range(MLA_HEADS):
        lo = hd * MLA_QK_PAD
        mid = lo + MLA_NOPE
        hi = lo + MLA_QK_PAD
        q_ref[:, lo:mid] = (q_all[:, lo:mid] * q_scale).astype(BF16)
        q_ref[:, mid:hi] = (rope(q_all[:, mid:hi]) * q_scale).astype(BF16)
        kc_ref[:, lo:mid] = k_nope[:, hd * MLA_NOPE:(hd + 1) * MLA_NOPE].astype(BF16)
        kc_ref[:, mid:hi] = k_rope
        vt_ref[hd, :MLA_VDIM, :] = v_t[hd * MLA_VDIM:(hd + 1) * MLA_VDIM, :].astype(BF16)
        vt_ref[hd, MLA_VDIM:, :] = ones


def _half_split(w_rope):
    even, odd = w_rope[..., 0::2], w_rope[..., 1::2]
    return jnp.concatenate([even, odd, odd, even], axis=-1)


def _mla_in(x, mod, gain, cos_t, sin_t, w_in, g_q, w_q_up, g_kv, w_kv_up):
    b, s, d = x.shape
    hq = MLA_HEADS * MLA_QK_PAD
    hn = MLA_HEADS * MLA_NOPE
    hv = MLA_HEADS * MLA_VDIM
    lat = MLA_Q_RANK + MLA_KV_RANK
    win = jnp.concatenate([w_in[:, :lat], _half_split(w_in[:, lat:])], axis=-1).astype(BF16)
    wq3 = w_q_up.reshape(MLA_Q_RANK, MLA_HEADS, MLA_NOPE + MLA_ROPE)
    wq = jnp.concatenate([wq3[..., :MLA_NOPE], _half_split(wq3[..., MLA_NOPE:])], axis=-1)
    wq = wq.reshape(MLA_Q_RANK, hq).astype(BF16)
    wkv3 = w_kv_up.reshape(MLA_KV_RANK, MLA_HEADS, MLA_NOPE + MLA_VDIM)
    wkn = wkv3[..., :MLA_NOPE].reshape(MLA_KV_RANK, hn).astype(BF16)
    wvt = wkv3[..., MLA_NOPE:].reshape(MLA_KV_RANK, hv).T.astype(BF16)
    win_w = win.shape[-1]
    q_scale = (MLA_NOPE + MLA_ROPE) ** -0.5 * math.log2(math.e)
    return pl.pallas_call(
        functools.partial(_mla_in_kernel, q_scale=q_scale),
        grid=(b, s // ROW_TILE),
        in_specs=[
            _row_spec(d), _const_spec((1, d)), _mod_spec(0), _mod_spec(1),
            _const_spec((d, win_w)), _const_spec((1, MLA_Q_RANK)), _const_spec((MLA_Q_RANK, hq)),
            _const_spec((1, MLA_KV_RANK)), _const_spec((MLA_KV_RANK, hn)),
            _const_spec((hv, MLA_KV_RANK)), _row_spec(LANE), _row_spec(LANE),
        ],
        out_specs=[_row_spec(hq), _row_spec(hq),
                   pl.BlockSpec((None, MLA_HEADS, MLA_VT_ROWS, ROW_TILE), lambda bi, i: (bi, 0, 0, i))],
        out_shape=[jax.ShapeDtypeStruct((b, s, hq), BF16), jax.ShapeDtypeStruct((b, s, hq), BF16),
                   jax.ShapeDtypeStruct((b, MLA_HEADS, MLA_VT_ROWS, s), BF16)],
        compiler_params=_cparams(("arbitrary", "arbitrary")),
        name="mla_in",
    )(x, gain.reshape(1, d), mod, mod, win, g_q.reshape(1, -1), wq, g_kv.reshape(1, -1),
      wkn, wvt, cos_t, sin_t)


def _attn_kernel(q_ref, k_ref, vt_ref, o_ref, s0_ref, s1_ref, mb0_ref, mb1_ref, m_ref, acc_ref):
    t = ATT_BLOCK
    qi = pl.program_id(2)
    q = q_ref[...]
    m_ref[...] = jnp.full_like(m_ref, -jnp.inf)
    acc_ref[...] = jnp.zeros_like(acc_ref)

    def scores(kj, s_ref, mb_ref, masked=False):
        start = pl.multiple_of(kj * t, t)
        s_t = _dot_nt(k_ref[pl.ds(start, t), :], q)
        if masked:
            key = lax.broadcasted_iota(jnp.int32, (t, t), 0)
            qry = lax.broadcasted_iota(jnp.int32, (t, t), 1)
            s_t = jnp.where(key <= qry, s_t, -jnp.inf)
        s_ref[...] = s_t
        mb_ref[...] = jnp.max(s_t, axis=0, keepdims=True)

    def accumulate(kj, s_ref, mb_ref):
        start = pl.multiple_of(kj * t, t)
        m_prev = m_ref[...]
        m_new = jnp.maximum(m_prev, mb_ref[...])
        p_t = jnp.exp2((s_ref[...] - m_new).astype(BF16))
        alpha = jnp.exp2(m_prev - m_new)
        acc_ref[...] = alpha * acc_ref[...] + _dot(vt_ref[:, pl.ds(start, t)], p_t)
        m_ref[...] = m_new

    scores(qi, s0_ref, mb0_ref, masked=True)

    def pair(p, pending):
        scores(2 * p, s1_ref, mb1_ref)
        accumulate(pending, s0_ref, mb0_ref)
        scores(2 * p + 1, s0_ref, mb0_ref)
        accumulate(2 * p, s1_ref, mb1_ref)
        return 2 * p + 1

    pending = lax.fori_loop(0, qi // 2, pair, qi)

    @pl.when(qi % 2 == 1)
    def _():
        scores(qi - 1, s1_ref, mb1_ref)
        accumulate(pending, s0_ref, mb0_ref)
        accumulate(qi - 1, s1_ref, mb1_ref)

    @pl.when(qi % 2 == 0)
    def _():
        accumulate(pending, s0_ref, mb0_ref)

    acc = acc_ref[...]
    o_t = acc[:MLA_VDIM, :] / acc[MLA_VDIM:MLA_VDIM + 1, :]
    o_ref[...] = o_t.T.astype(BF16)


def _attention(q, kc, vt):
    b, _, _, s = vt.shape
    t = ATT_BLOCK
    return pl.pallas_call(
        _attn_kernel,
        grid=(b, MLA_HEADS, s // t),
        in_specs=[
            pl.BlockSpec((None, t, MLA_QK_PAD), lambda bi, h, i: (bi, i, h)),
            pl.BlockSpec((None, s, MLA_QK_PAD), lambda bi, h, i: (bi, 0, h)),
            pl.BlockSpec((None, None, MLA_VT_ROWS, s), lambda bi, h, i: (bi, h, 0, 0)),
        ],
        out_specs=pl.BlockSpec((None, t, MLA_VDIM), lambda bi, h, i: (bi, i, h)),
        out_shape=jax.ShapeDtypeStruct((b, s, MLA_HEADS * MLA_VDIM), BF16),
        scratch_shapes=[pltpu.VMEM((t, t), F32), pltpu.VMEM((t, t), F32),
                        pltpu.VMEM((1, t), F32), pltpu.VMEM((1, t), F32),
                        pltpu.VMEM((1, t), F32), pltpu.VMEM((MLA_VT_ROWS, t), F32)],
        compiler_params=_cparams(("arbitrary", "arbitrary", "arbitrary")),
        name="mla_attention",
    )(q, kc, vt)


def kernel(x, c, positions, ada_w, ada_b, norm_mix, norm_ffn, gla_w_in, gla_w_gate, gla_b_gate,
           gla_g_out, gla_w_out, mla_w_in, mla_g_q, mla_w_q_up, mla_g_kv, mla_w_kv_up, mla_w_out,
           ffn_w_in, ffn_w_out, final_norm):
    batch, _, d = x.shape
    depth = ada_w.shape[0]
    assert batch <= 8
    c_pad = jnp.pad(c, ((0, 8 - batch), (0, 0)))
    mod_all = _adaln(c_pad, ada_w, ada_b)[:, :batch].reshape(depth, batch, 6, 1, d)
    cos_t, sin_t = _rope_tables(positions)

    for i in range(depth):
        mod = mod_all[i]
        j = i // 2
        if i % 2 == 0:
            q, k, v, r, la = _gla_in(x, mod, norm_mix[i], gla_w_in[j], gla_w_gate[j], gla_b_gate[j])
            mixed = _gla(q, k, v, r, la, gla_g_out[j])
            x = _out_res(mixed, gla_w_out[j], x, mod, 2)
        else:
            q, kc, vt = _mla_in(x, mod, norm_mix[i], cos_t, sin_t, mla_w_in[j], mla_g_q[j],
                                mla_w_q_up[j], mla_g_kv[j], mla_w_kv_up[j])
            mixed = _attention(q, kc, vt)
            x = _out_res(mixed, mla_w_out[j], x, mod, 2)
        act = _ffn_in(x, mod, norm_ffn[i], ffn_w_in[i])
        last = i == depth - 1
        x = _out_res(act, ffn_w_out[i], x, mod, 5, final_gain=final_norm if last else None)
    return x
```

```python
import functools
import math

import jax
import jax.numpy as jnp
from jax import lax
from jax.experimental import pallas as pl
from jax.experimental.pallas import tpu as pltpu

F32 = jnp.float32
BF16 = jnp.bfloat16

D_MODEL = 1024
EPS = 1e-6

GLA_HEADS = 4
GLA_DK = 128
GLA_DV = 256
GLA_GATE_RANK = 16
GLA_TAU = 16.0
GLA_CHUNK = 64
GLA_SUB = 16

MLA_HEADS = 8
MLA_NOPE = 128
MLA_ROPE = 64
MLA_VDIM = 128
MLA_Q_RANK = 384
MLA_KV_RANK = 256
ROPE_THETA = 10000.0
MLA_QK_PAD = 256
MLA_VT_ROWS = MLA_VDIM + 16

D_FF = 2816

LANE = 128
ROW_TILE = 512
GLA_TIME_BLOCK = 512
ATT_Q_BLOCK = 1024
ATT_KV_BLOCK = 512
FFN_COL_CHUNK = 256
VMEM_LIMIT = 56 * 1024 * 1024


def _cparams(sem):
    return pltpu.CompilerParams(dimension_semantics=sem, vmem_limit_bytes=VMEM_LIMIT)


def _dot(a, b):
    return jnp.dot(a, b, preferred_element_type=F32)


def _dot_nt(a, b):
    return lax.dot_general(a, b, (((1,), (1,)), ((), ())), preferred_element_type=F32)


def _sigmoid(x):
    return 1.0 / (1.0 + jnp.exp(-x))


def _rms(x, g):
    return x * lax.rsqrt(jnp.mean(x * x, axis=-1, keepdims=True) + EPS) * g


def _norm_mod(x, g, shift, scale):
    return _rms(x, g) * (1.0 + scale) + shift


def _adaln_kernel(c_ref, w_ref, b_ref, o_ref):
    c = c_ref[...]
    o_ref[...] = _dot(c * _sigmoid(c), w_ref[...]) + b_ref[...]


def _adaln(c_pad, ada_w, ada_b):
    depth, d, n = ada_w.shape
    tn = 1536
    return pl.pallas_call(
        _adaln_kernel,
        grid=(depth, n // tn),
        in_specs=[
            pl.BlockSpec((8, d), lambda l, j: (0, 0)),
            pl.BlockSpec((None, d, tn), lambda l, j: (l, 0, j)),
            pl.BlockSpec((None, 1, tn), lambda l, j: (l, 0, j)),
        ],
        out_specs=pl.BlockSpec((None, 8, tn), lambda l, j: (l, 0, j)),
        out_shape=jax.ShapeDtypeStruct((depth, 8, n), F32),
        compiler_params=_cparams(("arbitrary", "arbitrary")),
        name="adaln",
    )(c_pad, ada_w, ada_b.reshape(depth, 1, n))


def _rope_table_kernel(pos_ref, f_ref, cos_ref, sin_ref):
    ang = pos_ref[...].astype(F32) * f_ref[...]
    cos_ref[...] = jnp.cos(ang)
    sin_ref[...] = jnp.sin(ang)


def _rope_tables(positions):
    b, s = positions.shape
    half = MLA_ROPE // 2
    per_row = LANE // half
    inv_freq = ROPE_THETA ** (-jnp.arange(0, MLA_ROPE, 2, dtype=F32) / MLA_ROPE)
    pos_dense = jnp.repeat(positions, half, axis=-1).reshape(b, s // per_row, LANE)
    f_dense = jnp.tile(inv_freq, per_row).reshape(1, LANE)
    spec = pl.BlockSpec((None, s // per_row, LANE), lambda i: (i, 0, 0))
    cos, sin = pl.pallas_call(
        _rope_table_kernel,
        grid=(b,),
        in_specs=[spec, pl.BlockSpec((1, LANE), lambda i: (0, 0))],
        out_specs=[spec, spec],
        out_shape=[jax.ShapeDtypeStruct((b, s // per_row, LANE), F32)] * 2,
        compiler_params=_cparams(("arbitrary",)),
        name="rope_tables",
    )(pos_dense, f_dense)
    cos = cos.reshape(b, s, half)
    sin = sin.reshape(b, s, half)
    zero = jnp.zeros((b, s, LANE - 2 * half), F32)
    return (jnp.concatenate([cos, cos, zero], axis=-1),
            jnp.concatenate([-sin, sin, zero], axis=-1))


def _row_spec(width):
    return pl.BlockSpec((None, ROW_TILE, width), lambda b, i: (b, i, 0))


def _mod_spec(k):
    return pl.BlockSpec((None, None, 1, D_MODEL), lambda b, i, k=k: (b, k, 0, 0))


def _const_spec(shape):
    return pl.BlockSpec(shape, lambda b, i: (0,) * len(shape))


def _gla_in_kernel(x_ref, g_ref, sh_ref, sc_ref, wq_ref, wk_ref, wv_ref, wr_ref,
                   wl_ref, wgate_ref, bgate_ref, q_ref, k_ref, v_ref, r_ref, la_ref):
    h = _norm_mod(x_ref[...], g_ref[...], sh_ref[...], sc_ref[...]).astype(BF16)
    q_ref[...] = (_dot(h, wq_ref[...]) * (GLA_DK ** -0.5)).astype(BF16)
    k_ref[...] = _dot(h, wk_ref[...]).astype(BF16)
    v_ref[...] = _dot(h, wv_ref[...]).astype(BF16)
    r_ref[...] = _dot(h, wr_ref[...]).astype(BF16)
    low_rank = _dot(h, wl_ref[...]).astype(BF16)
    z = _dot(low_rank, wgate_ref[...]) + bgate_ref[...]
    log_sig = jnp.minimum(z, 0.0) - jnp.log(1.0 + jnp.exp(-jnp.abs(z)))
    la_ref[...] = log_sig / GLA_TAU


def _gla_in(x, mod, gain, w_in, w_gate, b_gate):
    b, s, d = x.shape
    hk = GLA_HEADS * GLA_DK
    hv = GLA_HEADS * GLA_DV
    w = w_in.astype(BF16)
    wq, wk = w[:, :hk], w[:, hk:2 * hk]
    wv, wr = w[:, 2 * hk:2 * hk + hv], w[:, 2 * hk + hv:2 * hk + 2 * hv]
    wl = jnp.pad(w[:, 2 * hk + 2 * hv:], ((0, 0), (0, LANE - GLA_GATE_RANK)))
    wgate = jnp.pad(w_gate.astype(BF16), ((0, LANE - GLA_GATE_RANK), (0, 0)))
    return pl.pallas_call(
        _gla_in_kernel,
        grid=(b, s // ROW_TILE),
        in_specs=[
            _row_spec(d), _const_spec((1, d)), _mod_spec(0), _mod_spec(1),
            _const_spec((d, hk)), _const_spec((d, hk)), _const_spec((d, hv)),
            _const_spec((d, hv)), _const_spec((d, LANE)), _const_spec((LANE, hk)),
            _const_spec((1, hk)),
        ],
        out_specs=[_row_spec(hk), _row_spec(hk), _row_spec(hv), _row_spec(hv), _row_spec(hk)],
        out_shape=[
            jax.ShapeDtypeStruct((b, s, hk), BF16), jax.ShapeDtypeStruct((b, s, hk), BF16),
            jax.ShapeDtypeStruct((b, s, hv), BF16), jax.ShapeDtypeStruct((b, s, hv), BF16),
            jax.ShapeDtypeStruct((b, s, hk), F32),
        ],
        compiler_params=_cparams(("arbitrary", "arbitrary")),
        name="gla_in",
    )(x, gain.reshape(1, d), mod, mod, wq, wk, wv, wr, wl, wgate, b_gate.reshape(1, hk))


def _gla_kernel(q_ref, k_ref, v_ref, r_ref, la_ref, go_ref, o_ref, state_ref):
    c, sub = GLA_CHUNK, GLA_SUB
    nsub = c // sub
    assert nsub * c == 2 * LANE

    @pl.when(pl.program_id(2) == 0)
    def _():
        state_ref[...] = jnp.zeros_like(state_ref)

    row = lax.broadcasted_iota(jnp.int32, (c, c), 0)
    col = lax.broadcasted_iota(jnp.int32, (c, c), 1)
    tri = jnp.where(row >= col, 1.0, 0.0).astype(BF16)
    qrow = lax.broadcasted_iota(jnp.int32, (c, LANE), 0)
    lane = lax.broadcasted_iota(jnp.int32, (c, LANE), 1)
    causal = (lane % c) <= qrow
    keep_lo = causal & ((lane // c) == (qrow // sub))
    keep_hi = causal & ((lane // c) + LANE // c == (qrow // sub))

    n_chunks = GLA_TIME_BLOCK // c
    cs = [slice(ci * c, (ci + 1) * c) for ci in range(n_chunks)]

    bs = []
    for sl in cs:
        la = la_ref[sl, :]
        la_hi = la.astype(BF16)
        la_lo = (la - la_hi.astype(F32)).astype(BF16)
        bs.append(_dot(tri, la_hi) + _dot(tri, la_lo))

    qes, attns, kd_ts, decays = [], [], [], []
    for sl, b in zip(cs, bs):
        q = q_ref[sl, :].astype(F32)
        k = k_ref[sl, :].astype(F32)
        qes.append((q * jnp.exp(b)).astype(BF16))
        q_parts, k_parts = [], []
        for i in range(nsub):
            rows = slice(i * sub, (i + 1) * sub)
            live = (i + 1) * sub
            if i == 0:
                q_parts.append(q[rows] * jnp.exp(b[rows]))
                k_var = k[:live] * jnp.exp(-b[:live])
            else:
                anchor = b[i * sub - 1:i * sub, :]
                q_parts.append(q[rows] * jnp.exp(b[rows] - anchor))
                k_var = k[:live] * jnp.exp(anchor - b[:live])
            k_parts.append(k_var)
            if live < c:
                k_parts.append(jnp.zeros((c - live, GLA_DK), F32))
        q_anch = jnp.concatenate(q_parts, axis=0).astype(BF16)
        k_stack = jnp.concatenate(k_parts, axis=0).astype(BF16)
        prod = _dot_nt(q_anch, k_stack)
        attns.append(jnp.where(keep_lo, prod[:, :LANE],
                               jnp.where(keep_hi, prod[:, LANE:], 0.0)).astype(BF16))
        b_last = b[c - 1:c, :]
        kd_ts.append((k * jnp.exp(b_last - b)).T.astype(BF16))
        decays.append(jnp.broadcast_to(jnp.exp(b_last), (LANE, GLA_DK)).T)

    state = state_ref[...]
    gain = go_ref[...]
    for ci, sl in enumerate(cs):
        v = v_ref[sl, :]
        v2 = jnp.concatenate([v, v], axis=0)
        o = _dot(qes[ci], state.astype(BF16)) + _dot(attns[ci], v2)
        decay = jnp.concatenate([decays[ci]] * (GLA_DV // LANE), axis=1)
        state = state * decay + _dot(kd_ts[ci], v)
        r = r_ref[sl, :].astype(F32)
        o_ref[sl, :] = (_rms(o, gain) * (r * _sigmoid(r))).astype(BF16)
    state_ref[...] = state


def _gla(q, k, v, r, la, g_out):
    b, s, _ = q.shape
    t = GLA_TIME_BLOCK

    def spec(width):
        return pl.BlockSpec((None, t, width), lambda bi, h, ti: (bi, ti, h))

    return pl.pallas_call(
        _gla_kernel,
        grid=(b, GLA_HEADS, s // t),
        in_specs=[spec(GLA_DK), spec(GLA_DK), spec(GLA_DV), spec(GLA_DV), spec(GLA_DK),
                  pl.BlockSpec((1, GLA_DV), lambda bi, h, ti: (0, 0))],
        out_specs=spec(GLA_DV),
        out_shape=jax.ShapeDtypeStruct((b, s, GLA_HEADS * GLA_DV), BF16),
        scratch_shapes=[pltpu.VMEM((GLA_DK, GLA_DV), F32)],
        compiler_params=_cparams(("arbitrary", "arbitrary", "arbitrary")),
        name="gla_scan",
    )(q, k, v, r, la, g_out.reshape(1, GLA_DV))


def _out_res_kernel(a_ref, w_ref, x_ref, gate_ref, *rest, final_norm):
    o_ref = rest[-1]
    y = x_ref[...] + gate_ref[...] * _dot(a_ref[...], w_ref[...])
    if final_norm:
        y = _rms(y, rest[0][...])
    o_ref[...] = y


def _out_res(a, w, x, mod, gate_idx, final_gain=None):
    b, s, d = x.shape
    kdim = a.shape[-1]
    in_specs = [_row_spec(kdim), _const_spec((kdim, d)), _row_spec(d), _mod_spec(gate_idx)]
    args = [a, w.astype(BF16), x, mod]
    if final_gain is not None:
        in_specs.append(_const_spec((1, d)))
        args.append(final_gain.reshape(1, d))
    return pl.pallas_call(
        functools.partial(_out_res_kernel, final_norm=final_gain is not None),
        grid=(b, s // ROW_TILE),
        in_specs=in_specs,
        out_specs=_row_spec(d),
        out_shape=jax.ShapeDtypeStruct((b, s, d), F32),
        compiler_params=_cparams(("arbitrary", "arbitrary")),
        name="out_res",
    )(*args)


def _ffn_in_kernel(x_ref, g_ref, sh_ref, sc_ref, wg_ref, wu_ref, a_ref):
    h = _norm_mod(x_ref[...], g_ref[...], sh_ref[...], sc_ref[...]).astype(BF16)
    for j in range(0, D_FF, FFN_COL_CHUNK):
        cols = slice(j, j + FFN_COL_CHUNK)
        gate = _dot(h, wg_ref[:, cols])
        up = _dot(h, wu_ref[:, cols])
        a_ref[:, cols] = (gate * _sigmoid(gate) * up).astype(BF16)


def _ffn_in(x, mod, gain, w_in):
    b, s, d = x.shape
    w = w_in.astype(BF16)
    return pl.pallas_call(
        _ffn_in_kernel,
        grid=(b, s // ROW_TILE),
        in_specs=[_row_spec(d), _const_spec((1, d)), _mod_spec(3), _mod_spec(4),
                  _const_spec((d, D_FF)), _const_spec((d, D_FF))],
        out_specs=_row_spec(D_FF),
        out_shape=jax.ShapeDtypeStruct((b, s, D_FF), BF16),
        compiler_params=_cparams(("arbitrary", "arbitrary")),
        name="ffn_in",
    )(x, gain.reshape(1, d), mod, mod, w[:, :D_FF], w[:, D_FF:])


def _mla_in_kernel(x_ref, g_ref, sh_ref, sc_ref, win_ref, gq_ref, wq_ref, gkv_ref,
                   wkn_ref, wvt_ref, cos_ref, sin_ref, q_ref, kc_ref, vt_ref, *, q_scale):
    h = _norm_mod(x_ref[...], g_ref[...], sh_ref[...], sc_ref[...]).astype(BF16)
    proj = _dot(h, win_ref[...])
    cq = _rms(proj[:, :MLA_Q_RANK], gq_ref[...]).astype(BF16)
    ckv = _rms(proj[:, MLA_Q_RANK:MLA_Q_RANK + MLA_KV_RANK], gkv_ref[...]).astype(BF16)
    cos = cos_ref[...]
    sin = sin_ref[...]

    def rope(t):
        return t * cos + pltpu.roll(t, MLA_ROPE, axis=1) * sin

    k_rope = rope(proj[:, MLA_Q_RANK + MLA_KV_RANK:]).astype(BF16)
    v_t = _dot_nt(wvt_ref[...], ckv)
    ones = jnp.ones((MLA_VT_ROWS - MLA_VDIM, v_t.shape[1]), BF16)
    k_nope = _dot(ckv, wkn_ref[...])
    q_all = _dot(cq, wq_ref[...])
    for hd in range(MLA_HEADS):
        lo = hd * MLA_QK_PAD
        mid = lo + MLA_NOPE
        hi = lo + MLA_QK_PAD
        q_ref[:, lo:mid] = (q_all[:, lo:mid] * q_scale).astype(BF16)
        q_ref[:, mid:hi] = (rope(q_all[:, mid:hi]) * q_scale).astype(BF16)
        kc_ref[:, lo:mid] = k_nope[:, hd * MLA_NOPE:(hd + 1) * MLA_NOPE].astype(BF16)
        kc_ref[:, mid:hi] = k_rope
        vt_ref[hd, :MLA_VDIM, :] = v_t[hd * MLA_VDIM:(hd + 1) * MLA_VDIM, :].astype(BF16)
        vt_ref[hd, MLA_VDIM:, :] = ones


def _half_split(w_rope):
    even, odd = w_rope[..., 0::2], w_rope[..., 1::2]
    return jnp.concatenate([even, odd, odd, even], axis=-1)


def _mla_in(x, mod, gain, cos_t, sin_t, w_in, g_q, w_q_up, g_kv, w_kv_up):
    b, s, d = x.shape
    hq = MLA_HEADS * MLA_QK_PAD
    hn = MLA_HEADS * MLA_NOPE
    hv = MLA_HEADS * MLA_VDIM
    lat = MLA_Q_RANK + MLA_KV_RANK
    win = jnp.concatenate([w_in[:, :lat], _half_split(w_in[:, lat:])], axis=-1).astype(BF16)
    wq3 = w_q_up.reshape(MLA_Q_RANK, MLA_HEADS, MLA_NOPE + MLA_ROPE)
    wq = jnp.concatenate([wq3[..., :MLA_NOPE], _half_split(wq3[..., MLA_NOPE:])], axis=-1)
    wq = wq.reshape(MLA_Q_RANK, hq).astype(BF16)
    wkv3 = w_kv_up.reshape(MLA_KV_RANK, MLA_HEADS, MLA_NOPE + MLA_VDIM)
    wkn = wkv3[..., :MLA_NOPE].reshape(MLA_KV_RANK, hn).astype(BF16)
    wvt = wkv3[..., MLA_NOPE:].reshape(MLA_KV_RANK, hv).T.astype(BF16)
    win_w = win.shape[-1]
    q_scale = (MLA_NOPE + MLA_ROPE) ** -0.5 * math.log2(math.e)
    return pl.pallas_call(
        functools.partial(_mla_in_kernel, q_scale=q_scale),
        grid=(b, s // ROW_TILE),
        in_specs=[
            _row_spec(d), _const_spec((1, d)), _mod_spec(0), _mod_spec(1),
            _const_spec((d, win_w)), _const_spec((1, MLA_Q_RANK)), _const_spec((MLA_Q_RANK, hq)),
            _const_spec((1, MLA_KV_RANK)), _const_spec((MLA_KV_RANK, hn)),
            _const_spec((hv, MLA_KV_RANK)), _row_spec(LANE), _row_spec(LANE),
        ],
        out_specs=[_row_spec(hq), _row_spec(hq),
                   pl.BlockSpec((None, MLA_HEADS, MLA_VT_ROWS, ROW_TILE), lambda bi, i: (bi, 0, 0, i))],
        out_shape=[jax.ShapeDtypeStruct((b, s, hq), BF16), jax.ShapeDtypeStruct((b, s, hq), BF16),
                   jax.ShapeDtypeStruct((b, MLA_HEADS, MLA_VT_ROWS, s), BF16)],
        compiler_params=_cparams(("arbitrary", "arbitrary")),
        name="mla_in",
    )(x, gain.reshape(1, d), mod, mod, win, g_q.reshape(1, -1), wq, g_kv.reshape(1, -1),
      wkn, wvt, cos_t, sin_t)


def _attn_kernel(q_ref, k_ref, vt_ref, o_ref, s0_ref, s1_ref, mb0_ref, mb1_ref, m_ref, acc_ref):
    tq, tk = ATT_Q_BLOCK, ATT_KV_BLOCK
    per_q = tq // tk
    assert per_q == 2
    qi = pl.program_id(2)
    q = q_ref[...]
    m_ref[...] = jnp.full_like(m_ref, -jnp.inf)
    acc_ref[...] = jnp.zeros_like(acc_ref)

    def scores(kj, s_ref, mb_ref, diag_offset=None):
        start = pl.multiple_of(kj * tk, tk)
        s_t = _dot_nt(k_ref[pl.ds(start, tk), :], q)
        if diag_offset is not None:
            key = lax.broadcasted_iota(jnp.int32, (tk, tq), 0) + diag_offset
            qry = lax.broadcasted_iota(jnp.int32, (tk, tq), 1)
            s_t = jnp.where(key <= qry, s_t, -jnp.inf)
        s_ref[...] = s_t
        mb_ref[...] = jnp.max(s_t, axis=0, keepdims=True)

    def accumulate(kj, s_ref, mb_ref):
        start = pl.multiple_of(kj * tk, tk)
        m_prev = m_ref[...]
        m_new = jnp.maximum(m_prev, mb_ref[...])
        p_t = jnp.exp2((s_ref[...] - m_new).astype(BF16))
        alpha = jnp.exp2(m_prev - m_new)
        acc_ref[...] = alpha * acc_ref[...] + _dot(vt_ref[:, pl.ds(start, tk)], p_t)
        m_ref[...] = m_new

    first = per_q * qi
    scores(first, s0_ref, mb0_ref, diag_offset=0)
    scores(first + 1, s1_ref, mb1_ref, diag_offset=tk)
    accumulate(first, s0_ref, mb0_ref)

    def pair(p, pending):
        scores(2 * p, s0_ref, mb0_ref)
        accumulate(pending, s1_ref, mb1_ref)
        scores(2 * p + 1, s1_ref, mb1_ref)
        accumulate(2 * p, s0_ref, mb0_ref)
        return 2 * p + 1

    pending = lax.fori_loop(0, qi, pair, first + 1)
    accumulate(pending, s1_ref, mb1_ref)

    acc = acc_ref[...]
    o_t = acc[:MLA_VDIM, :] / acc[MLA_VDIM:MLA_VDIM + 1, :]
    o_ref[...] = o_t.T.astype(BF16)


def _attention(q, kc, vt):
    b, _, _, s = vt.shape
    tq, tk = ATT_Q_BLOCK, ATT_KV_BLOCK
    return pl.pallas_call(
        _attn_kernel,
        grid=(b, MLA_HEADS, s // tq),
        in_specs=[
            pl.BlockSpec((None, tq, MLA_QK_PAD), lambda bi, h, i: (bi, i, h)),
            pl.BlockSpec((None, s, MLA_QK_PAD), lambda bi, h, i: (bi, 0, h)),
            pl.BlockSpec((None, None, MLA_VT_ROWS, s), lambda bi, h, i: (bi, h, 0, 0)),
        ],
        out_specs=pl.BlockSpec((None, tq, MLA_VDIM), lambda bi, h, i: (bi, i, h)),
        out_shape=jax.ShapeDtypeStruct((b, s, MLA_HEADS * MLA_VDIM), BF16),
        scratch_shapes=[pltpu.VMEM((tk, tq), F32), pltpu.VMEM((tk, tq), F32),
                        pltpu.VMEM((1, tq), F32), pltpu.VMEM((1, tq), F32),
                        pltpu.VMEM((1, tq), F32), pltpu.VMEM((MLA_VT_ROWS, tq), F32)],
        compiler_params=_cparams(("arbitrary", "arbitrary", "arbitrary")),
        name="mla_attention",
    )(q, kc, vt)


def kernel(x, c, positions, ada_w, ada_b, norm_mix, norm_ffn, gla_w_in, gla_w_gate, gla_b_gate,
           gla_g_out, gla_w_out, mla_w_in, mla_g_q, mla_w_q_up, mla_g_kv, mla_w_kv_up, mla_w_out,
           ffn_w_in, ffn_w_out, final_norm):
    batch, _, d = x.shape
    depth = ada_w.shape[0]
    assert batch <= 8
    c_pad = jnp.pad(c, ((0, 8 - batch), (0, 0)))
    mod_all = _adaln(c_pad, ada_w, ada_b)[:, :batch].reshape(depth, batch, 6, 1, d)
    cos_t, sin_t = _rope_tables(positions)

    for i in range(depth):
        mod = mod_all[i]
        j = i // 2
        if i % 2 == 0:
            q, k, v, r, la = _gla_in(x, mod, norm_mix[i], gla_w_in[j], gla_w_gate[j], gla_b_gate[j])
            mixed = _gla(q, k, v, r, la, gla_g_out[j])
            x = _out_res(mixed, gla_w_out[j], x, mod, 2)
        else:
            q, kc, vt = _mla_in(x, mod, norm_mix[i], cos_t, sin_t, mla_w_in[j], mla_g_q[j],
                                mla_w_q_up[j], mla_g_kv[j], mla_w_kv_up[j])
            mixed = _attention(q, kc, vt)
            x = _out_res(mixed, mla_w_out[j], x, mod, 2)
        act = _ffn_in(x, mod, norm_ffn[i], ffn_w_in[i])
        last = i == depth - 1
        x = _out_res(act, ffn_w_out[i], x, mod, 5, final_gain=final_norm if last else None)
    return x
```

```python
import functools
import math

import jax
import jax.numpy as jnp
from jax import lax
from jax.experimental import pallas as pl
from jax.experimental.pallas import tpu as pltpu

F32 = jnp.float32
BF16 = jnp.bfloat16

D_MODEL = 1024
EPS = 1e-6

GLA_HEADS = 4
GLA_DK = 128
GLA_DV = 256
GLA_GATE_RANK = 16
GLA_TAU = 16.0
GLA_CHUNK = 64
GLA_SUB = 16

MLA_HEADS = 8
MLA_NOPE = 128
MLA_ROPE = 64
MLA_VDIM = 128
MLA_Q_RANK = 384
MLA_KV_RANK = 256
ROPE_THETA = 10000.0
MLA_QK_PAD = 256
MLA_VT_ROWS = MLA_VDIM + 16

D_FF = 2816
N_MOD = 6

LANE = 128
SUBLANE = 8
ROW_TILE = 512
GLA_TIME_BLOCK = 512
ATT_Q_BLOCK = 1024
ATT_KV_BLOCK = 512
FFN_COL_CHUNK = 256
VMEM_LIMIT = 56 * 1024 * 1024


def _cparams(sem):
    return pltpu.CompilerParams(dimension_semantics=sem, vmem_limit_bytes=VMEM_LIMIT)


def _dot(a, b):
    return jnp.dot(a, b, preferred_element_type=F32)


def _dot_nt(a, b):
    return lax.dot_general(a, b, (((1,), (1,)), ((), ())), preferred_element_type=F32)


def _sigmoid(x):
    return 1.0 / (1.0 + jnp.exp(-x))


def _rms(x, g):
    return x * lax.rsqrt(jnp.mean(x * x, axis=-1, keepdims=True) + EPS) * g


def _norm_mod(x, g, shift, scale):
    return _rms(x, g) * (1.0 + scale) + shift


def _adaln_kernel(c_ref, w_ref, b_ref, o_ref):
    c = c_ref[...]
    o_ref[...] = _dot(c * _sigmoid(c), w_ref[...]) + b_ref[...]


def _adaln(c_pad, ada_w, ada_b):
    depth, d, n = ada_w.shape
    tn = 1536
    return pl.pallas_call(
        _adaln_kernel,
        grid=(depth, n // tn),
        in_specs=[
            pl.BlockSpec((SUBLANE, d), lambda l, j: (0, 0)),
            pl.BlockSpec((None, d, tn), lambda l, j: (l, 0, j)),
            pl.BlockSpec((None, 1, tn), lambda l, j: (l, 0, j)),
        ],
        out_specs=pl.BlockSpec((None, SUBLANE, tn), lambda l, j: (l, 0, j)),
        out_shape=jax.ShapeDtypeStruct((depth, SUBLANE, n), F32),
        compiler_params=_cparams(("arbitrary", "arbitrary")),
        name="adaln",
    )(c_pad, ada_w, ada_b.reshape(depth, 1, n))


def _row_spec(width):
    return pl.BlockSpec((None, ROW_TILE, width), lambda b, i: (b, i, 0))


def _mod_spec(layer, k):
    return pl.BlockSpec((None, SUBLANE, D_MODEL), lambda b, i: (layer, 0, k))


def _mod_row(ref):
    return ref[pl.ds(pl.program_id(0), 1), :]


def _const_spec(shape):
    return pl.BlockSpec(shape, lambda b, i: (0,) * len(shape))


def _layer_spec(shape, layer, col_block=0):
    return pl.BlockSpec((None,) + shape, lambda b, i: (layer, 0, col_block),
                        pipeline_mode=pl.Buffered(1))


def _gla_in_kernel(x_ref, g_ref, sh_ref, sc_ref, w_ref, wgate_ref, bgate_ref,
                   q_ref, k_ref, v_ref, r_ref, la_ref):
    hk = GLA_HEADS * GLA_DK
    hv = GLA_HEADS * GLA_DV
    h = _norm_mod(x_ref[...], g_ref[...], _mod_row(sh_ref), _mod_row(sc_ref)).astype(BF16)

    def proj(lo, width):
        return _dot(h, w_ref[:, lo:lo + width].astype(BF16))

    q_ref[...] = (proj(0, hk) * (GLA_DK ** -0.5)).astype(BF16)
    k_ref[...] = proj(hk, hk).astype(BF16)
    v_ref[...] = proj(2 * hk, hv).astype(BF16)
    r_ref[...] = proj(2 * hk + hv, hv).astype(BF16)
    low_rank = proj(2 * hk + 2 * hv, GLA_GATE_RANK).astype(BF16)
    z = _dot(low_rank, wgate_ref[...].astype(BF16)) + bgate_ref[...]
    log_sig = jnp.minimum(z, 0.0) - jnp.log(1.0 + jnp.exp(-jnp.abs(z)))
    la_ref[...] = log_sig / GLA_TAU


def _gla_in(x, mod_all, layer, gain, w_in, j, w_gate, b_gate):
    b, s, d = x.shape
    hk = GLA_HEADS * GLA_DK
    hv = GLA_HEADS * GLA_DV
    return pl.pallas_call(
        _gla_in_kernel,
        grid=(b, s // ROW_TILE),
        in_specs=[
            _row_spec(d), _const_spec((1, d)), _mod_spec(layer, 0), _mod_spec(layer, 1),
            _layer_spec((d, w_in.shape[-1]), j), _layer_spec((GLA_GATE_RANK, hk), j),
            _const_spec((1, hk)),
        ],
        out_specs=[_row_spec(hk), _row_spec(hk), _row_spec(hv), _row_spec(hv), _row_spec(hk)],
        out_shape=[
            jax.ShapeDtypeStruct((b, s, hk), BF16), jax.ShapeDtypeStruct((b, s, hk), BF16),
            jax.ShapeDtypeStruct((b, s, hv), BF16), jax.ShapeDtypeStruct((b, s, hv), BF16),
            jax.ShapeDtypeStruct((b, s, hk), F32),
        ],
        compiler_params=_cparams(("arbitrary", "arbitrary")),
        name="gla_in",
    )(x, gain.reshape(1, d), mod_all, mod_all, w_in, w_gate, b_gate.reshape(1, hk))


def _gla_kernel(q_ref, k_ref, v_ref, r_ref, la_ref, go_ref, o_ref, state_ref):
    c, sub = GLA_CHUNK, GLA_SUB
    nsub = c // sub
    assert nsub * c == 2 * LANE

    @pl.when(pl.program_id(2) == 0)
    def _():
        state_ref[...] = jnp.zeros_like(state_ref)

    row = lax.broadcasted_iota(jnp.int32, (c, c), 0)
    col = lax.broadcasted_iota(jnp.int32, (c, c), 1)
    tri = jnp.where(row >= col, 1.0, 0.0).astype(BF16)
    qrow = lax.broadcasted_iota(jnp.int32, (c, LANE), 0)
    lane = lax.broadcasted_iota(jnp.int32, (c, LANE), 1)
    causal = (lane % c) <= qrow
    keep_lo = causal & ((lane // c) == (qrow // sub))
    keep_hi = causal & ((lane // c) + LANE // c == (qrow // sub))

    n_chunks = GLA_TIME_BLOCK // c
    cs = [slice(ci * c, (ci + 1) * c) for ci in range(n_chunks)]

    bs = []
    for sl in cs:
        la = la_ref[sl, :]
        la_hi = la.astype(BF16)
        la_lo = (la - la_hi.astype(F32)).astype(BF16)
        bs.append(_dot(tri, la_hi) + _dot(tri, la_lo))

    qes, attns, kd_ts, decays = [], [], [], []
    for sl, b in zip(cs, bs):
        q = q_ref[sl, :].astype(F32)
        k = k_ref[sl, :].astype(F32)
        qes.append((q * jnp.exp(b)).astype(BF16))
        q_parts, k_parts = [], []
        for i in range(nsub):
            rows = slice(i * sub, (i + 1) * sub)
            live = (i + 1) * sub
            if i == 0:
                q_parts.append(q[rows] * jnp.exp(b[rows]))
                k_var = k[:live] * jnp.exp(-b[:live])
            else:
                anchor = b[i * sub - 1:i * sub, :]
                q_parts.append(q[rows] * jnp.exp(b[rows] - anchor))
                k_var = k[:live] * jnp.exp(anchor - b[:live])
            k_parts.append(k_var)
            if live < c:
                k_parts.append(jnp.zeros((c - live, GLA_DK), F32))
        q_anch = jnp.concatenate(q_parts, axis=0).astype(BF16)
        k_stack = jnp.concatenate(k_parts, axis=0).astype(BF16)
        prod = _dot_nt(q_anch, k_stack)
        attns.append(jnp.where(keep_lo, prod[:, :LANE],
                               jnp.where(keep_hi, prod[:, LANE:], 0.0)).astype(BF16))
        b_last = b[c - 1:c, :]
        kd_ts.append((k * jnp.exp(b_last - b)).T.astype(BF16))
        decays.append(jnp.broadcast_to(jnp.exp(b_last), (LANE, GLA_DK)).T)

    state = state_ref[...]
    gain = go_ref[...]
    for ci, sl in enumerate(cs):
        v = v_ref[sl, :]
        v2 = jnp.concatenate([v, v], axis=0)
        o = _dot(qes[ci], state.astype(BF16)) + _dot(attns[ci], v2)
        decay = jnp.concatenate([decays[ci]] * (GLA_DV // LANE), axis=1)
        state = state * decay + _dot(kd_ts[ci], v)
        r = r_ref[sl, :].astype(F32)
        o_ref[sl, :] = (_rms(o, gain) * (r * _sigmoid(r))).astype(BF16)
    state_ref[...] = state


def _gla(q, k, v, r, la, g_out):
    b, s, _ = q.shape
    t = GLA_TIME_BLOCK

    def spec(width):
        return pl.BlockSpec((None, t, width), lambda bi, h, ti: (bi, ti, h))

    return pl.pallas_call(
        _gla_kernel,
        grid=(b, GLA_HEADS, s // t),
        in_specs=[spec(GLA_DK), spec(GLA_DK), spec(GLA_DV), spec(GLA_DV), spec(GLA_DK),
                  pl.BlockSpec((1, GLA_DV), lambda bi, h, ti: (0, 0))],
        out_specs=spec(GLA_DV),
        out_shape=jax.ShapeDtypeStruct((b, s, GLA_HEADS * GLA_DV), BF16),
        scratch_shapes=[pltpu.VMEM((GLA_DK, GLA_DV), F32)],
        compiler_params=_cparams(("arbitrary", "arbitrary", "arbitrary")),
        name="gla_scan",
    )(q, k, v, r, la, g_out.reshape(1, GLA_DV))


def _out_res_kernel(a_ref, w_ref, x_ref, gate_ref, *rest, final_norm):
    o_ref = rest[-1]
    y = x_ref[...] + _mod_row(gate_ref) * _dot(a_ref[...], w_ref[...].astype(BF16))
    if final_norm:
        y = _rms(y, rest[0][...])
    o_ref[...] = y


def _out_res(a, w, j, x, mod_all, layer, gate_idx, final_gain=None):
    b, s, d = x.shape
    kdim = a.shape[-1]
    in_specs = [_row_spec(kdim), _layer_spec((kdim, d), j), _row_spec(d), _mod_spec(layer, gate_idx)]
    args = [a, w, x, mod_all]
    if final_gain is not None:
        in_specs.append(_const_spec((1, d)))
        args.append(final_gain.reshape(1, d))
    return pl.pallas_call(
        functools.partial(_out_res_kernel, final_norm=final_gain is not None),
        grid=(b, s // ROW_TILE),
        in_specs=in_specs,
        out_specs=_row_spec(d),
        out_shape=jax.ShapeDtypeStruct((b, s, d), F32),
        compiler_params=_cparams(("arbitrary", "arbitrary")),
        name="out_res",
    )(*args)


def _ffn_in_kernel(x_ref, g_ref, sh_ref, sc_ref, wg_ref, wu_ref, a_ref):
    h = _norm_mod(x_ref[...], g_ref[...], _mod_row(sh_ref), _mod_row(sc_ref)).astype(BF16)
    for j in range(0, D_FF, FFN_COL_CHUNK):
        cols = slice(j, j + FFN_COL_CHUNK)
        gate = _dot(h, wg_ref[:, cols].astype(BF16))
        up = _dot(h, wu_ref[:, cols].astype(BF16))
        a_ref[:, cols] = (gate * _sigmoid(gate) * up).astype(BF16)


def _ffn_in(x, mod_all, layer, gain, w_in):
    b, s, d = x.shape
    return pl.pallas_call(
        _ffn_in_kernel,
        grid=(b, s // ROW_TILE),
        in_specs=[_row_spec(d), _const_spec((1, d)), _mod_spec(layer, 3), _mod_spec(layer, 4),
                  _layer_spec((d, D_FF), layer, 0), _layer_spec((d, D_FF), layer, 1)],
        out_specs=_row_spec(D_FF),
        out_shape=jax.ShapeDtypeStruct((b, s, D_FF), BF16),
        compiler_params=_cparams(("arbitrary", "arbitrary")),
        name="ffn_in",
    )(x, gain.reshape(1, d), mod_all, mod_all, w_in, w_in)


def _mla_in_kernel(x_ref, g_ref, sh_ref, sc_ref, pos_ref, freq_ref, phase_ref, win_ref, gq_ref,
                   wq_ref, gkv_ref, wkn_ref, wvt_ref, q_ref, kc_ref, vt_ref, *, q_scale):
    h = _norm_mod(x_ref[...], g_ref[...], _mod_row(sh_ref), _mod_row(sc_ref)).astype(BF16)
    proj = _dot(h, win_ref[...])
    cq = _rms(proj[:, :MLA_Q_RANK], gq_ref[...]).astype(BF16)
    ckv = _rms(proj[:, MLA_Q_RANK:MLA_Q_RANK + MLA_KV_RANK], gkv_ref[...]).astype(BF16)

    pos = jnp.broadcast_to(pos_ref[...].astype(F32), (LANE, ROW_TILE)).T
    table = jnp.cos(pos * freq_ref[...] + phase_ref[...])
    rope_lanes = lax.broadcasted_iota(jnp.int32, (ROW_TILE, LANE), 1) < MLA_ROPE

    def rope(t):
        u = t * table
        return jnp.where(rope_lanes, u + pltpu.roll(u, MLA_ROPE, axis=1), 0.0)

    k_rope = rope(proj[:, MLA_Q_RANK + MLA_KV_RANK:]).astype(BF16)
    v_t = _dot_nt(wvt_ref[...], ckv)
    ones = jnp.ones((MLA_VT_ROWS - MLA_VDIM, v_t.shape[1]), BF16)
    k_nope = _dot(ckv, wkn_ref[...])
    q_all = _dot(cq, wq_ref[...])
    for hd in range(MLA_HEADS):
        lo = hd * MLA_QK_PAD
        mid = lo + MLA_NOPE
        hi = lo + MLA_QK_PAD
        q_ref[:, lo:mid] = (q_all[:, lo:mid] * q_scale).astype(BF16)
        q_ref[:, mid:hi] = (rope(q_all[:, mid:hi]) * q_scale).astype(BF16)
        kc_ref[:, lo:mid] = k_nope[:, hd * MLA_NOPE:(hd + 1) * MLA_NOPE].astype(BF16)
        kc_ref[:, mid:hi] = k_rope
        vt_ref[hd, :MLA_VDIM, :] = v_t[hd * MLA_VDIM:(hd + 1) * MLA_VDIM, :].astype(BF16)
        vt_ref[hd, MLA_VDIM:, :] = ones


def _half_split(w_rope):
    even, odd = w_rope[..., 0::2], w_rope[..., 1::2]
    return jnp.concatenate([even, odd, odd, even], axis=-1)


def _mla_in(x, mod_all, layer, gain, positions, w_in, g_q, w_q_up, g_kv, w_kv_up):
    b, s, d = x.shape
    hq = MLA_HEADS * MLA_QK_PAD
    hn = MLA_HEADS * MLA_NOPE
    hv = MLA_HEADS * MLA_VDIM
    lat = MLA_Q_RANK + MLA_KV_RANK
    half = MLA_ROPE // 2
    win = jnp.concatenate([w_in[:, :lat], _half_split(w_in[:, lat:])], axis=-1).astype(BF16)
    wq3 = w_q_up.reshape(MLA_Q_RANK, MLA_HEADS, MLA_NOPE + MLA_ROPE)
    wq = jnp.concatenate([wq3[..., :MLA_NOPE], _half_split(wq3[..., MLA_NOPE:])], axis=-1)
    wq = wq.reshape(MLA_Q_RANK, hq).astype(BF16)
    wkv3 = w_kv_up.reshape(MLA_KV_RANK, MLA_HEADS, MLA_NOPE + MLA_VDIM)
    wkn = wkv3[..., :MLA_NOPE].reshape(MLA_KV_RANK, hn).astype(BF16)
    wvt = wkv3[..., MLA_NOPE:].reshape(MLA_KV_RANK, hv).T.astype(BF16)
    win_w = win.shape[-1]
    inv_freq = ROPE_THETA ** (-jnp.arange(0, MLA_ROPE, 2, dtype=F32) / MLA_ROPE)
    freq_row = jnp.tile(inv_freq, LANE // half).reshape(1, LANE)
    quarter = jnp.full((half,), math.pi / 2, F32)
    phase_row = jnp.concatenate([jnp.zeros((2 * half,), F32), quarter, -quarter]).reshape(1, LANE)
    q_scale = (MLA_NOPE + MLA_ROPE) ** -0.5 * math.log2(math.e)
    return pl.pallas_call(
        functools.partial(_mla_in_kernel, q_scale=q_scale),
        grid=(b, s // ROW_TILE),
        in_specs=[
            _row_spec(d), _const_spec((1, d)), _mod_spec(layer, 0), _mod_spec(layer, 1),
            pl.BlockSpec((None, 1, ROW_TILE), lambda bi, i: (bi, 0, i)),
            _const_spec((1, LANE)), _const_spec((1, LANE)),
            _const_spec((d, win_w)), _const_spec((1, MLA_Q_RANK)), _const_spec((MLA_Q_RANK, hq)),
            _const_spec((1, MLA_KV_RANK)), _const_spec((MLA_KV_RANK, hn)),
            _const_spec((hv, MLA_KV_RANK)),
        ],
        out_specs=[_row_spec(hq), _row_spec(hq),
                   pl.BlockSpec((None, MLA_HEADS, MLA_VT_ROWS, ROW_TILE), lambda bi, i: (bi, 0, 0, i))],
        out_shape=[jax.ShapeDtypeStruct((b, s, hq), BF16), jax.ShapeDtypeStruct((b, s, hq), BF16),
                   jax.ShapeDtypeStruct((b, MLA_HEADS, MLA_VT_ROWS, s), BF16)],
        compiler_params=_cparams(("arbitrary", "arbitrary")),
        name="mla_in",
    )(x, gain.reshape(1, d), mod_all, mod_all, positions.reshape(b, 1, s), freq_row, phase_row,
      win, g_q.reshape(1, -1), wq, g_kv.reshape(1, -1), wkn, wvt)


def _attn_kernel(q_ref, k_ref, vt_ref, o_ref, s0_ref, s1_ref, mb0_ref, mb1_ref, m_ref, acc_ref):
    tq, tk = ATT_Q_BLOCK, ATT_KV_BLOCK
    per_q = tq // tk
    assert per_q == 2
    qi = pl.program_id(2)
    q = q_ref[...]
    m_ref[...] = jnp.full_like(m_ref, -jnp.inf)
    acc_ref[...] = jnp.zeros_like(acc_ref)

    def scores(kj, s_ref, mb_ref, diag_offset=None):
        start = pl.multiple_of(kj * tk, tk)
        s_t = _dot_nt(k_ref[pl.ds(start, tk), :], q)
        if diag_offset is not None:
            key = lax.broadcasted_iota(jnp.int32, (tk, tq), 0) + diag_offset
            qry = lax.broadcasted_iota(jnp.int32, (tk, tq), 1)
            s_t = jnp.where(key <= qry, s_t, -jnp.inf)
        s_ref[...] = s_t
        mb_ref[...] = jnp.max(s_t, axis=0, keepdims=True)

    def accumulate(kj, s_ref, mb_ref):
        start = pl.multiple_of(kj * tk, tk)
        m_prev = m_ref[...]
        m_new = jnp.maximum(m_prev, mb_ref[...])
        p_t = jnp.exp2((s_ref[...] - m_new).astype(BF16))
        alpha = jnp.exp2(m_prev - m_new)
        acc_ref[...] = alpha * acc_ref[...] + _dot(vt_ref[:, pl.ds(start, tk)], p_t)
        m_ref[...] = m_new

    first = per_q * qi
    scores(first, s0_ref, mb0_ref, diag_offset=0)
    scores(first + 1, s1_ref, mb1_ref, diag_offset=tk)
    accumulate(first, s0_ref, mb0_ref)

    def pair(p, pending):
        scores(2 * p, s0_ref, mb0_ref)
        accumulate(pending, s1_ref, mb1_ref)
        scores(2 * p + 1, s1_ref, mb1_ref)
        accumulate(2 * p, s0_ref, mb0_ref)
        return 2 * p + 1

    pending = lax.fori_loop(0, qi, pair, first + 1)
    accumulate(pending, s1_ref, mb1_ref)

    acc = acc_ref[...]
    o_t = acc[:MLA_VDIM, :] / acc[MLA_VDIM:MLA_VDIM + 1, :]
    o_ref[...] = o_t.T.astype(BF16)


def _attention(q, kc, vt):
    b, _, _, s = vt.shape
    tq, tk = ATT_Q_BLOCK, ATT_KV_BLOCK
    return pl.pallas_call(
        _attn_kernel,
        grid=(b, MLA_HEADS, s // tq),
        in_specs=[
            pl.BlockSpec((None, tq, MLA_QK_PAD), lambda bi, h, i: (bi, i, h)),
            pl.BlockSpec((None, s, MLA_QK_PAD), lambda bi, h, i: (bi, 0, h)),
            pl.BlockSpec((None, None, MLA_VT_ROWS, s), lambda bi, h, i: (bi, h, 0, 0)),
        ],
        out_specs=pl.BlockSpec((None, tq, MLA_VDIM), lambda bi, h, i: (bi, i, h)),
        out_shape=jax.ShapeDtypeStruct((b, s, MLA_HEADS * MLA_VDIM), BF16),
        scratch_shapes=[pltpu.VMEM((tk, tq), F32), pltpu.VMEM((tk, tq), F32),
                        pltpu.VMEM((1, tq), F32), pltpu.VMEM((1, tq), F32),
                        pltpu.VMEM((1, tq), F32), pltpu.VMEM((MLA_VT_ROWS, tq), F32)],
        compiler_params=_cparams(("arbitrary", "arbitrary", "arbitrary")),
        name="mla_attention",
    )(q, kc, vt)


def kernel(x, c, positions, ada_w, ada_b, norm_mix, norm_ffn, gla_w_in, gla_w_gate, gla_b_gate,
           gla_g_out, gla_w_out, mla_w_in, mla_g_q, mla_w_q_up, mla_g_kv, mla_w_kv_up, mla_w_out,
           ffn_w_in, ffn_w_out, final_norm):
    batch = x.shape[0]
    depth = ada_w.shape[0]
    assert batch <= SUBLANE and ada_w.shape[-1] == N_MOD * D_MODEL
    c_pad = jnp.pad(c, ((0, SUBLANE - batch), (0, 0)))
    mod_all = _adaln(c_pad, ada_w, ada_b)

    for i in range(depth):
        j = i // 2
        if i % 2 == 0:
            q, k, v, r, la = _gla_in(x, mod_all, i, norm_mix[i], gla_w_in, j, gla_w_gate,
                                     gla_b_gate[j])
            mixed = _gla(q, k, v, r, la, gla_g_out[j])
            x = _out_res(mixed, gla_w_out, j, x, mod_all, i, 2)
        else:
            q, kc, vt = _mla_in(x, mod_all, i, norm_mix[i], positions, mla_w_in[j], mla_g_q[j],
                                mla_w_q_up[j], mla_g_kv[j], mla_w_kv_up[j])
            mixed = _attention(q, kc, vt)
            x = _out_res(mixed, mla_w_out, j, x, mod_all, i, 2)
        act = _ffn_in(x, mod_all, i, norm_ffn[i], ffn_w_in)
        last = i == depth - 1
        x = _out_res(act, ffn_w_out, i, x, mod_all, i, 5, final_gain=final_norm if last else None)
    return x
```

```python
import functools
import math

import jax
import jax.numpy as jnp
from jax import lax
from jax.experimental import pallas as pl
from jax.experimental.pallas import tpu as pltpu

F32 = jnp.float32
BF16 = jnp.bfloat16

D_MODEL = 1024
EPS = 1e-6

GLA_HEADS = 4
GLA_DK = 128
GLA_DV = 256
GLA_GATE_RANK = 16
GLA_TAU = 16.0
GLA_CHUNK = 64
GLA_SUB = 16

MLA_HEADS = 8
MLA_NOPE = 128
MLA_ROPE = 64
MLA_VDIM = 128
MLA_Q_RANK = 384
MLA_KV_RANK = 256
ROPE_THETA = 10000.0
MLA_QK_PAD = 256
MLA_VT_ROWS = MLA_VDIM + 16

D_FF = 2816
N_MOD = 6

LANE = 128
SUBLANE = 8
ROW_TILE = 512
GLA_TIME_BLOCK = 512
ATT_Q_BLOCK = 1024
ATT_KV_BLOCK = 512
FFN_COL_CHUNK = 256
VMEM_LIMIT = 56 * 1024 * 1024


def _cparams(sem):
    return pltpu.CompilerParams(dimension_semantics=sem, vmem_limit_bytes=VMEM_LIMIT)


def _dot(a, b):
    return jnp.dot(a, b, preferred_element_type=F32)


def _dot_nt(a, b):
    return lax.dot_general(a, b, (((1,), (1,)), ((), ())), preferred_element_type=F32)


def _sigmoid(x):
    return 1.0 / (1.0 + jnp.exp(-x))


def _rms(x, g):
    return x * lax.rsqrt(jnp.mean(x * x, axis=-1, keepdims=True) + EPS) * g


def _norm_mod(x, g, shift, scale):
    return _rms(x, g) * (1.0 + scale) + shift


def _adaln_kernel(c_ref, w_ref, b_ref, o_ref):
    c = c_ref[...]
    o_ref[...] = _dot(c * _sigmoid(c), w_ref[...]) + b_ref[...]


def _adaln(c_pad, ada_w, ada_b):
    depth, d, n = ada_w.shape
    tn = 1536
    return pl.pallas_call(
        _adaln_kernel,
        grid=(depth, n // tn),
        in_specs=[
            pl.BlockSpec((SUBLANE, d), lambda l, j: (0, 0)),
            pl.BlockSpec((None, d, tn), lambda l, j: (l, 0, j)),
            pl.BlockSpec((None, 1, tn), lambda l, j: (l, 0, j)),
        ],
        out_specs=pl.BlockSpec((None, SUBLANE, tn), lambda l, j: (l, 0, j)),
        out_shape=jax.ShapeDtypeStruct((depth, SUBLANE, n), F32),
        compiler_params=_cparams(("arbitrary", "arbitrary")),
        name="adaln",
    )(c_pad, ada_w, ada_b.reshape(depth, 1, n))


def _row_spec(width):
    return pl.BlockSpec((None, ROW_TILE, width), lambda b, i: (b, i, 0))


def _mod_spec(layer, k):
    return pl.BlockSpec((None, SUBLANE, D_MODEL), lambda b, i: (layer, 0, k))


def _mod_row(ref):
    return ref[pl.ds(pl.program_id(0), 1), :]


def _const_spec(shape):
    return pl.BlockSpec(shape, lambda b, i: (0,) * len(shape))


def _layer_spec(shape, layer, col_block=0):
    return pl.BlockSpec((None,) + shape, lambda b, i: (layer, 0, col_block),
                        pipeline_mode=pl.Buffered(1))


def _gla_in_kernel(x_ref, g_ref, sh_ref, sc_ref, w_ref, wgate_ref, bgate_ref,
                   q_ref, k_ref, v_ref, r_ref, la_ref):
    hk = GLA_HEADS * GLA_DK
    hv = GLA_HEADS * GLA_DV
    h = _norm_mod(x_ref[...], g_ref[...], _mod_row(sh_ref), _mod_row(sc_ref)).astype(BF16)

    def proj(lo, width):
        return _dot(h, w_ref[:, lo:lo + width].astype(BF16))

    q_ref[...] = (proj(0, hk) * (GLA_DK ** -0.5)).astype(BF16)
    k_ref[...] = proj(hk, hk).astype(BF16)
    v_ref[...] = proj(2 * hk, hv).astype(BF16)
    r_ref[...] = proj(2 * hk + hv, hv).astype(BF16)
    low_rank = proj(2 * hk + 2 * hv, GLA_GATE_RANK).astype(BF16)
    z = _dot(low_rank, wgate_ref[...].astype(BF16)) + bgate_ref[...]
    log_sig = jnp.minimum(z, 0.0) - jnp.log(1.0 + jnp.exp(-jnp.abs(z)))
    la_ref[...] = log_sig / GLA_TAU


def _gla_in(x, mod_all, layer, gain, w_in, j, w_gate, b_gate):
    b, s, d = x.shape
    hk = GLA_HEADS * GLA_DK
    hv = GLA_HEADS * GLA_DV
    return pl.pallas_call(
        _gla_in_kernel,
        grid=(b, s // ROW_TILE),
        in_specs=[
            _row_spec(d), _const_spec((1, d)), _mod_spec(layer, 0), _mod_spec(layer, 1),
            _layer_spec((d, w_in.shape[-1]), j), _layer_spec((GLA_GATE_RANK, hk), j),
            _const_spec((1, hk)),
        ],
        out_specs=[_row_spec(hk), _row_spec(hk), _row_spec(hv), _row_spec(hv), _row_spec(hk)],
        out_shape=[
            jax.ShapeDtypeStruct((b, s, hk), BF16), jax.ShapeDtypeStruct((b, s, hk), BF16),
            jax.ShapeDtypeStruct((b, s, hv), BF16), jax.ShapeDtypeStruct((b, s, hv), BF16),
            jax.ShapeDtypeStruct((b, s, hk), F32),
        ],
        compiler_params=_cparams(("arbitrary", "arbitrary")),
        name="gla_in",
    )(x, gain.reshape(1, d), mod_all, mod_all, w_in, w_gate, b_gate.reshape(1, hk))


def _gla_kernel(q_ref, k_ref, v_ref, r_ref, la_ref, go_ref, o_ref, state_ref):
    c, sub = GLA_CHUNK, GLA_SUB
    nsub = c // sub
    assert nsub * c == 2 * LANE

    @pl.when(pl.program_id(2) == 0)
    def _():
        state_ref[...] = jnp.zeros_like(state_ref)

    row = lax.broadcasted_iota(jnp.int32, (c, c), 0)
    col = lax.broadcasted_iota(jnp.int32, (c, c), 1)
    tri = jnp.where(row >= col, 1.0, 0.0).astype(BF16)
    qrow = lax.broadcasted_iota(jnp.int32, (c, LANE), 0)
    lane = lax.broadcasted_iota(jnp.int32, (c, LANE), 1)
    causal = (lane % c) <= qrow
    keep_lo = causal & ((lane // c) == (qrow // sub))
    keep_hi = causal & ((lane // c) + LANE // c == (qrow // sub))

    n_chunks = GLA_TIME_BLOCK // c
    cs = [slice(ci * c, (ci + 1) * c) for ci in range(n_chunks)]

    bs = []
    for sl in cs:
        la = la_ref[sl, :]
        la_hi = la.astype(BF16)
        la_lo = (la - la_hi.astype(F32)).astype(BF16)
        bs.append(_dot(tri, la_hi) + _dot(tri, la_lo))

    qes, attns, kd_ts, decays = [], [], [], []
    for sl, b in zip(cs, bs):
        q = q_ref[sl, :].astype(F32)
        k = k_ref[sl, :].astype(F32)
        qes.append((q * jnp.exp(b)).astype(BF16))
        q_parts, k_parts = [], []
        for i in range(nsub):
            rows = slice(i * sub, (i + 1) * sub)
            live = (i + 1) * sub
            if i == 0:
                q_parts.append(q[rows] * jnp.exp(b[rows]))
                k_var = k[:live] * jnp.exp(-b[:live])
            else:
                anchor = b[i * sub - 1:i * sub, :]
                q_parts.append(q[rows] * jnp.exp(b[rows] - anchor))
                k_var = k[:live] * jnp.exp(anchor - b[:live])
            k_parts.append(k_var)
            if live < c:
                k_parts.append(jnp.zeros((c - live, GLA_DK), F32))
        q_anch = jnp.concatenate(q_parts, axis=0).astype(BF16)
        k_stack = jnp.concatenate(k_parts, axis=0).astype(BF16)
        prod = _dot_nt(q_anch, k_stack)
        attns.append(jnp.where(keep_lo, prod[:, :LANE],
                               jnp.where(keep_hi, prod[:, LANE:], 0.0)).astype(BF16))
        b_last = b[c - 1:c, :]
        kd_ts.append((k * jnp.exp(b_last - b)).T.astype(BF16))
        decays.append(jnp.broadcast_to(jnp.exp(b_last), (LANE, GLA_DK)).T)

    state = state_ref[...]
    gain = go_ref[...]
    for ci, sl in enumerate(cs):
        v = v_ref[sl, :]
        v2 = jnp.concatenate([v, v], axis=0)
        o = _dot(qes[ci], state.astype(BF16)) + _dot(attns[ci], v2)
        decay = jnp.concatenate([decays[ci]] * (GLA_DV // LANE), axis=1)
        state = state * decay + _dot(kd_ts[ci], v)
        r = r_ref[sl, :].astype(F32)
        o_ref[sl, :] = (_rms(o, gain) * (r * _sigmoid(r))).astype(BF16)
    state_ref[...] = state


def _gla(q, k, v, r, la, g_out):
    b, s, _ = q.shape
    t = GLA_TIME_BLOCK

    def spec(width):
        return pl.BlockSpec((None, t, width), lambda bi, h, ti: (bi, ti, h))

    return pl.pallas_call(
        _gla_kernel,
        grid=(b, GLA_HEADS, s // t),
        in_specs=[spec(GLA_DK), spec(GLA_DK), spec(GLA_DV), spec(GLA_DV), spec(GLA_DK),
                  pl.BlockSpec((1, GLA_DV), lambda bi, h, ti: (0, 0))],
        out_specs=spec(GLA_DV),
        out_shape=jax.ShapeDtypeStruct((b, s, GLA_HEADS * GLA_DV), BF16),
        scratch_shapes=[pltpu.VMEM((GLA_DK, GLA_DV), F32)],
        compiler_params=_cparams(("arbitrary", "arbitrary", "arbitrary")),
        name="gla_scan",
    )(q, k, v, r, la, g_out.reshape(1, GLA_DV))


def _out_res_kernel(a_ref, w_ref, x_ref, gate_ref, *rest, final_norm):
    o_ref = rest[-1]
    y = x_ref[...] + _mod_row(gate_ref) * _dot(a_ref[...], w_ref[...].astype(BF16))
    if final_norm:
        y = _rms(y, rest[0][...])
    o_ref[...] = y


def _out_res(a, w, j, x, mod_all, layer, gate_idx, final_gain=None):
    b, s, d = x.shape
    kdim = a.shape[-1]
    in_specs = [_row_spec(kdim), _layer_spec((kdim, d), j), _row_spec(d), _mod_spec(layer, gate_idx)]
    args = [a, w, x, mod_all]
    if final_gain is not None:
        in_specs.append(_const_spec((1, d)))
        args.append(final_gain.reshape(1, d))
    return pl.pallas_call(
        functools.partial(_out_res_kernel, final_norm=final_gain is not None),
        grid=(b, s // ROW_TILE),
        in_specs=in_specs,
        out_specs=_row_spec(d),
        out_shape=jax.ShapeDtypeStruct((b, s, d), F32),
        compiler_params=_cparams(("arbitrary", "arbitrary")),
        name="out_res",
    )(*args)


def _mix_out_ffn_in_kernel(m_ref, wo_ref, x_ref, g1_ref, g_ref, sh_ref, sc_ref, wg_ref, wu_ref,
                           x1_ref, a_ref):
    x1 = x_ref[...] + _mod_row(g1_ref) * _dot(m_ref[...], wo_ref[...].astype(BF16))
    x1_ref[...] = x1
    h = _norm_mod(x1, g_ref[...], _mod_row(sh_ref), _mod_row(sc_ref)).astype(BF16)
    for j in range(0, D_FF, FFN_COL_CHUNK):
        cols = slice(j, j + FFN_COL_CHUNK)
        gate = _dot(h, wg_ref[:, cols].astype(BF16))
        up = _dot(h, wu_ref[:, cols].astype(BF16))
        a_ref[:, cols] = (gate * _sigmoid(gate) * up).astype(BF16)


def _mix_out_ffn_in(mixed, w_o, j, x, mod_all, layer, gain, w_in):
    b, s, d = x.shape
    kdim = mixed.shape[-1]
    return pl.pallas_call(
        _mix_out_ffn_in_kernel,
        grid=(b, s // ROW_TILE),
        in_specs=[_row_spec(kdim), _layer_spec((kdim, d), j), _row_spec(d), _mod_spec(layer, 2),
                  _const_spec((1, d)), _mod_spec(layer, 3), _mod_spec(layer, 4),
                  _layer_spec((d, D_FF), layer, 0), _layer_spec((d, D_FF), layer, 1)],
        out_specs=[_row_spec(d), _row_spec(D_FF)],
        out_shape=[jax.ShapeDtypeStruct((b, s, d), F32), jax.ShapeDtypeStruct((b, s, D_FF), BF16)],
        compiler_params=_cparams(("arbitrary", "arbitrary")),
        name="mix_out_ffn_in",
    )(mixed, w_o, x, mod_all, gain.reshape(1, d), mod_all, mod_all, w_in, w_in)


def _mla_in_kernel(x_ref, g_ref, sh_ref, sc_ref, pos_ref, freq_ref, phase_ref, win_ref, gq_ref,
                   wq_ref, gkv_ref, wkn_ref, wvt_ref, q_ref, kc_ref, vt_ref, *, q_scale):
    h = _norm_mod(x_ref[...], g_ref[...], _mod_row(sh_ref), _mod_row(sc_ref)).astype(BF16)
    proj = _dot(h, win_ref[...])
    cq = _rms(proj[:, :MLA_Q_RANK], gq_ref[...]).astype(BF16)
    ckv = _rms(proj[:, MLA_Q_RANK:MLA_Q_RANK + MLA_KV_RANK], gkv_ref[...]).astype(BF16)

    pos = jnp.broadcast_to(pos_ref[...].astype(F32), (LANE, ROW_TILE)).T
    table = jnp.cos(pos * freq_ref[...] + phase_ref[...])
    rope_lanes = lax.broadcasted_iota(jnp.int32, (ROW_TILE, LANE), 1) < MLA_ROPE

    def rope(t):
        u = t * table
        return jnp.where(rope_lanes, u + pltpu.roll(u, MLA_ROPE, axis=1), 0.0)

    k_rope = rope(proj[:, MLA_Q_RANK + MLA_KV_RANK:]).astype(BF16)
    v_t = _dot_nt(wvt_ref[...], ckv)
    ones = jnp.ones((MLA_VT_ROWS - MLA_VDIM, v_t.shape[1]), BF16)
    k_nope = _dot(ckv, wkn_ref[...])
    q_all = _dot(cq, wq_ref[...])
    for hd in range(MLA_HEADS):
        lo = hd * MLA_QK_PAD
        mid = lo + MLA_NOPE
        hi = lo + MLA_QK_PAD
        q_ref[:, lo:mid] = (q_all[:, lo:mid] * q_scale).astype(BF16)
        q_ref[:, mid:hi] = (rope(q_all[:, mid:hi]) * q_scale).astype(BF16)
        kc_ref[:, lo:mid] = k_nope[:, hd * MLA_NOPE:(hd + 1) * MLA_NOPE].astype(BF16)
        kc_ref[:, mid:hi] = k_rope
        vt_ref[hd, :MLA_VDIM, :] = v_t[hd * MLA_VDIM:(hd + 1) * MLA_VDIM, :].astype(BF16)
        vt_ref[hd, MLA_VDIM:, :] = ones


def _half_split(w_rope):
    even, odd = w_rope[..., 0::2], w_rope[..., 1::2]
    return jnp.concatenate([even, odd, odd, even], axis=-1)


def _mla_in(x, mod_all, layer, gain, positions, w_in, g_q, w_q_up, g_kv, w_kv_up):
    b, s, d = x.shape
    hq = MLA_HEADS * MLA_QK_PAD
    hn = MLA_HEADS * MLA_NOPE
    hv = MLA_HEADS * MLA_VDIM
    lat = MLA_Q_RANK + MLA_KV_RANK
    half = MLA_ROPE // 2
    win = jnp.concatenate([w_in[:, :lat], _half_split(w_in[:, lat:])], axis=-1).astype(BF16)
    wq3 = w_q_up.reshape(MLA_Q_RANK, MLA_HEADS, MLA_NOPE + MLA_ROPE)
    wq = jnp.concatenate([wq3[..., :MLA_NOPE], _half_split(wq3[..., MLA_NOPE:])], axis=-1)
    wq = wq.reshape(MLA_Q_RANK, hq).astype(BF16)
    wkv3 = w_kv_up.reshape(MLA_KV_RANK, MLA_HEADS, MLA_NOPE + MLA_VDIM)
    wkn = wkv3[..., :MLA_NOPE].reshape(MLA_KV_RANK, hn).astype(BF16)
    wvt = wkv3[..., MLA_NOPE:].reshape(MLA_KV_RANK, hv).T.astype(BF16)
    win_w = win.shape[-1]
    inv_freq = ROPE_THETA ** (-jnp.arange(0, MLA_ROPE, 2, dtype=F32) / MLA_ROPE)
    freq_row = jnp.tile(inv_freq, LANE // half).reshape(1, LANE)
    quarter = jnp.full((half,), math.pi / 2, F32)
    phase_row = jnp.concatenate([jnp.zeros((2 * half,), F32), quarter, -quarter]).reshape(1, LANE)
    q_scale = (MLA_NOPE + MLA_ROPE) ** -0.5 * math.log2(math.e)
    return pl.pallas_call(
        functools.partial(_mla_in_kernel, q_scale=q_scale),
        grid=(b, s // ROW_TILE),
        in_specs=[
            _row_spec(d), _const_spec((1, d)), _mod_spec(layer, 0), _mod_spec(layer, 1),
            pl.BlockSpec((None, 1, ROW_TILE), lambda bi, i: (bi, 0, i)),
            _const_spec((1, LANE)), _const_spec((1, LANE)),
            _const_spec((d, win_w)), _const_spec((1, MLA_Q_RANK)), _const_spec((MLA_Q_RANK, hq)),
            _const_spec((1, MLA_KV_RANK)), _const_spec((MLA_KV_RANK, hn)),
            _const_spec((hv, MLA_KV_RANK)),
        ],
        out_specs=[_row_spec(hq), _row_spec(hq),
                   pl.BlockSpec((None, MLA_HEADS, MLA_VT_ROWS, ROW_TILE), lambda bi, i: (bi, 0, 0, i))],
        out_shape=[jax.ShapeDtypeStruct((b, s, hq), BF16), jax.ShapeDtypeStruct((b, s, hq), BF16),
                   jax.ShapeDtypeStruct((b, MLA_HEADS, MLA_VT_ROWS, s), BF16)],
        compiler_params=_cparams(("arbitrary", "arbitrary")),
        name="mla_in",
    )(x, gain.reshape(1, d), mod_all, mod_all, positions.reshape(b, 1, s), freq_row, phase_row,
      win, g_q.reshape(1, -1), wq, g_kv.reshape(1, -1), wkn, wvt)


def _attn_kernel(q_ref, k_ref, vt_ref, o_ref, s0_ref, s1_ref, mb0_ref, mb1_ref, m_ref, acc_ref):
    tq, tk = ATT_Q_BLOCK, ATT_KV_BLOCK
    per_q = tq // tk
    assert per_q == 2
    qi = pl.program_id(2)
    q = q_ref[...]
    m_ref[...] = jnp.full_like(m_ref, -jnp.inf)
    acc_ref[...] = jnp.zeros_like(acc_ref)

    def scores(kj, s_ref, mb_ref, diag_offset=None):
        start = pl.multiple_of(kj * tk, tk)
        s_t = _dot_nt(k_ref[pl.ds(start, tk), :], q)
        if diag_offset is not None:
            key = lax.broadcasted_iota(jnp.int32, (tk, tq), 0) + diag_offset
            qry = lax.broadcasted_iota(jnp.int32, (tk, tq), 1)
            s_t = jnp.where(key <= qry, s_t, -jnp.inf)
        s_ref[...] = s_t
        mb_ref[...] = jnp.max(s_t, axis=0, keepdims=True)

    def accumulate(kj, s_ref, mb_ref):
        start = pl.multiple_of(kj * tk, tk)
        m_prev = m_ref[...]
        m_new = jnp.maximum(m_prev, mb_ref[...])
        p_t = jnp.exp2((s_ref[...] - m_new).astype(BF16))
        alpha = jnp.exp2(m_prev - m_new)
        acc_ref[...] = alpha * acc_ref[...] + _dot(vt_ref[:, pl.ds(start, tk)], p_t)
        m_ref[...] = m_new

    first = per_q * qi
    scores(first, s0_ref, mb0_ref, diag_offset=0)
    scores(first + 1, s1_ref, mb1_ref, diag_offset=tk)
    accumulate(first, s0_ref, mb0_ref)

    def pair(p, pending):
        scores(2 * p, s0_ref, mb0_ref)
        accumulate(pending, s1_ref, mb1_ref)
        scores(2 * p + 1, s1_ref, mb1_ref)
        accumulate(2 * p, s0_ref, mb0_ref)
        return 2 * p + 1

    pending = lax.fori_loop(0, qi, pair, first + 1)
    accumulate(pending, s1_ref, mb1_ref)

    acc = acc_ref[...]
    o_t = acc[:MLA_VDIM, :] / acc[MLA_VDIM:MLA_VDIM + 1, :]
    o_ref[...] = o_t.T.astype(BF16)


def _attention(q, kc, vt):
    b, _, _, s = vt.shape
    tq, tk = ATT_Q_BLOCK, ATT_KV_BLOCK
    return pl.pallas_call(
        _attn_kernel,
        grid=(b, MLA_HEADS, s // tq),
        in_specs=[
            pl.BlockSpec((None, tq, MLA_QK_PAD), lambda bi, h, i: (bi, i, h)),
            pl.BlockSpec((None, s, MLA_QK_PAD), lambda bi, h, i: (bi, 0, h)),
            pl.BlockSpec((None, None, MLA_VT_ROWS, s), lambda bi, h, i: (bi, h, 0, 0)),
        ],
        out_specs=pl.BlockSpec((None, tq, MLA_VDIM), lambda bi, h, i: (bi, i, h)),
        out_shape=jax.ShapeDtypeStruct((b, s, MLA_HEADS * MLA_VDIM), BF16),
        scratch_shapes=[pltpu.VMEM((tk, tq), F32), pltpu.VMEM((tk, tq), F32),
                        pltpu.VMEM((1, tq), F32), pltpu.VMEM((1, tq), F32),
                        pltpu.VMEM((1, tq), F32), pltpu.VMEM((MLA_VT_ROWS, tq), F32)],
        compiler_params=_cparams(("arbitrary", "arbitrary", "arbitrary")),
        name="mla_attention",
    )(q, kc, vt)


def kernel(x, c, positions, ada_w, ada_b, norm_mix, norm_ffn, gla_w_in, gla_w_gate, gla_b_gate,
           gla_g_out, gla_w_out, mla_w_in, mla_g_q, mla_w_q_up, mla_g_kv, mla_w_kv_up, mla_w_out,
           ffn_w_in, ffn_w_out, final_norm):
    batch = x.shape[0]
    depth = ada_w.shape[0]
    assert batch <= SUBLANE and ada_w.shape[-1] == N_MOD * D_MODEL
    c_pad = jnp.pad(c, ((0, SUBLANE - batch), (0, 0)))
    mod_all = _adaln(c_pad, ada_w, ada_b)

    for i in range(depth):
        j = i // 2
        if i % 2 == 0:
            q, k, v, r, la = _gla_in(x, mod_all, i, norm_mix[i], gla_w_in, j, gla_w_gate,
                                     gla_b_gate[j])
            mixed = _gla(q, k, v, r, la, gla_g_out[j])
            w_o = gla_w_out
        else:
            q, kc, vt = _mla_in(x, mod_all, i, norm_mix[i], positions, mla_w_in[j], mla_g_q[j],
                                mla_w_q_up[j], mla_g_kv[j], mla_w_kv_up[j])
            mixed = _attention(q, kc, vt)
            w_o = mla_w_out
        x, act = _mix_out_ffn_in(mixed, w_o, j, x, mod_all, i, norm_ffn[i], ffn_w_in)
        last = i == depth - 1
        x = _out_res(act, ffn_w_out, i, x, mod_all, i, 5, final_gain=final_norm if last else None)
    return x
```

```python
import functools
import math

import jax
import jax.numpy as jnp
from jax import lax
from jax.experimental import pallas as pl
from jax.experimental.pallas import tpu as pltpu

F32 = jnp.float32
BF16 = jnp.bfloat16

D_MODEL = 1024
EPS = 1e-6

GLA_HEADS = 4
GLA_DK = 128
GLA_DV = 256
GLA_GATE_RANK = 16
GLA_TAU = 16.0
GLA_CHUNK = 64
GLA_SUB = 16

MLA_HEADS = 8
MLA_NOPE = 128
MLA_ROPE = 64
MLA_VDIM = 128
MLA_Q_RANK = 384
MLA_KV_RANK = 256
ROPE_THETA = 10000.0
MLA_QK_PAD = 256
MLA_VT_ROWS = MLA_VDIM + 16

D_FF = 2816
N_MOD = 6

LANE = 128
SUBLANE = 8
ROW_TILE = 512
GLA_TIME_BLOCK = 512
ATT_Q_BLOCK = 1024
ATT_KV_BLOCK = 512
ATT_STRIP = 256
FFN_COL_CHUNK = 256
VMEM_LIMIT = 56 * 1024 * 1024


def _cparams(sem):
    return pltpu.CompilerParams(dimension_semantics=sem, vmem_limit_bytes=VMEM_LIMIT)


def _dot(a, b):
    return jnp.dot(a, b, preferred_element_type=F32)


def _dot_nt(a, b):
    return lax.dot_general(a, b, (((1,), (1,)), ((), ())), preferred_element_type=F32)


def _sigmoid(x):
    return 1.0 / (1.0 + jnp.exp(-x))


def _rms(x, g):
    return x * lax.rsqrt(jnp.mean(x * x, axis=-1, keepdims=True) + EPS) * g


def _norm_mod(x, g, shift, scale):
    return _rms(x, g) * (1.0 + scale) + shift


def _adaln_kernel(c_ref, w_ref, b_ref, o_ref):
    c = c_ref[...]
    o_ref[...] = _dot(c * _sigmoid(c), w_ref[...]) + b_ref[...]


def _adaln(c_pad, ada_w, ada_b):
    depth, d, n = ada_w.shape
    tn = 1536
    return pl.pallas_call(
        _adaln_kernel,
        grid=(depth, n // tn),
        in_specs=[
            pl.BlockSpec((SUBLANE, d), lambda l, j: (0, 0)),
            pl.BlockSpec((None, d, tn), lambda l, j: (l, 0, j)),
            pl.BlockSpec((None, 1, tn), lambda l, j: (l, 0, j)),
        ],
        out_specs=pl.BlockSpec((None, SUBLANE, tn), lambda l, j: (l, 0, j)),
        out_shape=jax.ShapeDtypeStruct((depth, SUBLANE, n), F32),
        compiler_params=_cparams(("arbitrary", "arbitrary")),
        name="adaln",
    )(c_pad, ada_w, ada_b.reshape(depth, 1, n))


def _row_spec(width):
    return pl.BlockSpec((None, ROW_TILE, width), lambda b, i: (b, i, 0))


def _mod_spec(layer, k):
    return pl.BlockSpec((None, SUBLANE, D_MODEL), lambda b, i: (layer, 0, k))


def _mod_row(ref):
    return ref[pl.ds(pl.program_id(0), 1), :]


def _const_spec(shape):
    return pl.BlockSpec(shape, lambda b, i: (0,) * len(shape))


def _layer_spec(shape, layer, col_block=0):
    return pl.BlockSpec((None,) + shape, lambda b, i: (layer, 0, col_block),
                        pipeline_mode=pl.Buffered(1))


def _gla_in_kernel(x_ref, g_ref, sh_ref, sc_ref, w_ref, wgate_ref, bgate_ref,
                   q_ref, k_ref, v_ref, r_ref, la_ref):
    hk = GLA_HEADS * GLA_DK
    hv = GLA_HEADS * GLA_DV
    h = _norm_mod(x_ref[...], g_ref[...], _mod_row(sh_ref), _mod_row(sc_ref)).astype(BF16)

    def proj(lo, width):
        return _dot(h, w_ref[:, lo:lo + width].astype(BF16))

    q_ref[...] = (proj(0, hk) * (GLA_DK ** -0.5)).astype(BF16)
    k_ref[...] = proj(hk, hk).astype(BF16)
    v_ref[...] = proj(2 * hk, hv).astype(BF16)
    r_ref[...] = proj(2 * hk + hv, hv).astype(BF16)
    low_rank = proj(2 * hk + 2 * hv, GLA_GATE_RANK).astype(BF16)
    z = _dot(low_rank, wgate_ref[...].astype(BF16)) + bgate_ref[...]
    log_sig = jnp.minimum(z, 0.0) - jnp.log(1.0 + jnp.exp(-jnp.abs(z)))
    la_ref[...] = log_sig / GLA_TAU


def _gla_in(x, mod_all, layer, gain, w_in, j, w_gate, b_gate):
    b, s, d = x.shape
    hk = GLA_HEADS * GLA_DK
    hv = GLA_HEADS * GLA_DV
    return pl.pallas_call(
        _gla_in_kernel,
        grid=(b, s // ROW_TILE),
        in_specs=[
            _row_spec(d), _const_spec((1, d)), _mod_spec(layer, 0), _mod_spec(layer, 1),
            _layer_spec((d, w_in.shape[-1]), j), _layer_spec((GLA_GATE_RANK, hk), j),
            _const_spec((1, hk)),
        ],
        out_specs=[_row_spec(hk), _row_spec(hk), _row_spec(hv), _row_spec(hv), _row_spec(hk)],
        out_shape=[
            jax.ShapeDtypeStruct((b, s, hk), BF16), jax.ShapeDtypeStruct((b, s, hk), BF16),
            jax.ShapeDtypeStruct((b, s, hv), BF16), jax.ShapeDtypeStruct((b, s, hv), BF16),
            jax.ShapeDtypeStruct((b, s, hk), F32),
        ],
        compiler_params=_cparams(("arbitrary", "arbitrary")),
        name="gla_in",
    )(x, gain.reshape(1, d), mod_all, mod_all, w_in, w_gate, b_gate.reshape(1, hk))


def _gla_kernel(q_ref, k_ref, v_ref, r_ref, la_ref, go_ref, o_ref, state_ref):
    c, sub = GLA_CHUNK, GLA_SUB
    nsub = c // sub
    assert nsub * c == 2 * LANE

    @pl.when(pl.program_id(2) == 0)
    def _():
        state_ref[...] = jnp.zeros_like(state_ref)

    row = lax.broadcasted_iota(jnp.int32, (c, c), 0)
    col = lax.broadcasted_iota(jnp.int32, (c, c), 1)
    tri = jnp.where(row >= col, 1.0, 0.0).astype(BF16)
    qrow = lax.broadcasted_iota(jnp.int32, (c, LANE), 0)
    lane = lax.broadcasted_iota(jnp.int32, (c, LANE), 1)
    causal = (lane % c) <= qrow
    keep_lo = causal & ((lane // c) == (qrow // sub))
    keep_hi = causal & ((lane // c) + LANE // c == (qrow // sub))

    n_chunks = GLA_TIME_BLOCK // c
    cs = [slice(ci * c, (ci + 1) * c) for ci in range(n_chunks)]

    bs = []
    for sl in cs:
        la = la_ref[sl, :]
        la_hi = la.astype(BF16)
        la_lo = (la - la_hi.astype(F32)).astype(BF16)
        bs.append(_dot(tri, la_hi) + _dot(tri, la_lo))

    qes, attns, kd_ts, decays = [], [], [], []
    for sl, b in zip(cs, bs):
        q = q_ref[sl, :].astype(F32)
        k = k_ref[sl, :].astype(F32)
        qes.append((q * jnp.exp(b)).astype(BF16))
        q_parts, k_parts = [], []
        for i in range(nsub):
            rows = slice(i * sub, (i + 1) * sub)
            live = (i + 1) * sub
            if i == 0:
                q_parts.append(q[rows] * jnp.exp(b[rows]))
                k_var = k[:live] * jnp.exp(-b[:live])
            else:
                anchor = b[i * sub - 1:i * sub, :]
                q_parts.append(q[rows] * jnp.exp(b[rows] - anchor))
                k_var = k[:live] * jnp.exp(anchor - b[:live])
            k_parts.append(k_var)
            if live < c:
                k_parts.append(jnp.zeros((c - live, GLA_DK), F32))
        q_anch = jnp.concatenate(q_parts, axis=0).astype(BF16)
        k_stack = jnp.concatenate(k_parts, axis=0).astype(BF16)
        prod = _dot_nt(q_anch, k_stack)
        attns.append(jnp.where(keep_lo, prod[:, :LANE],
                               jnp.where(keep_hi, prod[:, LANE:], 0.0)).astype(BF16))
        b_last = b[c - 1:c, :]
        kd_ts.append((k * jnp.exp(b_last - b)).T.astype(BF16))
        decays.append(jnp.broadcast_to(jnp.exp(b_last), (LANE, GLA_DK)).T)

    state = state_ref[...]
    gain = go_ref[...]
    for ci, sl in enumerate(cs):
        v = v_ref[sl, :]
        v2 = jnp.concatenate([v, v], axis=0)
        o = _dot(qes[ci], state.astype(BF16)) + _dot(attns[ci], v2)
        decay = jnp.concatenate([decays[ci]] * (GLA_DV // LANE), axis=1)
        state = state * decay + _dot(kd_ts[ci], v)
        r = r_ref[sl, :].astype(F32)
        o_ref[sl, :] = (_rms(o, gain) * (r * _sigmoid(r))).astype(BF16)
    state_ref[...] = state


def _gla(q, k, v, r, la, g_out):
    b, s, _ = q.shape
    t = GLA_TIME_BLOCK

    def spec(width):
        return pl.BlockSpec((None, t, width), lambda bi, h, ti: (bi, ti, h))

    return pl.pallas_call(
        _gla_kernel,
        grid=(b, GLA_HEADS, s // t),
        in_specs=[spec(GLA_DK), spec(GLA_DK), spec(GLA_DV), spec(GLA_DV), spec(GLA_DK),
                  pl.BlockSpec((1, GLA_DV), lambda bi, h, ti: (0, 0))],
        out_specs=spec(GLA_DV),
        out_shape=jax.ShapeDtypeStruct((b, s, GLA_HEADS * GLA_DV), BF16),
        scratch_shapes=[pltpu.VMEM((GLA_DK, GLA_DV), F32)],
        compiler_params=_cparams(("arbitrary", "arbitrary", "arbitrary")),
        name="gla_scan",
    )(q, k, v, r, la, g_out.reshape(1, GLA_DV))


def _out_res_kernel(a_ref, w_ref, x_ref, gate_ref, *rest, final_norm):
    o_ref = rest[-1]
    y = x_ref[...] + _mod_row(gate_ref) * _dot(a_ref[...], w_ref[...].astype(BF16))
    if final_norm:
        y = _rms(y, rest[0][...])
    o_ref[...] = y


def _out_res(a, w, j, x, mod_all, layer, gate_idx, final_gain=None):
    b, s, d = x.shape
    kdim = a.shape[-1]
    in_specs = [_row_spec(kdim), _layer_spec((kdim, d), j), _row_spec(d), _mod_spec(layer, gate_idx)]
    args = [a, w, x, mod_all]
    if final_gain is not None:
        in_specs.append(_const_spec((1, d)))
        args.append(final_gain.reshape(1, d))
    return pl.pallas_call(
        functools.partial(_out_res_kernel, final_norm=final_gain is not None),
        grid=(b, s // ROW_TILE),
        in_specs=in_specs,
        out_specs=_row_spec(d),
        out_shape=jax.ShapeDtypeStruct((b, s, d), F32),
        compiler_params=_cparams(("arbitrary", "arbitrary")),
        name="out_res",
    )(*args)


def _mix_out_ffn_in_kernel(m_ref, wo_ref, x_ref, g1_ref, g_ref, sh_ref, sc_ref, wg_ref, wu_ref,
                           x1_ref, a_ref):
    x1 = x_ref[...] + _mod_row(g1_ref) * _dot(m_ref[...], wo_ref[...].astype(BF16))
    x1_ref[...] = x1
    h = _norm_mod(x1, g_ref[...], _mod_row(sh_ref), _mod_row(sc_ref)).astype(BF16)
    for j in range(0, D_FF, FFN_COL_CHUNK):
        cols = slice(j, j + FFN_COL_CHUNK)
        gate = _dot(h, wg_ref[:, cols].astype(BF16))
        up = _dot(h, wu_ref[:, cols].astype(BF16))
        a_ref[:, cols] = (gate * _sigmoid(gate) * up).astype(BF16)


def _mix_out_ffn_in(mixed, w_o, j, x, mod_all, layer, gain, w_in):
    b, s, d = x.shape
    kdim = mixed.shape[-1]
    return pl.pallas_call(
        _mix_out_ffn_in_kernel,
        grid=(b, s // ROW_TILE),
        in_specs=[_row_spec(kdim), _layer_spec((kdim, d), j), _row_spec(d), _mod_spec(layer, 2),
                  _const_spec((1, d)), _mod_spec(layer, 3), _mod_spec(layer, 4),
                  _layer_spec((d, D_FF), layer, 0), _layer_spec((d, D_FF), layer, 1)],
        out_specs=[_row_spec(d), _row_spec(D_FF)],
        out_shape=[jax.ShapeDtypeStruct((b, s, d), F32), jax.ShapeDtypeStruct((b, s, D_FF), BF16)],
        compiler_params=_cparams(("arbitrary", "arbitrary")),
        name="mix_out_ffn_in",
    )(mixed, w_o, x, mod_all, gain.reshape(1, d), mod_all, mod_all, w_in, w_in)


def _mla_in_kernel(x_ref, g_ref, sh_ref, sc_ref, pos_ref, freq_ref, phase_ref, win_ref, gq_ref,
                   wq_ref, gkv_ref, wkn_ref, wvt_ref, q_ref, kc_ref, vt_ref, *, q_scale):
    h = _norm_mod(x_ref[...], g_ref[...], _mod_row(sh_ref), _mod_row(sc_ref)).astype(BF16)
    proj = _dot(h, win_ref[...])
    cq = _rms(proj[:, :MLA_Q_RANK], gq_ref[...]).astype(BF16)
    ckv = _rms(proj[:, MLA_Q_RANK:MLA_Q_RANK + MLA_KV_RANK], gkv_ref[...]).astype(BF16)

    pos = jnp.broadcast_to(pos_ref[...].astype(F32), (LANE, ROW_TILE)).T
    table = jnp.cos(pos * freq_ref[...] + phase_ref[...])
    rope_lanes = lax.broadcasted_iota(jnp.int32, (ROW_TILE, LANE), 1) < MLA_ROPE

    def rope(t):
        u = t * table
        return jnp.where(rope_lanes, u + pltpu.roll(u, MLA_ROPE, axis=1), 0.0)

    k_rope = rope(proj[:, MLA_Q_RANK + MLA_KV_RANK:]).astype(BF16)
    v_t = _dot_nt(wvt_ref[...], ckv)
    ones = jnp.ones((MLA_VT_ROWS - MLA_VDIM, v_t.shape[1]), BF16)
    k_nope = _dot(ckv, wkn_ref[...])
    q_all = _dot(cq, wq_ref[...])
    for hd in range(MLA_HEADS):
        lo = hd * MLA_QK_PAD
        mid = lo + MLA_NOPE
        hi = lo + MLA_QK_PAD
        q_ref[:, lo:mid] = (q_all[:, lo:mid] * q_scale).astype(BF16)
        q_ref[:, mid:hi] = (rope(q_all[:, mid:hi]) * q_scale).astype(BF16)
        kc_ref[:, lo:mid] = k_nope[:, hd * MLA_NOPE:(hd + 1) * MLA_NOPE].astype(BF16)
        kc_ref[:, mid:hi] = k_rope
        vt_ref[hd, :MLA_VDIM, :] = v_t[hd * MLA_VDIM:(hd + 1) * MLA_VDIM, :].astype(BF16)
        vt_ref[hd, MLA_VDIM:, :] = ones


def _half_split(w_rope):
    even, odd = w_rope[..., 0::2], w_rope[..., 1::2]
    return jnp.concatenate([even, odd, odd, even], axis=-1)


def _mla_in(x, mod_all, layer, gain, positions, w_in, g_q, w_q_up, g_kv, w_kv_up):
    b, s, d = x.shape
    hq = MLA_HEADS * MLA_QK_PAD
    hn = MLA_HEADS * MLA_NOPE
    hv = MLA_HEADS * MLA_VDIM
    lat = MLA_Q_RANK + MLA_KV_RANK
    half = MLA_ROPE // 2
    win = jnp.concatenate([w_in[:, :lat], _half_split(w_in[:, lat:])], axis=-1).astype(BF16)
    wq3 = w_q_up.reshape(MLA_Q_RANK, MLA_HEADS, MLA_NOPE + MLA_ROPE)
    wq = jnp.concatenate([wq3[..., :MLA_NOPE], _half_split(wq3[..., MLA_NOPE:])], axis=-1)
    wq = wq.reshape(MLA_Q_RANK, hq).astype(BF16)
    wkv3 = w_kv_up.reshape(MLA_KV_RANK, MLA_HEADS, MLA_NOPE + MLA_VDIM)
    wkn = wkv3[..., :MLA_NOPE].reshape(MLA_KV_RANK, hn).astype(BF16)
    wvt = wkv3[..., MLA_NOPE:].reshape(MLA_KV_RANK, hv).T.astype(BF16)
    win_w = win.shape[-1]
    inv_freq = ROPE_THETA ** (-jnp.arange(0, MLA_ROPE, 2, dtype=F32) / MLA_ROPE)
    freq_row = jnp.tile(inv_freq, LANE // half).reshape(1, LANE)
    quarter = jnp.full((half,), math.pi / 2, F32)
    phase_row = jnp.concatenate([jnp.zeros((2 * half,), F32), quarter, -quarter]).reshape(1, LANE)
    q_scale = (MLA_NOPE + MLA_ROPE) ** -0.5 * math.log2(math.e)
    return pl.pallas_call(
        functools.partial(_mla_in_kernel, q_scale=q_scale),
        grid=(b, s // ROW_TILE),
        in_specs=[
            _row_spec(d), _const_spec((1, d)), _mod_spec(layer, 0), _mod_spec(layer, 1),
            pl.BlockSpec((None, 1, ROW_TILE), lambda bi, i: (bi, 0, i)),
            _const_spec((1, LANE)), _const_spec((1, LANE)),
            _const_spec((d, win_w)), _const_spec((1, MLA_Q_RANK)), _const_spec((MLA_Q_RANK, hq)),
            _const_spec((1, MLA_KV_RANK)), _const_spec((MLA_KV_RANK, hn)),
            _const_spec((hv, MLA_KV_RANK)),
        ],
        out_specs=[_row_spec(hq), _row_spec(hq),
                   pl.BlockSpec((None, MLA_HEADS, MLA_VT_ROWS, ROW_TILE), lambda bi, i: (bi, 0, 0, i))],
        out_shape=[jax.ShapeDtypeStruct((b, s, hq), BF16), jax.ShapeDtypeStruct((b, s, hq), BF16),
                   jax.ShapeDtypeStruct((b, MLA_HEADS, MLA_VT_ROWS, s), BF16)],
        compiler_params=_cparams(("arbitrary", "arbitrary")),
        name="mla_in",
    )(x, gain.reshape(1, d), mod_all, mod_all, positions.reshape(b, 1, s), freq_row, phase_row,
      win, g_q.reshape(1, -1), wq, g_kv.reshape(1, -1), wkn, wvt)


def _attn_kernel(q_ref, k_ref, vt_ref, o_ref, s0_ref, s1_ref, mb0_ref, mb1_ref, m_ref, acc_ref):
    tq, tk = ATT_Q_BLOCK, ATT_KV_BLOCK
    per_q = tq // tk
    assert per_q == 2
    n_q = q_ref.shape[0] // tq
    strips = [slice(c0, c0 + ATT_STRIP) for c0 in range(0, tq, ATT_STRIP)]

    def scores(qi, kj, s_ref, mb_ref, diag_offset=None, strips=strips):
        start = pl.multiple_of(kj * tk, tk)
        k_blk = k_ref[pl.ds(start, tk), :]
        for cols in strips:
            q_strip = q_ref[pl.ds(pl.multiple_of(qi * tq + cols.start, ATT_STRIP), ATT_STRIP), :]
            s_t = _dot_nt(k_blk, q_strip)
            if diag_offset is not None:
                key = lax.broadcasted_iota(jnp.int32, (tk, ATT_STRIP), 0) + diag_offset
                qry = lax.broadcasted_iota(jnp.int32, (tk, ATT_STRIP), 1) + cols.start
                s_t = jnp.where(key <= qry, s_t, -jnp.inf)
            s_ref[:, cols] = s_t
            mb_ref[:, cols] = jnp.max(s_t, axis=0, keepdims=True)

    def accumulate(kj, s_ref, mb_ref, strips=strips):
        start = pl.multiple_of(kj * tk, tk)
        vt_blk = vt_ref[:, pl.ds(start, tk)]
        for cols in strips:
            m_prev = m_ref[:, cols]
            m_new = jnp.maximum(m_prev, mb_ref[:, cols])
            p_t = jnp.exp2((s_ref[:, cols] - m_new).astype(BF16))
            alpha = jnp.exp2(m_prev - m_new)
            acc_ref[:, cols] = alpha * acc_ref[:, cols] + _dot(vt_blk, p_t)
            m_ref[:, cols] = m_new

    def reset():
        m_ref[...] = jnp.full_like(m_ref, -jnp.inf)
        acc_ref[...] = jnp.zeros_like(acc_ref)

    def finish(qi, pending):
        accumulate(pending, s1_ref, mb1_ref)
        acc = acc_ref[...]
        o_t = acc[:MLA_VDIM, :] / acc[MLA_VDIM:MLA_VDIM + 1, :]
        o_ref[pl.ds(pl.multiple_of(qi * tq, tq), tq), :] = o_t.T.astype(BF16)

    def run_block(qi):
        first = per_q * qi
        reset()
        scores(qi, first + 1, s1_ref, mb1_ref, diag_offset=tk)
        accumulate(first, s0_ref, mb0_ref)

        def pair(p, pending):
            for cols in strips:
                accumulate(pending, s1_ref, mb1_ref, strips=[cols])
                scores(qi, 2 * p, s0_ref, mb0_ref, strips=[cols])
            for cols in strips:
                accumulate(2 * p, s0_ref, mb0_ref, strips=[cols])
                scores(qi, 2 * p + 1, s1_ref, mb1_ref, strips=[cols])
            return 2 * p + 1

        return lax.fori_loop(0, qi, pair, first + 1)

    scores(0, 0, s0_ref, mb0_ref, diag_offset=0)
    pending0 = run_block(0)

    def q_block(qi, pending):
        scores(qi, per_q * qi, s0_ref, mb0_ref, diag_offset=0)
        finish(qi - 1, pending)
        return run_block(qi)

    pending_last = lax.fori_loop(1, n_q, q_block, pending0)
    finish(n_q - 1, pending_last)


def _attention(q, kc, vt):
    b, _, _, s = vt.shape
    tq, tk = ATT_Q_BLOCK, ATT_KV_BLOCK
    return pl.pallas_call(
        _attn_kernel,
        grid=(b, MLA_HEADS),
        in_specs=[
            pl.BlockSpec((None, s, MLA_QK_PAD), lambda bi, h: (bi, 0, h)),
            pl.BlockSpec((None, s, MLA_QK_PAD), lambda bi, h: (bi, 0, h)),
            pl.BlockSpec((None, None, MLA_VT_ROWS, s), lambda bi, h: (bi, h, 0, 0)),
        ],
        out_specs=pl.BlockSpec((None, s, MLA_VDIM), lambda bi, h: (bi, 0, h)),
        out_shape=jax.ShapeDtypeStruct((b, s, MLA_HEADS * MLA_VDIM), BF16),
        scratch_shapes=[pltpu.VMEM((tk, tq), F32), pltpu.VMEM((tk, tq), F32),
                        pltpu.VMEM((1, tq), F32), pltpu.VMEM((1, tq), F32),
                        pltpu.VMEM((1, tq), F32), pltpu.VMEM((MLA_VT_ROWS, tq), F32)],
        compiler_params=_cparams(("arbitrary", "arbitrary")),
        name="mla_attention",
    )(q, kc, vt)


def kernel(x, c, positions, ada_w, ada_b, norm_mix, norm_ffn, gla_w_in, gla_w_gate, gla_b_gate,
           gla_g_out, gla_w_out, mla_w_in, mla_g_q, mla_w_q_up, mla_g_kv, mla_w_kv_up, mla_w_out,
           ffn_w_in, ffn_w_out, final_norm):
    batch = x.shape[0]
    depth = ada_w.shape[0]
    assert batch <= SUBLANE and ada_w.shape[-1] == N_MOD * D_MODEL
    c_pad = jnp.pad(c, ((0, SUBLANE - batch), (0, 0)))
    mod_all = _adaln(c_pad, ada_w, ada_b)

    for i in range(depth):
        j = i // 2
        if i % 2 == 0:
            q, k, v, r, la = _gla_in(x, mod_all, i, norm_mix[i], gla_w_in, j, gla_w_gate,
                                     gla_b_gate[j])
            mixed = _gla(q, k, v, r, la, gla_g_out[j])
            w_o = gla_w_out
        else:
            q, kc, vt = _mla_in(x, mod_all, i, norm_mix[i], positions, mla_w_in[j], mla_g_q[j],
                                mla_w_q_up[j], mla_g_kv[j], mla_w_kv_up[j])
            mixed = _attention(q, kc, vt)
            w_o = mla_w_out
        x, act = _mix_out_ffn_in(mixed, w_o, j, x, mod_all, i, norm_ffn[i], ffn_w_in)
        last = i == depth - 1
        x = _out_res(act, ffn_w_out, i, x, mod_all, i, 5, final_gain=final_norm if last else None)
    return x
```

```python
import functools
import math

import jax
import jax.numpy as jnp
from jax import lax
from jax.experimental import pallas as pl
from jax.experimental.pallas import tpu as pltpu

F32 = jnp.float32
BF16 = jnp.bfloat16

D_MODEL = 1024
EPS = 1e-6

GLA_HEADS = 4
GLA_DK = 128
GLA_DV = 256
GLA_GATE_RANK = 16
GLA_TAU = 16.0
GLA_CHUNK = 64
GLA_SUB = 16

MLA_HEADS = 8
MLA_NOPE = 128
MLA_ROPE = 64
MLA_VDIM = 128
MLA_Q_RANK = 384
MLA_KV_RANK = 256
ROPE_THETA = 10000.0
MLA_QK_PAD = 256
MLA_VT_ROWS = MLA_VDIM + 16

D_FF = 2816
N_MOD = 6

LANE = 128
SUBLANE = 8
ROW_TILE = 512
GLA_TIME_BLOCK = 512
ATT_Q_BLOCK = 1024
ATT_KV_BLOCK = 512
ATT_STRIP = 256
FFN_COL_CHUNK = 256
VMEM_LIMIT = 56 * 1024 * 1024


def _cparams(sem):
    return pltpu.CompilerParams(dimension_semantics=sem, vmem_limit_bytes=VMEM_LIMIT)


def _dot(a, b):
    return jnp.dot(a, b, preferred_element_type=F32)


def _dot_nt(a, b):
    return lax.dot_general(a, b, (((1,), (1,)), ((), ())), preferred_element_type=F32)


def _sigmoid(x):
    return 1.0 / (1.0 + jnp.exp(-x))


def _rms(x, g):
    return x * lax.rsqrt(jnp.mean(x * x, axis=-1, keepdims=True) + EPS) * g


def _norm_mod(x, g, shift, scale):
    return _rms(x, g * (1.0 + scale)) + shift


def _adaln_kernel(c_ref, w_ref, b_ref, o_ref):
    c = c_ref[...]
    o_ref[...] = _dot(c * _sigmoid(c), w_ref[...]) + b_ref[...]


def _adaln(c_pad, ada_w, ada_b):
    depth, d, n = ada_w.shape
    tn = 1536
    return pl.pallas_call(
        _adaln_kernel,
        grid=(depth, n // tn),
        in_specs=[
            pl.BlockSpec((SUBLANE, d), lambda l, j: (0, 0)),
            pl.BlockSpec((None, d, tn), lambda l, j: (l, 0, j)),
            pl.BlockSpec((None, 1, tn), lambda l, j: (l, 0, j)),
        ],
        out_specs=pl.BlockSpec((None, SUBLANE, tn), lambda l, j: (l, 0, j)),
        out_shape=jax.ShapeDtypeStruct((depth, SUBLANE, n), F32),
        compiler_params=_cparams(("arbitrary", "arbitrary")),
        name="adaln",
    )(c_pad, ada_w, ada_b.reshape(depth, 1, n))


def _row_spec(width):
    return pl.BlockSpec((None, ROW_TILE, width), lambda b, i: (b, i, 0))


def _mod_spec(layer, k):
    return pl.BlockSpec((None, SUBLANE, D_MODEL), lambda b, i: (layer, 0, k))


def _mod_row(ref):
    return ref[pl.ds(pl.program_id(0), 1), :]


def _row_halves():
    half = ROW_TILE // 2
    return (slice(0, half), slice(half, ROW_TILE))


def _const_spec(shape):
    return pl.BlockSpec(shape, lambda b, i: (0,) * len(shape))


def _layer_spec(shape, layer, col_block=0):
    return pl.BlockSpec((None,) + shape, lambda b, i: (layer, 0, col_block),
                        pipeline_mode=pl.Buffered(1))


def _gla_in_kernel(x_ref, g_ref, sh_ref, sc_ref, w_ref, wgate_ref, bgate_ref,
                   q_ref, k_ref, v_ref, r_ref, la_ref):
    hk = GLA_HEADS * GLA_DK
    hv = GLA_HEADS * GLA_DV
    shift, scale = _mod_row(sh_ref), _mod_row(sc_ref)
    for rows in _row_halves():
        h = _norm_mod(x_ref[rows, :], g_ref[...], shift, scale).astype(BF16)

        def proj(lo, width):
            return _dot(h, w_ref[:, lo:lo + width].astype(BF16))

        q_ref[rows, :] = (proj(0, hk) * (GLA_DK ** -0.5)).astype(BF16)
        k_ref[rows, :] = proj(hk, hk).astype(BF16)
        v_ref[rows, :] = proj(2 * hk, hv).astype(BF16)
        r_ref[rows, :] = proj(2 * hk + hv, hv).astype(BF16)
        low_rank = proj(2 * hk + 2 * hv, GLA_GATE_RANK).astype(BF16)
        z = _dot(low_rank, wgate_ref[...].astype(BF16)) + bgate_ref[...]
        log_sig = jnp.minimum(z, 0.0) - jnp.log(1.0 + jnp.exp(-jnp.abs(z)))
        la_ref[rows, :] = log_sig / GLA_TAU


def _gla_in(x, mod_all, layer, gain, w_in, j, w_gate, b_gate):
    b, s, d = x.shape
    hk = GLA_HEADS * GLA_DK
    hv = GLA_HEADS * GLA_DV
    return pl.pallas_call(
        _gla_in_kernel,
        grid=(b, s // ROW_TILE),
        in_specs=[
            _row_spec(d), _const_spec((1, d)), _mod_spec(layer, 0), _mod_spec(layer, 1),
            _layer_spec((d, w_in.shape[-1]), j), _layer_spec((GLA_GATE_RANK, hk), j),
            _const_spec((1, hk)),
        ],
        out_specs=[_row_spec(hk), _row_spec(hk), _row_spec(hv), _row_spec(hv), _row_spec(hk)],
        out_shape=[
            jax.ShapeDtypeStruct((b, s, hk), BF16), jax.ShapeDtypeStruct((b, s, hk), BF16),
            jax.ShapeDtypeStruct((b, s, hv), BF16), jax.ShapeDtypeStruct((b, s, hv), BF16),
            jax.ShapeDtypeStruct((b, s, hk), F32),
        ],
        compiler_params=_cparams(("arbitrary", "arbitrary")),
        name="gla_in",
    )(x, gain.reshape(1, d), mod_all, mod_all, w_in, w_gate, b_gate.reshape(1, hk))


def _gla_kernel(q_ref, k_ref, v_ref, r_ref, la_ref, go_ref, o_ref, state_ref):
    c, sub = GLA_CHUNK, GLA_SUB
    nsub = c // sub
    assert nsub * c == 2 * LANE

    @pl.when(pl.program_id(2) == 0)
    def _():
        state_ref[...] = jnp.zeros_like(state_ref)

    row = lax.broadcasted_iota(jnp.int32, (c, c), 0)
    col = lax.broadcasted_iota(jnp.int32, (c, c), 1)
    tri = jnp.where(row >= col, 1.0, 0.0).astype(BF16)
    qrow = lax.broadcasted_iota(jnp.int32, (c, LANE), 0)
    lane = lax.broadcasted_iota(jnp.int32, (c, LANE), 1)
    causal = (lane % c) <= qrow
    keep_lo = causal & ((lane // c) == (qrow // sub))
    keep_hi = causal & ((lane // c) + LANE // c == (qrow // sub))

    n_chunks = GLA_TIME_BLOCK // c
    cs = [slice(ci * c, (ci + 1) * c) for ci in range(n_chunks)]

    bs = []
    for sl in cs:
        la = la_ref[sl, :]
        la_hi = la.astype(BF16)
        la_lo = (la - la_hi.astype(F32)).astype(BF16)
        bs.append(_dot(tri, la_hi) + _dot(tri, la_lo))

    qes, attns, kd_ts, decays = [], [], [], []
    for sl, b in zip(cs, bs):
        q = q_ref[sl, :].astype(F32)
        k = k_ref[sl, :].astype(F32)
        qes.append((q * jnp.exp(b)).astype(BF16))
        q_parts, k_parts = [], []
        for i in range(nsub):
            rows = slice(i * sub, (i + 1) * sub)
            live = (i + 1) * sub
            if i == 0:
                q_parts.append(q[rows] * jnp.exp(b[rows]))
                k_var = k[:live] * jnp.exp(-b[:live])
            else:
                anchor = b[i * sub - 1:i * sub, :]
                q_parts.append(q[rows] * jnp.exp(b[rows] - anchor))
                k_var = k[:live] * jnp.exp(anchor - b[:live])
            k_parts.append(k_var)
            if live < c:
                k_parts.append(jnp.zeros((c - live, GLA_DK), F32))
        q_anch = jnp.concatenate(q_parts, axis=0).astype(BF16)
        k_stack = jnp.concatenate(k_parts, axis=0).astype(BF16)
        prod = _dot_nt(q_anch, k_stack)
        attns.append(jnp.where(keep_lo, prod[:, :LANE],
                               jnp.where(keep_hi, prod[:, LANE:], 0.0)).astype(BF16))
        b_last = b[c - 1:c, :]
        kd_ts.append((k * jnp.exp(b_last - b)).T.astype(BF16))
        decays.append(jnp.broadcast_to(jnp.exp(b_last), (LANE, GLA_DK)).T)

    state = state_ref[...]
    gain = go_ref[...]
    for ci, sl in enumerate(cs):
        v = v_ref[sl, :]
        v2 = jnp.concatenate([v, v], axis=0)
        o = _dot(qes[ci], state.astype(BF16)) + _dot(attns[ci], v2)
        decay = jnp.concatenate([decays[ci]] * (GLA_DV // LANE), axis=1)
        state = state * decay + _dot(kd_ts[ci], v)
        r = r_ref[sl, :].astype(F32)
        o_ref[sl, :] = (_rms(o, gain) * (r * _sigmoid(r))).astype(BF16)
    state_ref[...] = state


def _gla(q, k, v, r, la, g_out):
    b, s, _ = q.shape
    t = GLA_TIME_BLOCK

    def spec(width):
        return pl.BlockSpec((None, t, width), lambda bi, h, ti: (bi, ti, h))

    return pl.pallas_call(
        _gla_kernel,
        grid=(b, GLA_HEADS, s // t),
        in_specs=[spec(GLA_DK), spec(GLA_DK), spec(GLA_DV), spec(GLA_DV), spec(GLA_DK),
                  pl.BlockSpec((1, GLA_DV), lambda bi, h, ti: (0, 0))],
        out_specs=spec(GLA_DV),
        out_shape=jax.ShapeDtypeStruct((b, s, GLA_HEADS * GLA_DV), BF16),
        scratch_shapes=[pltpu.VMEM((GLA_DK, GLA_DV), F32)],
        compiler_params=_cparams(("arbitrary", "arbitrary", "arbitrary")),
        name="gla_scan",
    )(q, k, v, r, la, g_out.reshape(1, GLA_DV))


def _out_res_kernel(a_ref, w_ref, x_ref, gate_ref, *rest, final_norm):
    o_ref = rest[-1]
    y = x_ref[...] + _mod_row(gate_ref) * _dot(a_ref[...], w_ref[...].astype(BF16))
    if final_norm:
        y = _rms(y, rest[0][...])
    o_ref[...] = y


def _out_res(a, w, j, x, mod_all, layer, gate_idx, final_gain=None):
    b, s, d = x.shape
    kdim = a.shape[-1]
    in_specs = [_row_spec(kdim), _layer_spec((kdim, d), j), _row_spec(d), _mod_spec(layer, gate_idx)]
    args = [a, w, x, mod_all]
    if final_gain is not None:
        in_specs.append(_const_spec((1, d)))
        args.append(final_gain.reshape(1, d))
    return pl.pallas_call(
        functools.partial(_out_res_kernel, final_norm=final_gain is not None),
        grid=(b, s // ROW_TILE),
        in_specs=in_specs,
        out_specs=_row_spec(d),
        out_shape=jax.ShapeDtypeStruct((b, s, d), F32),
        compiler_params=_cparams(("arbitrary", "arbitrary")),
        name="out_res",
    )(*args)


def _mix_out_ffn_in_kernel(m_ref, wo_ref, x_ref, g1_ref, g_ref, sh_ref, sc_ref, wg_ref, wu_ref,
                           x1_ref, a_ref):
    gate1, shift, scale = _mod_row(g1_ref), _mod_row(sh_ref), _mod_row(sc_ref)
    w_o = wo_ref[...].astype(BF16)
    hs = []
    for rows in _row_halves():
        x1 = x_ref[rows, :] + gate1 * _dot(m_ref[rows, :], w_o)
        x1_ref[rows, :] = x1
        hs.append((rows, _norm_mod(x1, g_ref[...], shift, scale).astype(BF16)))
    for j in range(0, D_FF, FFN_COL_CHUNK):
        cols = slice(j, j + FFN_COL_CHUNK)
        w_gate = wg_ref[:, cols].astype(BF16)
        w_up = wu_ref[:, cols].astype(BF16)
        for rows, h in hs:
            gate = _dot(h, w_gate)
            up = _dot(h, w_up)
            a_ref[rows, cols] = (gate * _sigmoid(gate) * up).astype(BF16)


def _mix_out_ffn_in(mixed, w_o, j, x, mod_all, layer, gain, w_in):
    b, s, d = x.shape
    kdim = mixed.shape[-1]
    return pl.pallas_call(
        _mix_out_ffn_in_kernel,
        grid=(b, s // ROW_TILE),
        in_specs=[_row_spec(kdim), _layer_spec((kdim, d), j), _row_spec(d), _mod_spec(layer, 2),
                  _const_spec((1, d)), _mod_spec(layer, 3), _mod_spec(layer, 4),
                  _layer_spec((d, D_FF), layer, 0), _layer_spec((d, D_FF), layer, 1)],
        out_specs=[_row_spec(d), _row_spec(D_FF)],
        out_shape=[jax.ShapeDtypeStruct((b, s, d), F32), jax.ShapeDtypeStruct((b, s, D_FF), BF16)],
        compiler_params=_cparams(("arbitrary", "arbitrary")),
        name="mix_out_ffn_in",
    )(mixed, w_o, x, mod_all, gain.reshape(1, d), mod_all, mod_all, w_in, w_in)


def _mla_in_kernel(x_ref, g_ref, sh_ref, sc_ref, pos_ref, freq_ref, phase_ref, win_ref, gq_ref,
                   wq_ref, gkv_ref, wkn_ref, wvt_ref, q_ref, kc_ref, vt_ref, *, q_scale):
    h = _norm_mod(x_ref[...], g_ref[...], _mod_row(sh_ref), _mod_row(sc_ref)).astype(BF16)
    proj = _dot(h, win_ref[...])
    cq = _rms(proj[:, :MLA_Q_RANK], gq_ref[...]).astype(BF16)
    ckv = _rms(proj[:, MLA_Q_RANK:MLA_Q_RANK + MLA_KV_RANK], gkv_ref[...]).astype(BF16)

    pos = jnp.broadcast_to(pos_ref[...].astype(F32), (LANE, ROW_TILE)).T
    table = jnp.cos(pos * freq_ref[...] + phase_ref[...])
    table_q = table * q_scale
    rope_lanes = lax.broadcasted_iota(jnp.int32, (ROW_TILE, LANE), 1) < MLA_ROPE

    def rotate(t, tab):
        u = t * tab
        return u + pltpu.roll(u, MLA_ROPE, axis=1)

    k_rope = jnp.where(rope_lanes, rotate(proj[:, MLA_Q_RANK + MLA_KV_RANK:], table), 0.0)
    k_rope = k_rope.astype(BF16)
    v_t = _dot_nt(wvt_ref[...], ckv)
    ones = jnp.ones((MLA_VT_ROWS - MLA_VDIM, v_t.shape[1]), BF16)
    k_nope = _dot(ckv, wkn_ref[...])
    q_all = _dot(cq, wq_ref[...])
    for hd in range(MLA_HEADS):
        lo = hd * MLA_QK_PAD
        mid = lo + MLA_NOPE
        hi = lo + MLA_QK_PAD
        q_ref[:, lo:mid] = (q_all[:, lo:mid] * q_scale).astype(BF16)
        q_ref[:, mid:hi] = rotate(q_all[:, mid:hi], table_q).astype(BF16)
        kc_ref[:, lo:mid] = k_nope[:, hd * MLA_NOPE:(hd + 1) * MLA_NOPE].astype(BF16)
        kc_ref[:, mid:hi] = k_rope
        vt_ref[hd, :MLA_VDIM, :] = v_t[hd * MLA_VDIM:(hd + 1) * MLA_VDIM, :].astype(BF16)
        vt_ref[hd, MLA_VDIM:, :] = ones


def _half_split(w_rope):
    even, odd = w_rope[..., 0::2], w_rope[..., 1::2]
    return jnp.concatenate([even, odd, odd, even], axis=-1)


def _mla_in(x, mod_all, layer, gain, positions, w_in, g_q, w_q_up, g_kv, w_kv_up):
    b, s, d = x.shape
    hq = MLA_HEADS * MLA_QK_PAD
    hn = MLA_HEADS * MLA_NOPE
    hv = MLA_HEADS * MLA_VDIM
    lat = MLA_Q_RANK + MLA_KV_RANK
    half = MLA_ROPE // 2
    win = jnp.concatenate([w_in[:, :lat], _half_split(w_in[:, lat:])], axis=-1).astype(BF16)
    wq3 = w_q_up.reshape(MLA_Q_RANK, MLA_HEADS, MLA_NOPE + MLA_ROPE)
    wq = jnp.concatenate([wq3[..., :MLA_NOPE], _half_split(wq3[..., MLA_NOPE:])], axis=-1)
    wq = wq.reshape(MLA_Q_RANK, hq).astype(BF16)
    wkv3 = w_kv_up.reshape(MLA_KV_RANK, MLA_HEADS, MLA_NOPE + MLA_VDIM)
    wkn = wkv3[..., :MLA_NOPE].reshape(MLA_KV_RANK, hn).astype(BF16)
    wvt = wkv3[..., MLA_NOPE:].reshape(MLA_KV_RANK, hv).T.astype(BF16)
    win_w = win.shape[-1]
    inv_freq = ROPE_THETA ** (-jnp.arange(0, MLA_ROPE, 2, dtype=F32) / MLA_ROPE)
    freq_row = jnp.tile(inv_freq, LANE // half).reshape(1, LANE)
    quarter = jnp.full((half,), math.pi / 2, F32)
    phase_row = jnp.concatenate([jnp.zeros((2 * half,), F32), quarter, -quarter]).reshape(1, LANE)
    q_scale = (MLA_NOPE + MLA_ROPE) ** -0.5 * math.log2(math.e)
    return pl.pallas_call(
        functools.partial(_mla_in_kernel, q_scale=q_scale),
        grid=(b, s // ROW_TILE),
        in_specs=[
            _row_spec(d), _const_spec((1, d)), _mod_spec(layer, 0), _mod_spec(layer, 1),
            pl.BlockSpec((None, 1, ROW_TILE), lambda bi, i: (bi, 0, i)),
            _const_spec((1, LANE)), _const_spec((1, LANE)),
            _const_spec((d, win_w)), _const_spec((1, MLA_Q_RANK)), _const_spec((MLA_Q_RANK, hq)),
            _const_spec((1, MLA_KV_RANK)), _const_spec((MLA_KV_RANK, hn)),
            _const_spec((hv, MLA_KV_RANK)),
        ],
        out_specs=[_row_spec(hq), _row_spec(hq),
                   pl.BlockSpec((None, MLA_HEADS, MLA_VT_ROWS, ROW_TILE), lambda bi, i: (bi, 0, 0, i))],
        out_shape=[jax.ShapeDtypeStruct((b, s, hq), BF16), jax.ShapeDtypeStruct((b, s, hq), BF16),
                   jax.ShapeDtypeStruct((b, MLA_HEADS, MLA_VT_ROWS, s), BF16)],
        compiler_params=_cparams(("arbitrary", "arbitrary")),
        name="mla_in",
    )(x, gain.reshape(1, d), mod_all, mod_all, positions.reshape(b, 1, s), freq_row, phase_row,
      win, g_q.reshape(1, -1), wq, g_kv.reshape(1, -1), wkn, wvt)


def _attn_kernel(q_ref, k_ref, vt_ref, o_ref, s0_ref, s1_ref, mb0_ref, mb1_ref, m_ref, acc_ref):
    tq, tk = ATT_Q_BLOCK, ATT_KV_BLOCK
    per_q = tq // tk
    assert per_q == 2
    n_q = q_ref.shape[0] // tq
    strips = [slice(c0, c0 + ATT_STRIP) for c0 in range(0, tq, ATT_STRIP)]

    def scores(qi, kj, s_ref, mb_ref, diag_offset=None, strips=strips):
        start = pl.multiple_of(kj * tk, tk)
        k_blk = k_ref[pl.ds(start, tk), :]
        for cols in strips:
            q_strip = q_ref[pl.ds(pl.multiple_of(qi * tq + cols.start, ATT_STRIP), ATT_STRIP), :]
            s_t = _dot_nt(k_blk, q_strip)
            if diag_offset is not None:
                key = lax.broadcasted_iota(jnp.int32, (tk, ATT_STRIP), 0) + diag_offset
                qry = lax.broadcasted_iota(jnp.int32, (tk, ATT_STRIP), 1) + cols.start
                s_t = jnp.where(key <= qry, s_t, -jnp.inf)
            s_ref[:, cols] = s_t
            mb_ref[:, cols] = jnp.max(s_t, axis=0, keepdims=True)

    def accumulate(kj, s_ref, mb_ref, strips=strips):
        start = pl.multiple_of(kj * tk, tk)
        vt_blk = vt_ref[:, pl.ds(start, tk)]
        for cols in strips:
            m_prev = m_ref[:, cols]
            m_new = jnp.maximum(m_prev, mb_ref[:, cols])
            p_t = jnp.exp2((s_ref[:, cols] - m_new).astype(BF16))
            alpha = jnp.exp2(m_prev - m_new)
            acc_ref[:, cols] = alpha * acc_ref[:, cols] + _dot(vt_blk, p_t)
            m_ref[:, cols] = m_new

    def reset():
        m_ref[...] = jnp.full_like(m_ref, -jnp.inf)
        acc_ref[...] = jnp.zeros_like(acc_ref)

    def finish(qi, pending):
        accumulate(pending, s1_ref, mb1_ref)
        acc = acc_ref[...]
        o_t = acc[:MLA_VDIM, :] / acc[MLA_VDIM:MLA_VDIM + 1, :]
        o_ref[pl.ds(pl.multiple_of(qi * tq, tq), tq), :] = o_t.T.astype(BF16)

    def run_block(qi):
        first = per_q * qi
        reset()
        scores(qi, first + 1, s1_ref, mb1_ref, diag_offset=tk)
        accumulate(first, s0_ref, mb0_ref)

        def pair(p, pending):
            for cols in strips:
                accumulate(pending, s1_ref, mb1_ref, strips=[cols])
                scores(qi, 2 * p, s0_ref, mb0_ref, strips=[cols])
            for cols in strips:
                accumulate(2 * p, s0_ref, mb0_ref, strips=[cols])
                scores(qi, 2 * p + 1, s1_ref, mb1_ref, strips=[cols])
            return 2 * p + 1

        return lax.fori_loop(0, qi, pair, first + 1)

    scores(0, 0, s0_ref, mb0_ref, diag_offset=0)
    pending0 = run_block(0)

    def q_block(qi, pending):
        scores(qi, per_q * qi, s0_ref, mb0_ref, diag_offset=0)
        finish(qi - 1, pending)
        return run_block(qi)

    pending_last = lax.fori_loop(1, n_q, q_block, pending0)
    finish(n_q - 1, pending_last)


def _attention(q, kc, vt):
    b, _, _, s = vt.shape
    tq, tk = ATT_Q_BLOCK, ATT_KV_BLOCK
    return pl.pallas_call(
        _attn_kernel,
        grid=(b, MLA_HEADS),
        in_specs=[
            pl.BlockSpec((None, s, MLA_QK_PAD), lambda bi, h: (bi, 0, h)),
            pl.BlockSpec((None, s, MLA_QK_PAD), lambda bi, h: (bi, 0, h)),
            pl.BlockSpec((None, None, MLA_VT_ROWS, s), lambda bi, h: (bi, h, 0, 0)),
        ],
        out_specs=pl.BlockSpec((None, s, MLA_VDIM), lambda bi, h: (bi, 0, h)),
        out_shape=jax.ShapeDtypeStruct((b, s, MLA_HEADS * MLA_VDIM), BF16),
        scratch_shapes=[pltpu.VMEM((tk, tq), F32), pltpu.VMEM((tk, tq), F32),
                        pltpu.VMEM((1, tq), F32), pltpu.VMEM((1, tq), F32),
                        pltpu.VMEM((1, tq), F32), pltpu.VMEM((MLA_VT_ROWS, tq), F32)],
        compiler_params=_cparams(("arbitrary", "arbitrary")),
        name="mla_attention",
    )(q, kc, vt)


def kernel(x, c, positions, ada_w, ada_b, norm_mix, norm_ffn, gla_w_in, gla_w_gate, gla_b_gate,
           gla_g_out, gla_w_out, mla_w_in, mla_g_q, mla_w_q_up, mla_g_kv, mla_w_kv_up, mla_w_out,
           ffn_w_in, ffn_w_out, final_norm):
    batch = x.shape[0]
    depth = ada_w.shape[0]
    assert batch <= SUBLANE and ada_w.shape[-1] == N_MOD * D_MODEL
    c_pad = jnp.pad(c, ((0, SUBLANE - batch), (0, 0)))
    mod_all = _adaln(c_pad, ada_w, ada_b)

    for i in range(depth):
        j = i // 2
        if i % 2 == 0:
            q, k, v, r, la = _gla_in(x, mod_all, i, norm_mix[i], gla_w_in, j, gla_w_gate,
                                     gla_b_gate[j])
            mixed = _gla(q, k, v, r, la, gla_g_out[j])
            w_o = gla_w_out
        else:
            q, kc, vt = _mla_in(x, mod_all, i, norm_mix[i], positions, mla_w_in[j], mla_g_q[j],
                                mla_w_q_up[j], mla_g_kv[j], mla_w_kv_up[j])
            mixed = _attention(q, kc, vt)
            w_o = mla_w_out
        x, act = _mix_out_ffn_in(mixed, w_o, j, x, mod_all, i, norm_ffn[i], ffn_w_in)
        last = i == depth - 1
        x = _out_res(act, ffn_w_out, i, x, mod_all, i, 5, final_gain=final_norm if last else None)
    return x
```

```python
import functools
import math

import jax
import jax.numpy as jnp
from jax import lax
from jax.experimental import pallas as pl
from jax.experimental.pallas import tpu as pltpu

F32 = jnp.float32
BF16 = jnp.bfloat16

D_MODEL = 1024
EPS = 1e-6

GLA_HEADS = 4
GLA_DK = 128
GLA_DV = 256
GLA_GATE_RANK = 16
GLA_TAU = 16.0
GLA_CHUNK = 64
GLA_SUB = 16

MLA_HEADS = 8
MLA_NOPE = 128
MLA_ROPE = 64
MLA_VDIM = 128
MLA_Q_RANK = 384
MLA_KV_RANK = 256
ROPE_THETA = 10000.0
MLA_QK_PAD = 256
MLA_VT_ROWS = MLA_VDIM + 16

D_FF = 2816
N_MOD = 6

LANE = 128
SUBLANE = 8
ROW_TILE = 512
GLA_TIME_BLOCK = 512
ATT_Q_BLOCK = 1024
ATT_KV_BLOCK = 512
ATT_STRIP = 256
FFN_COL_CHUNK = 256
VMEM_LIMIT = 56 * 1024 * 1024


def _cparams(sem):
    return pltpu.CompilerParams(dimension_semantics=sem, vmem_limit_bytes=VMEM_LIMIT)


def _dot(a, b):
    return jnp.dot(a, b, preferred_element_type=F32)


def _dot_nt(a, b):
    return lax.dot_general(a, b, (((1,), (1,)), ((), ())), preferred_element_type=F32)


def _sigmoid(x):
    return 1.0 / (1.0 + jnp.exp(-x))


def _rms(x, g):
    return x * lax.rsqrt(jnp.mean(x * x, axis=-1, keepdims=True) + EPS) * g


def _norm_mod(x, g, shift, scale):
    return _rms(x, g) * (1.0 + scale) + shift


def _adaln_kernel(c_ref, w_ref, b_ref, o_ref):
    c = c_ref[...]
    o_ref[...] = _dot(c * _sigmoid(c), w_ref[...]) + b_ref[...]


def _adaln(c_pad, ada_w, ada_b):
    depth, d, n = ada_w.shape
    tn = 1536
    return pl.pallas_call(
        _adaln_kernel,
        grid=(depth, n // tn),
        in_specs=[
            pl.BlockSpec((SUBLANE, d), lambda l, j: (0, 0)),
            pl.BlockSpec((None, d, tn), lambda l, j: (l, 0, j)),
            pl.BlockSpec((None, 1, tn), lambda l, j: (l, 0, j)),
        ],
        out_specs=pl.BlockSpec((None, SUBLANE, tn), lambda l, j: (l, 0, j)),
        out_shape=jax.ShapeDtypeStruct((depth, SUBLANE, n), F32),
        compiler_params=_cparams(("arbitrary", "arbitrary")),
        name="adaln",
    )(c_pad, ada_w, ada_b.reshape(depth, 1, n))


def _row_spec(width):
    return pl.BlockSpec((None, ROW_TILE, width), lambda b, i: (b, i, 0))


def _mod_spec(layer, k):
    return pl.BlockSpec((None, SUBLANE, D_MODEL), lambda b, i: (layer, 0, k))


def _mod_row(ref):
    return ref[pl.ds(pl.program_id(0), 1), :]


def _const_spec(shape):
    return pl.BlockSpec(shape, lambda b, i: (0,) * len(shape))


def _layer_spec(shape, layer, col_block=0):
    return pl.BlockSpec((None,) + shape, lambda b, i: (layer, 0, col_block),
                        pipeline_mode=pl.Buffered(1))


def _gla_in_kernel(x_ref, g_ref, sh_ref, sc_ref, w_ref, wgate_ref, bgate_ref,
                   q_ref, k_ref, v_ref, r_ref, la_ref):
    hk = GLA_HEADS * GLA_DK
    hv = GLA_HEADS * GLA_DV
    h = _norm_mod(x_ref[...], g_ref[...], _mod_row(sh_ref), _mod_row(sc_ref)).astype(BF16)

    def proj(lo, width):
        return _dot(h, w_ref[:, lo:lo + width].astype(BF16))

    q_ref[...] = (proj(0, hk) * (GLA_DK ** -0.5)).astype(BF16)
    k_ref[...] = proj(hk, hk).astype(BF16)
    v_ref[...] = proj(2 * hk, hv).astype(BF16)
    r_ref[...] = proj(2 * hk + hv, hv).astype(BF16)
    low_rank = proj(2 * hk + 2 * hv, GLA_GATE_RANK).astype(BF16)
    z = _dot(low_rank, wgate_ref[...].astype(BF16)) + bgate_ref[...]
    log_sig = jnp.minimum(z, 0.0) - jnp.log(1.0 + jnp.exp(-jnp.abs(z)))
    la_ref[...] = log_sig / GLA_TAU


def _gla_in(x, mod_all, layer, gain, w_in, j, w_gate, b_gate):
    b, s, d = x.shape
    hk = GLA_HEADS * GLA_DK
    hv = GLA_HEADS * GLA_DV
    return pl.pallas_call(
        _gla_in_kernel,
        grid=(b, s // ROW_TILE),
        in_specs=[
            _row_spec(d), _const_spec((1, d)), _mod_spec(layer, 0), _mod_spec(layer, 1),
            _layer_spec((d, w_in.shape[-1]), j), _layer_spec((GLA_GATE_RANK, hk), j),
            _const_spec((1, hk)),
        ],
        out_specs=[_row_spec(hk), _row_spec(hk), _row_spec(hv), _row_spec(hv), _row_spec(hk)],
        out_shape=[
            jax.ShapeDtypeStruct((b, s, hk), BF16), jax.ShapeDtypeStruct((b, s, hk), BF16),
            jax.ShapeDtypeStruct((b, s, hv), BF16), jax.ShapeDtypeStruct((b, s, hv), BF16),
            jax.ShapeDtypeStruct((b, s, hk), F32),
        ],
        compiler_params=_cparams(("arbitrary", "arbitrary")),
        name="gla_in",
    )(x, gain.reshape(1, d), mod_all, mod_all, w_in, w_gate, b_gate.reshape(1, hk))


def _gla_kernel(q_ref, k_ref, v_ref, r_ref, la_ref, go_ref, o_ref, state_ref):
    c, sub = GLA_CHUNK, GLA_SUB
    nsub = c // sub
    assert nsub * c == 2 * LANE

    @pl.when(pl.program_id(2) == 0)
    def _():
        state_ref[...] = jnp.zeros_like(state_ref)

    row = lax.broadcasted_iota(jnp.int32, (c, c), 0)
    col = lax.broadcasted_iota(jnp.int32, (c, c), 1)
    tri = jnp.where(row >= col, 1.0, 0.0).astype(BF16)
    qrow = lax.broadcasted_iota(jnp.int32, (c, LANE), 0)
    lane = lax.broadcasted_iota(jnp.int32, (c, LANE), 1)
    causal = (lane % c) <= qrow
    keep_lo = causal & ((lane // c) == (qrow // sub))
    keep_hi = causal & ((lane // c) + LANE // c == (qrow // sub))

    n_chunks = GLA_TIME_BLOCK // c
    cs = [slice(ci * c, (ci + 1) * c) for ci in range(n_chunks)]

    bs = []
    for sl in cs:
        la = la_ref[sl, :]
        la_hi = la.astype(BF16)
        la_lo = (la - la_hi.astype(F32)).astype(BF16)
        bs.append(_dot(tri, la_hi) + _dot(tri, la_lo))

    qes, attns, kd_ts, decays = [], [], [], []
    for sl, b in zip(cs, bs):
        q = q_ref[sl, :].astype(F32)
        k = k_ref[sl, :].astype(F32)
        qes.append((q * jnp.exp(b)).astype(BF16))
        q_parts, k_parts = [], []
        for i in range(nsub):
            rows = slice(i * sub, (i + 1) * sub)
            live = (i + 1) * sub
            if i == 0:
                q_parts.append(q[rows] * jnp.exp(b[rows]))
                k_var = k[:live] * jnp.exp(-b[:live])
            else:
                anchor = b[i * sub - 1:i * sub, :]
                q_parts.append(q[rows] * jnp.exp(b[rows] - anchor))
                k_var = k[:live] * jnp.exp(anchor - b[:live])
            k_parts.append(k_var)
            if live < c:
                k_parts.append(jnp.zeros((c - live, GLA_DK), F32))
        q_anch = jnp.concatenate(q_parts, axis=0).astype(BF16)
        k_stack = jnp.concatenate(k_parts, axis=0).astype(BF16)
        prod = _dot_nt(q_anch, k_stack)
        attns.append(jnp.where(keep_lo, prod[:, :LANE],
                               jnp.where(keep_hi, prod[:, LANE:], 0.0)).astype(BF16))
        b_last = b[c - 1:c, :]
        kd_ts.append((k * jnp.exp(b_last - b)).T.astype(BF16))
        decays.append(jnp.broadcast_to(jnp.exp(b_last), (LANE, GLA_DK)).T)

    state = state_ref[...]
    gain = go_ref[...]
    for ci, sl in enumerate(cs):
        v = v_ref[sl, :]
        v2 = jnp.concatenate([v, v], axis=0)
        o = _dot(qes[ci], state.astype(BF16)) + _dot(attns[ci], v2)
        decay = jnp.concatenate([decays[ci]] * (GLA_DV // LANE), axis=1)
        state = state * decay + _dot(kd_ts[ci], v)
        r = r_ref[sl, :].astype(F32)
        o_ref[sl, :] = (_rms(o, gain) * (r * _sigmoid(r))).astype(BF16)
    state_ref[...] = state


def _gla(q, k, v, r, la, g_out):
    b, s, _ = q.shape
    t = GLA_TIME_BLOCK

    def spec(width):
        return pl.BlockSpec((None, t, width), lambda bi, h, ti: (bi, ti, h))

    return pl.pallas_call(
        _gla_kernel,
        grid=(b, GLA_HEADS, s // t),
        in_specs=[spec(GLA_DK), spec(GLA_DK), spec(GLA_DV), spec(GLA_DV), spec(GLA_DK),
                  pl.BlockSpec((1, GLA_DV), lambda bi, h, ti: (0, 0))],
        out_specs=spec(GLA_DV),
        out_shape=jax.ShapeDtypeStruct((b, s, GLA_HEADS * GLA_DV), BF16),
        scratch_shapes=[pltpu.VMEM((GLA_DK, GLA_DV), F32)],
        compiler_params=_cparams(("arbitrary", "arbitrary", "arbitrary")),
        name="gla_scan",
    )(q, k, v, r, la, g_out.reshape(1, GLA_DV))


def _out_res_kernel(a_ref, w_ref, x_ref, gate_ref, *rest, final_norm):
    o_ref = rest[-1]
    y = x_ref[...] + _mod_row(gate_ref) * _dot(a_ref[...], w_ref[...].astype(BF16))
    if final_norm:
        y = _rms(y, rest[0][...])
    o_ref[...] = y


def _out_res(a, w, j, x, mod_all, layer, gate_idx, final_gain=None):
    b, s, d = x.shape
    kdim = a.shape[-1]
    in_specs = [_row_spec(kdim), _layer_spec((kdim, d), j), _row_spec(d), _mod_spec(layer, gate_idx)]
    args = [a, w, x, mod_all]
    if final_gain is not None:
        in_specs.append(_const_spec((1, d)))
        args.append(final_gain.reshape(1, d))
    return pl.pallas_call(
        functools.partial(_out_res_kernel, final_norm=final_gain is not None),
        grid=(b, s // ROW_TILE),
        in_specs=in_specs,
        out_specs=_row_spec(d),
        out_shape=jax.ShapeDtypeStruct((b, s, d), F32),
        compiler_params=_cparams(("arbitrary", "arbitrary")),
        name="out_res",
    )(*args)


def _mix_out_ffn_in_kernel(m_ref, wo_ref, x_ref, g1_ref, g_ref, sh_ref, sc_ref, wg_ref, wu_ref,
                           x1_ref, a_ref):
    x1 = x_ref[...] + _mod_row(g1_ref) * _dot(m_ref[...], wo_ref[...].astype(BF16))
    x1_ref[...] = x1
    h = _norm_mod(x1, g_ref[...], _mod_row(sh_ref), _mod_row(sc_ref)).astype(BF16)
    for j in range(0, D_FF, FFN_COL_CHUNK):
        cols = slice(j, j + FFN_COL_CHUNK)
        gate = _dot(h, wg_ref[:, cols].astype(BF16))
        up = _dot(h, wu_ref[:, cols].astype(BF16))
        a_ref[:, cols] = (gate * _sigmoid(gate) * up).astype(BF16)


def _mix_out_ffn_in(mixed, w_o, j, x, mod_all, layer, gain, w_in):
    b, s, d = x.shape
    kdim = mixed.shape[-1]
    return pl.pallas_call(
        _mix_out_ffn_in_kernel,
        grid=(b, s // ROW_TILE),
        in_specs=[_row_spec(kdim), _layer_spec((kdim, d), j), _row_spec(d), _mod_spec(layer, 2),
                  _const_spec((1, d)), _mod_spec(layer, 3), _mod_spec(layer, 4),
                  _layer_spec((d, D_FF), layer, 0), _layer_spec((d, D_FF), layer, 1)],
        out_specs=[_row_spec(d), _row_spec(D_FF)],
        out_shape=[jax.ShapeDtypeStruct((b, s, d), F32), jax.ShapeDtypeStruct((b, s, D_FF), BF16)],
        compiler_params=_cparams(("arbitrary", "arbitrary")),
        name="mix_out_ffn_in",
    )(mixed, w_o, x, mod_all, gain.reshape(1, d), mod_all, mod_all, w_in, w_in)


def _mla_in_kernel(x_ref, g_ref, sh_ref, sc_ref, pos_ref, freq_ref, phase_ref, win_ref, gq_ref,
                   wq_ref, gkv_ref, wkn_ref, wvt_ref, q_ref, kc_ref, vt_ref, *, q_scale):
    h = _norm_mod(x_ref[...], g_ref[...], _mod_row(sh_ref), _mod_row(sc_ref)).astype(BF16)
    proj = _dot(h, win_ref[...])
    cq = _rms(proj[:, :MLA_Q_RANK], gq_ref[...]).astype(BF16)
    ckv = _rms(proj[:, MLA_Q_RANK:MLA_Q_RANK + MLA_KV_RANK], gkv_ref[...]).astype(BF16)

    pos = jnp.broadcast_to(pos_ref[...].astype(F32), (LANE, ROW_TILE)).T
    table = jnp.cos(pos * freq_ref[...] + phase_ref[...])
    rope_lanes = lax.broadcasted_iota(jnp.int32, (ROW_TILE, LANE), 1) < MLA_ROPE

    def rope(t):
        u = t * table
        return jnp.where(rope_lanes, u + pltpu.roll(u, MLA_ROPE, axis=1), 0.0)

    k_rope = rope(proj[:, MLA_Q_RANK + MLA_KV_RANK:]).astype(BF16)
    v_t = _dot_nt(wvt_ref[...], ckv)
    ones = jnp.ones((MLA_VT_ROWS - MLA_VDIM, v_t.shape[1]), BF16)
    k_nope = _dot(ckv, wkn_ref[...])
    q_all = _dot(cq, wq_ref[...])
    for hd in range(MLA_HEADS):
        lo = hd * MLA_QK_PAD
        mid = lo + MLA_NOPE
        hi = lo + MLA_QK_PAD
        q_ref[:, lo:mid] = (q_all[:, lo:mid] * q_scale).astype(BF16)
        q_ref[:, mid:hi] = (rope(q_all[:, mid:hi]) * q_scale).astype(BF16)
        kc_ref[:, lo:mid] = k_nope[:, hd * MLA_NOPE:(hd + 1) * MLA_NOPE].astype(BF16)
        kc_ref[:, mid:hi] = k_rope
        vt_ref[hd, :MLA_VDIM, :] = v_t[hd * MLA_VDIM:(hd + 1) * MLA_VDIM, :].astype(BF16)
        vt_ref[hd, MLA_VDIM:, :] = ones


def _half_split(w_rope):
    even, odd = w_rope[..., 0::2], w_rope[..., 1::2]
    return jnp.concatenate([even, odd, odd, even], axis=-1)


def _mla_in(x, mod_all, layer, gain, positions, w_in, g_q, w_q_up, g_kv, w_kv_up):
    b, s, d = x.shape
    hq = MLA_HEADS * MLA_QK_PAD
    hn = MLA_HEADS * MLA_NOPE
    hv = MLA_HEADS * MLA_VDIM
    lat = MLA_Q_RANK + MLA_KV_RANK
    half = MLA_ROPE // 2
    win = jnp.concatenate([w_in[:, :lat], _half_split(w_in[:, lat:])], axis=-1).astype(BF16)
    wq3 = w_q_up.reshape(MLA_Q_RANK, MLA_HEADS, MLA_NOPE + MLA_ROPE)
    wq = jnp.concatenate([wq3[..., :MLA_NOPE], _half_split(wq3[..., MLA_NOPE:])], axis=-1)
    wq = wq.reshape(MLA_Q_RANK, hq).astype(BF16)
    wkv3 = w_kv_up.reshape(MLA_KV_RANK, MLA_HEADS, MLA_NOPE + MLA_VDIM)
    wkn = wkv3[..., :MLA_NOPE].reshape(MLA_KV_RANK, hn).astype(BF16)
    wvt = wkv3[..., MLA_NOPE:].reshape(MLA_KV_RANK, hv).T.astype(BF16)
    win_w = win.shape[-1]
    inv_freq = ROPE_THETA ** (-jnp.arange(0, MLA_ROPE, 2, dtype=F32) / MLA_ROPE)
    freq_row = jnp.tile(inv_freq, LANE // half).reshape(1, LANE)
    quarter = jnp.full((half,), math.pi / 2, F32)
    phase_row = jnp.concatenate([jnp.zeros((2 * half,), F32), quarter, -quarter]).reshape(1, LANE)
    q_scale = (MLA_NOPE + MLA_ROPE) ** -0.5 * math.log2(math.e)
    return pl.pallas_call(
        functools.partial(_mla_in_kernel, q_scale=q_scale),
        grid=(b, s // ROW_TILE),
        in_specs=[
            _row_spec(d), _const_spec((1, d)), _mod_spec(layer, 0), _mod_spec(layer, 1),
            pl.BlockSpec((None, 1, ROW_TILE), lambda bi, i: (bi, 0, i)),
            _const_spec((1, LANE)), _const_spec((1, LANE)),
            _const_spec((d, win_w)), _const_spec((1, MLA_Q_RANK)), _const_spec((MLA_Q_RANK, hq)),
            _const_spec((1, MLA_KV_RANK)), _const_spec((MLA_KV_RANK, hn)),
            _const_spec((hv, MLA_KV_RANK)),
        ],
        out_specs=[_row_spec(hq), _row_spec(hq),
                   pl.BlockSpec((None, MLA_HEADS, MLA_VT_ROWS, ROW_TILE), lambda bi, i: (bi, 0, 0, i))],
        out_shape=[jax.ShapeDtypeStruct((b, s, hq), BF16), jax.ShapeDtypeStruct((b, s, hq), BF16),
                   jax.ShapeDtypeStruct((b, MLA_HEADS, MLA_VT_ROWS, s), BF16)],
        compiler_params=_cparams(("arbitrary", "arbitrary")),
        name="mla_in",
    )(x, gain.reshape(1, d), mod_all, mod_all, positions.reshape(b, 1, s), freq_row, phase_row,
      win, g_q.reshape(1, -1), wq, g_kv.reshape(1, -1), wkn, wvt)


def _attn_kernel(q_ref, k_ref, vt_ref, o_ref, s0_ref, s1_ref, mb0_ref, mb1_ref, m_ref, acc_ref):
    tq, tk = ATT_Q_BLOCK, ATT_KV_BLOCK
    per_q = tq // tk
    assert per_q == 2
    n_q = q_ref.shape[0] // tq
    strips = [slice(c0, c0 + ATT_STRIP) for c0 in range(0, tq, ATT_STRIP)]

    def scores(qi, kj, s_ref, mb_ref, diag_offset=None, strips=strips):
        start = pl.multiple_of(kj * tk, tk)
        k_blk = k_ref[pl.ds(start, tk), :]
        for cols in strips:
            q_strip = q_ref[pl.ds(pl.multiple_of(qi * tq + cols.start, ATT_STRIP), ATT_STRIP), :]
            s_t = _dot_nt(k_blk, q_strip)
            if diag_offset is not None:
                key = lax.broadcasted_iota(jnp.int32, (tk, ATT_STRIP), 0) + diag_offset
                qry = lax.broadcasted_iota(jnp.int32, (tk, ATT_STRIP), 1) + cols.start
                s_t = jnp.where(key <= qry, s_t, -jnp.inf)
            s_ref[:, cols] = s_t
            mb_ref[:, cols] = jnp.max(s_t, axis=0, keepdims=True)

    def accumulate(kj, s_ref, mb_ref, strips=strips):
        start = pl.multiple_of(kj * tk, tk)
        vt_blk = vt_ref[:, pl.ds(start, tk)]
        for cols in strips:
            m_prev = m_ref[:, cols]
            m_new = jnp.maximum(m_prev, mb_ref[:, cols])
            p_t = jnp.exp2((s_ref[:, cols] - m_new).astype(BF16))
            alpha = jnp.exp2(m_prev - m_new)
            acc_ref[:, cols] = alpha * acc_ref[:, cols] + _dot(vt_blk, p_t)
            m_ref[:, cols] = m_new

    def reset():
        m_ref[...] = jnp.full_like(m_ref, -jnp.inf)
        acc_ref[...] = jnp.zeros_like(acc_ref)

    def finish(qi, pending):
        accumulate(pending, s1_ref, mb1_ref)
        acc = acc_ref[...]
        o_t = acc[:MLA_VDIM, :] / acc[MLA_VDIM:MLA_VDIM + 1, :]
        o_ref[pl.ds(pl.multiple_of(qi * tq, tq), tq), :] = o_t.T.astype(BF16)

    def run_block(qi):
        first = per_q * qi
        reset()
        scores(qi, first + 1, s1_ref, mb1_ref, diag_offset=tk)
        accumulate(first, s0_ref, mb0_ref)

        def pair(p, pending):
            for cols in strips:
                accumulate(pending, s1_ref, mb1_ref, strips=[cols])
                scores(qi, 2 * p, s0_ref, mb0_ref, strips=[cols])
            for cols in strips:
                accumulate(2 * p, s0_ref, mb0_ref, strips=[cols])
                scores(qi, 2 * p + 1, s1_ref, mb1_ref, strips=[cols])
            return 2 * p + 1

        def quad(u, pending):
            return pair(2 * u + 1, pair(2 * u, pending))

        pending = lax.fori_loop(0, qi // 2, quad, first + 1)
        return lax.fori_loop(0, qi % 2, lambda _, pend: pair(qi - 1, pend), pending)

    scores(0, 0, s0_ref, mb0_ref, diag_offset=0)
    pending0 = run_block(0)

    def q_block(qi, pending):
        scores(qi, per_q * qi, s0_ref, mb0_ref, diag_offset=0)
        finish(qi - 1, pending)
        return run_block(qi)

    pending_last = lax.fori_loop(1, n_q, q_block, pending0)
    finish(n_q - 1, pending_last)


def _attention(q, kc, vt):
    b, _, _, s = vt.shape
    tq, tk = ATT_Q_BLOCK, ATT_KV_BLOCK
    return pl.pallas_call(
        _attn_kernel,
        grid=(b, MLA_HEADS),
        in_specs=[
            pl.BlockSpec((None, s, MLA_QK_PAD), lambda bi, h: (bi, 0, h)),
            pl.BlockSpec((None, s, MLA_QK_PAD), lambda bi, h: (bi, 0, h)),
            pl.BlockSpec((None, None, MLA_VT_ROWS, s), lambda bi, h: (bi, h, 0, 0)),
        ],
        out_specs=pl.BlockSpec((None, s, MLA_VDIM), lambda bi, h: (bi, 0, h)),
        out_shape=jax.ShapeDtypeStruct((b, s, MLA_HEADS * MLA_VDIM), BF16),
        scratch_shapes=[pltpu.VMEM((tk, tq), F32), pltpu.VMEM((tk, tq), F32),
                        pltpu.VMEM((1, tq), F32), pltpu.VMEM((1, tq), F32),
                        pltpu.VMEM((1, tq), F32), pltpu.VMEM((MLA_VT_ROWS, tq), F32)],
        compiler_params=_cparams(("arbitrary", "arbitrary")),
        name="mla_attention",
    )(q, kc, vt)


def kernel(x, c, positions, ada_w, ada_b, norm_mix, norm_ffn, gla_w_in, gla_w_gate, gla_b_gate,
           gla_g_out, gla_w_out, mla_w_in, mla_g_q, mla_w_q_up, mla_g_kv, mla_w_kv_up, mla_w_out,
           ffn_w_in, ffn_w_out, final_norm):
    batch = x.shape[0]
    depth = ada_w.shape[0]
    assert batch <= SUBLANE and ada_w.shape[-1] == N_MOD * D_MODEL
    c_pad = jnp.pad(c, ((0, SUBLANE - batch), (0, 0)))
    mod_all = _adaln(c_pad, ada_w, ada_b)

    for i in range(depth):
        j = i // 2
        if i % 2 == 0:
            q, k, v, r, la = _gla_in(x, mod_all, i, norm_mix[i], gla_w_in, j, gla_w_gate,
                                     gla_b_gate[j])
            mixed = _gla(q, k, v, r, la, gla_g_out[j])
            w_o = gla_w_out
        else:
            q, kc, vt = _mla_in(x, mod_all, i, norm_mix[i], positions, mla_w_in[j], mla_g_q[j],
                                mla_w_q_up[j], mla_g_kv[j], mla_w_kv_up[j])
            mixed = _attention(q, kc, vt)
            w_o = mla_w_out
        x, act = _mix_out_ffn_in(mixed, w_o, j, x, mod_all, i, norm_ffn[i], ffn_w_in)
        last = i == depth - 1
        x = _out_res(act, ffn_w_out, i, x, mod_all, i, 5, final_gain=final_norm if last else None)
    return x
```

```python
import functools
import math

import jax
import jax.numpy as jnp
from jax import lax
from jax.experimental import pallas as pl
from jax.experimental.pallas import tpu as pltpu

F32 = jnp.float32
BF16 = jnp.bfloat16

D_MODEL = 1024
EPS = 1e-6

GLA_HEADS = 4
GLA_DK = 128
GLA_DV = 256
GLA_GATE_RANK = 16
GLA_TAU = 16.0
GLA_CHUNK = 64
GLA_SUB = 16

MLA_HEADS = 8
MLA_NOPE = 128
MLA_ROPE = 64
MLA_VDIM = 128
MLA_Q_RANK = 384
MLA_KV_RANK = 256
ROPE_THETA = 10000.0
MLA_QK_PAD = 256
MLA_VT_ROWS = MLA_VDIM + 16

D_FF = 2816
N_MOD = 6

LANE = 128
SUBLANE = 8
ROW_TILE = 512
GLA_TIME_BLOCK = 512
ATT_Q_BLOCK = 1024
ATT_KV_BLOCK = 512
ATT_STRIP = 256
FFN_COL_CHUNK = 256
VMEM_LIMIT = 56 * 1024 * 1024


def _cparams(sem):
    return pltpu.CompilerParams(dimension_semantics=sem, vmem_limit_bytes=VMEM_LIMIT)


def _dot(a, b):
    return jnp.dot(a, b, preferred_element_type=F32)


def _dot_nt(a, b):
    return lax.dot_general(a, b, (((1,), (1,)), ((), ())), preferred_element_type=F32)


def _sigmoid(x):
    return 1.0 / (1.0 + jnp.exp(-x))


def _rms(x, g):
    return x * lax.rsqrt(jnp.mean(x * x, axis=-1, keepdims=True) + EPS) * g


def _norm_mod(x, g, shift, scale):
    return _rms(x, g) * (1.0 + scale) + shift


def _adaln_kernel(c_ref, w_ref, b_ref, o_ref):
    c = c_ref[...]
    o_ref[...] = _dot(c * _sigmoid(c), w_ref[...]) + b_ref[...]


def _adaln(c_pad, ada_w, ada_b):
    depth, d, n = ada_w.shape
    tn = 1536
    return pl.pallas_call(
        _adaln_kernel,
        grid=(depth, n // tn),
        in_specs=[
            pl.BlockSpec((SUBLANE, d), lambda l, j: (0, 0)),
            pl.BlockSpec((None, d, tn), lambda l, j: (l, 0, j)),
            pl.BlockSpec((None, 1, tn), lambda l, j: (l, 0, j)),
        ],
        out_specs=pl.BlockSpec((None, SUBLANE, tn), lambda l, j: (l, 0, j)),
        out_shape=jax.ShapeDtypeStruct((depth, SUBLANE, n), F32),
        compiler_params=_cparams(("arbitrary", "arbitrary")),
        name="adaln",
    )(c_pad, ada_w, ada_b.reshape(depth, 1, n))


def _row_spec(width):
    return pl.BlockSpec((None, ROW_TILE, width), lambda b, i: (b, i, 0))


def _mod_spec(layer, k):
    return pl.BlockSpec((None, SUBLANE, D_MODEL), lambda b, i: (layer, 0, k))


def _mod_row(ref):
    return ref[pl.ds(pl.program_id(0), 1), :]


def _const_spec(shape):
    return pl.BlockSpec(shape, lambda b, i: (0,) * len(shape))


def _layer_spec(shape, layer, col_block=0):
    return pl.BlockSpec((None,) + shape, lambda b, i: (layer, 0, col_block),
                        pipeline_mode=pl.Buffered(1))


def _gla_in_kernel(x_ref, g_ref, sh_ref, sc_ref, w_ref, wgate_ref, bgate_ref,
                   q_ref, k_ref, v_ref, r_ref, la_ref):
    hk = GLA_HEADS * GLA_DK
    hv = GLA_HEADS * GLA_DV
    h = _norm_mod(x_ref[...], g_ref[...], _mod_row(sh_ref), _mod_row(sc_ref)).astype(BF16)

    def proj(lo, width):
        return _dot(h, w_ref[:, lo:lo + width].astype(BF16))

    q_ref[...] = (proj(0, hk) * (GLA_DK ** -0.5)).astype(BF16)
    k_ref[...] = proj(hk, hk).astype(BF16)
    v_ref[...] = proj(2 * hk, hv).astype(BF16)
    r_ref[...] = proj(2 * hk + hv, hv).astype(BF16)
    low_rank = proj(2 * hk + 2 * hv, GLA_GATE_RANK).astype(BF16)
    z = _dot(low_rank, wgate_ref[...].astype(BF16)) + bgate_ref[...]
    log_sig = jnp.minimum(z, 0.0) - jnp.log(1.0 + jnp.exp(-jnp.abs(z)))
    la_ref[...] = log_sig / GLA_TAU


def _gla_in(x, mod_all, layer, gain, w_in, j, w_gate, b_gate):
    b, s, d = x.shape
    hk = GLA_HEADS * GLA_DK
    hv = GLA_HEADS * GLA_DV
    return pl.pallas_call(
        _gla_in_kernel,
        grid=(b, s // ROW_TILE),
        in_specs=[
            _row_spec(d), _const_spec((1, d)), _mod_spec(layer, 0), _mod_spec(layer, 1),
            _layer_spec((d, w_in.shape[-1]), j), _layer_spec((GLA_GATE_RANK, hk), j),
            _const_spec((1, hk)),
        ],
        out_specs=[_row_spec(hk), _row_spec(hk), _row_spec(hv), _row_spec(hv), _row_spec(hk)],
        out_shape=[
            jax.ShapeDtypeStruct((b, s, hk), BF16), jax.ShapeDtypeStruct((b, s, hk), BF16),
            jax.ShapeDtypeStruct((b, s, hv), BF16), jax.ShapeDtypeStruct((b, s, hv), BF16),
            jax.ShapeDtypeStruct((b, s, hk), F32),
        ],
        compiler_params=_cparams(("arbitrary", "arbitrary")),
        name="gla_in",
    )(x, gain.reshape(1, d), mod_all, mod_all, w_in, w_gate, b_gate.reshape(1, hk))


def _gla_kernel(q_ref, k_ref, v_ref, r_ref, la_ref, go_ref, o_ref, state_ref):
    c, sub = GLA_CHUNK, GLA_SUB
    nsub = c // sub
    assert nsub * c == 2 * LANE

    @pl.when(pl.program_id(2) == 0)
    def _():
        state_ref[...] = jnp.zeros_like(state_ref)

    row = lax.broadcasted_iota(jnp.int32, (c, c), 0)
    col = lax.broadcasted_iota(jnp.int32, (c, c), 1)
    tri = jnp.where(row >= col, 1.0, 0.0).astype(BF16)
    qrow = lax.broadcasted_iota(jnp.int32, (c, LANE), 0)
    lane = lax.broadcasted_iota(jnp.int32, (c, LANE), 1)
    causal = (lane % c) <= qrow
    keep_lo = causal & ((lane // c) == (qrow // sub))
    keep_hi = causal & ((lane // c) + LANE // c == (qrow // sub))

    n_chunks = GLA_TIME_BLOCK // c
    cs = [slice(ci * c, (ci + 1) * c) for ci in range(n_chunks)]

    bs = []
    for sl in cs:
        la = la_ref[sl, :]
        la_hi = la.astype(BF16)
        la_lo = (la - la_hi.astype(F32)).astype(BF16)
        bs.append(_dot(tri, la_hi) + _dot(tri, la_lo))

    qes, attns, kd_ts, decays = [], [], [], []
    for sl, b in zip(cs, bs):
        q = q_ref[sl, :].astype(F32)
        k = k_ref[sl, :].astype(F32)
        qes.append((q * jnp.exp(b)).astype(BF16))
        q_parts, k_parts = [], []
        for i in range(nsub):
            rows = slice(i * sub, (i + 1) * sub)
            live = (i + 1) * sub
            if i == 0:
                q_parts.append(q[rows] * jnp.exp(b[rows]))
                k_var = k[:live] * jnp.exp(-b[:live])
            else:
                anchor = b[i * sub - 1:i * sub, :]
                q_parts.append(q[rows] * jnp.exp(b[rows] - anchor))
                k_var = k[:live] * jnp.exp(anchor - b[:live])
            k_parts.append(k_var)
            if live < c:
                k_parts.append(jnp.zeros((c - live, GLA_DK), F32))
        q_anch = jnp.concatenate(q_parts, axis=0).astype(BF16)
        k_stack = jnp.concatenate(k_parts, axis=0).astype(BF16)
        prod = _dot_nt(q_anch, k_stack)
        attns.append(jnp.where(keep_lo, prod[:, :LANE],
                               jnp.where(keep_hi, prod[:, LANE:], 0.0)).astype(BF16))
        b_last = b[c - 1:c, :]
        kd_ts.append((k * jnp.exp(b_last - b)).T.astype(BF16))
        decays.append(jnp.broadcast_to(jnp.exp(b_last), (LANE, GLA_DK)).T)

    state = state_ref[...]
    gain = go_ref[...]
    for ci, sl in enumerate(cs):
        v = v_ref[sl, :]
        v2 = jnp.concatenate([v, v], axis=0)
        o = _dot(qes[ci], state.astype(BF16)) + _dot(attns[ci], v2)
        decay = jnp.concatenate([decays[ci]] * (GLA_DV // LANE), axis=1)
        state = state * decay + _dot(kd_ts[ci], v)
        r = r_ref[sl, :].astype(F32)
        o_ref[sl, :] = (_rms(o, gain) * (r * _sigmoid(r))).astype(BF16)
    state_ref[...] = state


def _gla(q, k, v, r, la, g_out):
    b, s, _ = q.shape
    t = GLA_TIME_BLOCK

    def spec(width):
        return pl.BlockSpec((None, t, width), lambda bi, h, ti: (bi, ti, h))

    return pl.pallas_call(
        _gla_kernel,
        grid=(b, GLA_HEADS, s // t),
        in_specs=[spec(GLA_DK), spec(GLA_DK), spec(GLA_DV), spec(GLA_DV), spec(GLA_DK),
                  pl.BlockSpec((1, GLA_DV), lambda bi, h, ti: (0, 0))],
        out_specs=spec(GLA_DV),
        out_shape=jax.ShapeDtypeStruct((b, s, GLA_HEADS * GLA_DV), BF16),
        scratch_shapes=[pltpu.VMEM((GLA_DK, GLA_DV), F32)],
        compiler_params=_cparams(("arbitrary", "arbitrary", "arbitrary")),
        name="gla_scan",
    )(q, k, v, r, la, g_out.reshape(1, GLA_DV))


def _out_res_kernel(a_ref, w_ref, x_ref, gate_ref, *rest, final_norm):
    o_ref = rest[-1]
    y = x_ref[...] + _mod_row(gate_ref) * _dot(a_ref[...], w_ref[...].astype(BF16))
    if final_norm:
        y = _rms(y, rest[0][...])
    o_ref[...] = y


def _out_res(a, w, j, x, mod_all, layer, gate_idx, final_gain=None):
    b, s, d = x.shape
    kdim = a.shape[-1]
    in_specs = [_row_spec(kdim), _layer_spec((kdim, d), j), _row_spec(d), _mod_spec(layer, gate_idx)]
    args = [a, w, x, mod_all]
    if final_gain is not None:
        in_specs.append(_const_spec((1, d)))
        args.append(final_gain.reshape(1, d))
    return pl.pallas_call(
        functools.partial(_out_res_kernel, final_norm=final_gain is not None),
        grid=(b, s // ROW_TILE),
        in_specs=in_specs,
        out_specs=_row_spec(d),
        out_shape=jax.ShapeDtypeStruct((b, s, d), F32),
        compiler_params=_cparams(("arbitrary", "arbitrary")),
        name="out_res",
    )(*args)


def _mix_out_ffn_in_kernel(m_ref, wo_ref, x_ref, g1_ref, g_ref, sh_ref, sc_ref, wg_ref, wu_ref,
                           x1_ref, a_ref):
    x1 = x_ref[...] + _mod_row(g1_ref) * _dot(m_ref[...], wo_ref[...].astype(BF16))
    x1_ref[...] = x1
    h = _norm_mod(x1, g_ref[...], _mod_row(sh_ref), _mod_row(sc_ref)).astype(BF16)
    for j in range(0, D_FF, FFN_COL_CHUNK):
        cols = slice(j, j + FFN_COL_CHUNK)
        gate = _dot(h, wg_ref[:, cols].astype(BF16))
        up = _dot(h, wu_ref[:, cols].astype(BF16))
        a_ref[:, cols] = (gate * _sigmoid(gate) * up).astype(BF16)


def _mix_out_ffn_in(mixed, w_o, j, x, mod_all, layer, gain, w_in):
    b, s, d = x.shape
    kdim = mixed.shape[-1]
    return pl.pallas_call(
        _mix_out_ffn_in_kernel,
        grid=(b, s // ROW_TILE),
        in_specs=[_row_spec(kdim), _layer_spec((kdim, d), j), _row_spec(d), _mod_spec(layer, 2),
                  _const_spec((1, d)), _mod_spec(layer, 3), _mod_spec(layer, 4),
                  _layer_spec((d, D_FF), layer, 0), _layer_spec((d, D_FF), layer, 1)],
        out_specs=[_row_spec(d), _row_spec(D_FF)],
        out_shape=[jax.ShapeDtypeStruct((b, s, d), F32), jax.ShapeDtypeStruct((b, s, D_FF), BF16)],
        compiler_params=_cparams(("arbitrary", "arbitrary")),
        name="mix_out_ffn_in",
    )(mixed, w_o, x, mod_all, gain.reshape(1, d), mod_all, mod_all, w_in, w_in)


def _mla_in_kernel(x_ref, g_ref, sh_ref, sc_ref, pos_ref, freq_ref, phase_ref, win_ref, gq_ref,
                   wq_ref, gkv_ref, wkn_ref, wvt_ref, q_ref, kc_ref, vt_ref, *, q_scale):
    h = _norm_mod(x_ref[...], g_ref[...], _mod_row(sh_ref), _mod_row(sc_ref)).astype(BF16)
    proj = _dot(h, win_ref[...])
    cq = _rms(proj[:, :MLA_Q_RANK], gq_ref[...]).astype(BF16)
    ckv = _rms(proj[:, MLA_Q_RANK:MLA_Q_RANK + MLA_KV_RANK], gkv_ref[...]).astype(BF16)

    pos = jnp.broadcast_to(pos_ref[...].astype(F32), (LANE, ROW_TILE)).T
    table = jnp.cos(pos * freq_ref[...] + phase_ref[...])
    rope_lanes = lax.broadcasted_iota(jnp.int32, (ROW_TILE, LANE), 1) < MLA_ROPE

    def rope(t):
        u = t * table
        return jnp.where(rope_lanes, u + pltpu.roll(u, MLA_ROPE, axis=1), 0.0)

    k_rope = rope(proj[:, MLA_Q_RANK + MLA_KV_RANK:]).astype(BF16)
    v_t = _dot_nt(wvt_ref[...], ckv)
    ones = jnp.ones((MLA_VT_ROWS - MLA_VDIM, v_t.shape[1]), BF16)
    k_nope = _dot(ckv, wkn_ref[...])
    q_all = _dot(cq, wq_ref[...])
    for hd in range(MLA_HEADS):
        lo = hd * MLA_QK_PAD
        mid = lo + MLA_NOPE
        hi = lo + MLA_QK_PAD
        q_ref[:, lo:mid] = (q_all[:, lo:mid] * q_scale).astype(BF16)
        q_ref[:, mid:hi] = (rope(q_all[:, mid:hi]) * q_scale).astype(BF16)
        kc_ref[:, lo:mid] = k_nope[:, hd * MLA_NOPE:(hd + 1) * MLA_NOPE].astype(BF16)
        kc_ref[:, mid:hi] = k_rope
        vt_ref[hd, :MLA_VDIM, :] = v_t[hd * MLA_VDIM:(hd + 1) * MLA_VDIM, :].astype(BF16)
        vt_ref[hd, MLA_VDIM:, :] = ones


def _half_split(w_rope):
    even, odd = w_rope[..., 0::2], w_rope[..., 1::2]
    return jnp.concatenate([even, odd, odd, even], axis=-1)


def _mla_in(x, mod_all, layer, gain, positions, w_in, g_q, w_q_up, g_kv, w_kv_up):
    b, s, d = x.shape
    hq = MLA_HEADS * MLA_QK_PAD
    hn = MLA_HEADS * MLA_NOPE
    hv = MLA_HEADS * MLA_VDIM
    lat = MLA_Q_RANK + MLA_KV_RANK
    half = MLA_ROPE // 2
    win = jnp.concatenate([w_in[:, :lat], _half_split(w_in[:, lat:])], axis=-1).astype(BF16)
    wq3 = w_q_up.reshape(MLA_Q_RANK, MLA_HEADS, MLA_NOPE + MLA_ROPE)
    wq = jnp.concatenate([wq3[..., :MLA_NOPE], _half_split(wq3[..., MLA_NOPE:])], axis=-1)
    wq = wq.reshape(MLA_Q_RANK, hq).astype(BF16)
    wkv3 = w_kv_up.reshape(MLA_KV_RANK, MLA_HEADS, MLA_NOPE + MLA_VDIM)
    wkn = wkv3[..., :MLA_NOPE].reshape(MLA_KV_RANK, hn).astype(BF16)
    wvt = wkv3[..., MLA_NOPE:].reshape(MLA_KV_RANK, hv).T.astype(BF16)
    win_w = win.shape[-1]
    inv_freq = ROPE_THETA ** (-jnp.arange(0, MLA_ROPE, 2, dtype=F32) / MLA_ROPE)
    freq_row = jnp.tile(inv_freq, LANE // half).reshape(1, LANE)
    quarter = jnp.full((half,), math.pi / 2, F32)
    phase_row = jnp.concatenate([jnp.zeros((2 * half,), F32), quarter, -quarter]).reshape(1, LANE)
    q_scale = (MLA_NOPE + MLA_ROPE) ** -0.5 * math.log2(math.e)
    return pl.pallas_call(
        functools.partial(_mla_in_kernel, q_scale=q_scale),
        grid=(b, s // ROW_TILE),
        in_specs=[
            _row_spec(d), _const_spec((1, d)), _mod_spec(layer, 0), _mod_spec(layer, 1),
            pl.BlockSpec((None, 1, ROW_TILE), lambda bi, i: (bi, 0, i)),
            _const_spec((1, LANE)), _const_spec((1, LANE)),
            _const_spec((d, win_w)), _const_spec((1, MLA_Q_RANK)), _const_spec((MLA_Q_RANK, hq)),
            _const_spec((1, MLA_KV_RANK)), _const_spec((MLA_KV_RANK, hn)),
            _const_spec((hv, MLA_KV_RANK)),
        ],
        out_specs=[_row_spec(hq), _row_spec(hq),
                   pl.BlockSpec((None, MLA_HEADS, MLA_VT_ROWS, ROW_TILE), lambda bi, i: (bi, 0, 0, i))],
        out_shape=[jax.ShapeDtypeStruct((b, s, hq), BF16), jax.ShapeDtypeStruct((b, s, hq), BF16),
                   jax.ShapeDtypeStruct((b, MLA_HEADS, MLA_VT_ROWS, s), BF16)],
        compiler_params=_cparams(("arbitrary", "arbitrary")),
        name="mla_in",
    )(x, gain.reshape(1, d), mod_all, mod_all, positions.reshape(b, 1, s), freq_row, phase_row,
      win, g_q.reshape(1, -1), wq, g_kv.reshape(1, -1), wkn, wvt)


def _attn_kernel(q_ref, k_ref, vt_ref, o_ref, s0_ref, s1_ref, mb0_ref, mb1_ref, m_ref, acc_ref):
    tq, tk = ATT_Q_BLOCK, ATT_KV_BLOCK
    per_q = tq // tk
    assert per_q == 2
    n_q = q_ref.shape[0] // tq
    n_strips = tq // ATT_STRIP
    all_strips = tuple(range(n_strips))

    def lanes(c):
        return slice(c * ATT_STRIP, (c + 1) * ATT_STRIP)

    def scores(qi, kj, s_ref, mb_ref, diag_offset=None, strips=all_strips):
        start = pl.multiple_of(kj * tk, tk)
        k_blk = k_ref[pl.ds(start, tk), :]
        for c in strips:
            q_lo = c * ATT_STRIP
            if diag_offset is not None and diag_offset > q_lo + ATT_STRIP - 1:
                s_ref[c] = jnp.full((tk, ATT_STRIP), -jnp.inf, F32)
                mb_ref[:, lanes(c)] = jnp.full((1, ATT_STRIP), -jnp.inf, F32)
                continue
            q_strip = q_ref[pl.ds(pl.multiple_of(qi * tq + q_lo, ATT_STRIP), ATT_STRIP), :]
            s_t = _dot_nt(k_blk, q_strip)
            if diag_offset is not None and diag_offset + tk - 1 > q_lo:
                key = lax.broadcasted_iota(jnp.int32, (tk, ATT_STRIP), 0) + diag_offset
                qry = lax.broadcasted_iota(jnp.int32, (tk, ATT_STRIP), 1) + q_lo
                s_t = jnp.where(key <= qry, s_t, -jnp.inf)
            s_ref[c] = s_t
            mb_ref[:, lanes(c)] = jnp.max(s_t, axis=0, keepdims=True)

    def accumulate(kj, s_ref, mb_ref, strips=all_strips):
        start = pl.multiple_of(kj * tk, tk)
        vt_blk = vt_ref[:, pl.ds(start, tk)]
        for c in strips:
            cols = lanes(c)
            m_prev = m_ref[:, cols]
            m_new = jnp.maximum(m_prev, mb_ref[:, cols])
            p_t = jnp.exp2((s_ref[c] - m_new).astype(BF16))
            alpha = jnp.exp2(m_prev - m_new)
            acc_ref[:, cols] = alpha * acc_ref[:, cols] + _dot(vt_blk, p_t)
            m_ref[:, cols] = m_new

    def reset():
        m_ref[...] = jnp.full_like(m_ref, -jnp.inf)
        acc_ref[...] = jnp.zeros_like(acc_ref)

    def finish(qi, pending):
        accumulate(pending, s1_ref, mb1_ref)
        acc = acc_ref[...]
        o_t = acc[:MLA_VDIM, :] / acc[MLA_VDIM:MLA_VDIM + 1, :]
        o_ref[pl.ds(pl.multiple_of(qi * tq, tq), tq), :] = o_t.T.astype(BF16)

    def run_block(qi):
        first = per_q * qi
        reset()
        scores(qi, first + 1, s1_ref, mb1_ref, diag_offset=tk)
        accumulate(first, s0_ref, mb0_ref)

        def pair(p, pending):
            for c in all_strips:
                accumulate(pending, s1_ref, mb1_ref, strips=(c,))
                scores(qi, 2 * p, s0_ref, mb0_ref, strips=(c,))
            for c in all_strips:
                accumulate(2 * p, s0_ref, mb0_ref, strips=(c,))
                scores(qi, 2 * p + 1, s1_ref, mb1_ref, strips=(c,))
            return 2 * p + 1

        def quad(u, pending):
            return pair(2 * u + 1, pair(2 * u, pending))

        pending = lax.fori_loop(0, qi // 2, quad, first + 1)
        return lax.fori_loop(0, qi % 2, lambda _, pend: pair(qi - 1, pend), pending)

    scores(0, 0, s0_ref, mb0_ref, diag_offset=0)
    pending0 = run_block(0)

    def q_block(qi, pending):
        scores(qi, per_q * qi, s0_ref, mb0_ref, diag_offset=0)
        finish(qi - 1, pending)
        return run_block(qi)

    pending_last = lax.fori_loop(1, n_q, q_block, pending0)
    finish(n_q - 1, pending_last)


def _attention(q, kc, vt):
    b, _, _, s = vt.shape
    tq, tk = ATT_Q_BLOCK, ATT_KV_BLOCK
    return pl.pallas_call(
        _attn_kernel,
        grid=(b, MLA_HEADS),
        in_specs=[
            pl.BlockSpec((None, s, MLA_QK_PAD), lambda bi, h: (bi, 0, h)),
            pl.BlockSpec((None, s, MLA_QK_PAD), lambda bi, h: (bi, 0, h)),
            pl.BlockSpec((None, None, MLA_VT_ROWS, s), lambda bi, h: (bi, h, 0, 0)),
        ],
        out_specs=pl.BlockSpec((None, s, MLA_VDIM), lambda bi, h: (bi, 0, h)),
        out_shape=jax.ShapeDtypeStruct((b, s, MLA_HEADS * MLA_VDIM), BF16),
        scratch_shapes=[pltpu.VMEM((tq // ATT_STRIP, tk, ATT_STRIP), F32),
                        pltpu.VMEM((tq // ATT_STRIP, tk, ATT_STRIP), F32),
                        pltpu.VMEM((1, tq), F32), pltpu.VMEM((1, tq), F32),
                        pltpu.VMEM((1, tq), F32), pltpu.VMEM((MLA_VT_ROWS, tq), F32)],
        compiler_params=_cparams(("arbitrary", "arbitrary")),
        name="mla_attention",
    )(q, kc, vt)


def kernel(x, c, positions, ada_w, ada_b, norm_mix, norm_ffn, gla_w_in, gla_w_gate, gla_b_gate,
           gla_g_out, gla_w_out, mla_w_in, mla_g_q, mla_w_q_up, mla_g_kv, mla_w_kv_up, mla_w_out,
           ffn_w_in, ffn_w_out, final_norm):
    batch = x.shape[0]
    depth = ada_w.shape[0]
    assert batch <= SUBLANE and ada_w.shape[-1] == N_MOD * D_MODEL
    c_pad = jnp.pad(c, ((0, SUBLANE - batch), (0, 0)))
    mod_all = _adaln(c_pad, ada_w, ada_b)

    for i in range(depth):
        j = i // 2
        if i % 2 == 0:
            q, k, v, r, la = _gla_in(x, mod_all, i, norm_mix[i], gla_w_in, j, gla_w_gate,
                                     gla_b_gate[j])
            mixed = _gla(q, k, v, r, la, gla_g_out[j])
            w_o = gla_w_out
        else:
            q, kc, vt = _mla_in(x, mod_all, i, norm_mix[i], positions, mla_w_in[j], mla_g_q[j],
                                mla_w_q_up[j], mla_g_kv[j], mla_w_kv_up[j])
            mixed = _attention(q, kc, vt)
            w_o = mla_w_out
        x, act = _mix_out_ffn_in(mixed, w_o, j, x, mod_all, i, norm_ffn[i], ffn_w_in)
        last = i == depth - 1
        x = _out_res(act, ffn_w_out, i, x, mod_all, i, 5, final_gain=final_norm if last else None)
    return x
```

```python
import functools
import math

import jax
import jax.numpy as jnp
from jax import lax
from jax.experimental import pallas as pl
from jax.experimental.pallas import tpu as pltpu

F32 = jnp.float32
BF16 = jnp.bfloat16

D_MODEL = 1024
EPS = 1e-6

GLA_HEADS = 4
GLA_DK = 128
GLA_DV = 256
GLA_GATE_RANK = 16
GLA_TAU = 16.0
GLA_CHUNK = 64
GLA_SUB = 16

MLA_HEADS = 8
MLA_NOPE = 128
MLA_ROPE = 64
MLA_VDIM = 128
MLA_Q_RANK = 384
MLA_KV_RANK = 256
ROPE_THETA = 10000.0
MLA_QK_PAD = 256
MLA_VT_ROWS = MLA_VDIM + 16

D_FF = 2816
N_MOD = 6

LANE = 128
SUBLANE = 8
ROW_TILE = 512
GLA_TIME_BLOCK = 512
ATT_Q_BLOCK = 1024
ATT_KV_BLOCK = 512
ATT_STRIP = 256
FFN_COL_CHUNK = 256
VMEM_LIMIT = 56 * 1024 * 1024


def _cparams(sem):
    return pltpu.CompilerParams(dimension_semantics=sem, vmem_limit_bytes=VMEM_LIMIT)


def _dot(a, b):
    return jnp.dot(a, b, preferred_element_type=F32)


def _dot_nt(a, b):
    return lax.dot_general(a, b, (((1,), (1,)), ((), ())), preferred_element_type=F32)


def _sigmoid(x):
    return 1.0 / (1.0 + jnp.exp(-x))


def _rms(x, g):
    return x * lax.rsqrt(jnp.mean(x * x, axis=-1, keepdims=True) + EPS) * g


def _norm_mod(x, g, shift, scale):
    return _rms(x, g) * (1.0 + scale) + shift


def _adaln_kernel(c_ref, w_ref, b_ref, o_ref):
    c = c_ref[...]
    o_ref[...] = _dot(c * _sigmoid(c), w_ref[...]) + b_ref[...]


def _adaln(c_pad, ada_w, ada_b):
    depth, d, n = ada_w.shape
    tn = 1536
    return pl.pallas_call(
        _adaln_kernel,
        grid=(depth, n // tn),
        in_specs=[
            pl.BlockSpec((SUBLANE, d), lambda l, j: (0, 0)),
            pl.BlockSpec((None, d, tn), lambda l, j: (l, 0, j)),
            pl.BlockSpec((None, 1, tn), lambda l, j: (l, 0, j)),
        ],
        out_specs=pl.BlockSpec((None, SUBLANE, tn), lambda l, j: (l, 0, j)),
        out_shape=jax.ShapeDtypeStruct((depth, SUBLANE, n), F32),
        compiler_params=_cparams(("arbitrary", "arbitrary")),
        name="adaln",
    )(c_pad, ada_w, ada_b.reshape(depth, 1, n))


def _row_spec(width):
    return pl.BlockSpec((None, ROW_TILE, width), lambda b, i: (b, i, 0))


def _mod_spec(layer, k):
    return pl.BlockSpec((None, SUBLANE, D_MODEL), lambda b, i: (layer, 0, k))


def _mod_row(ref):
    return ref[pl.ds(pl.program_id(0), 1), :]


def _const_spec(shape):
    return pl.BlockSpec(shape, lambda b, i: (0,) * len(shape))


def _layer_spec(shape, layer, col_block=0):
    return pl.BlockSpec((None,) + shape, lambda b, i: (layer, 0, col_block),
                        pipeline_mode=pl.Buffered(1))


def _gla_in_kernel(x_ref, g_ref, sh_ref, sc_ref, w_ref, wgate_ref, bgate_ref,
                   q_ref, k_ref, v_ref, r_ref, la_ref):
    hk = GLA_HEADS * GLA_DK
    hv = GLA_HEADS * GLA_DV
    h = _norm_mod(x_ref[...], g_ref[...], _mod_row(sh_ref), _mod_row(sc_ref)).astype(BF16)

    def proj(lo, width):
        return _dot_nt(h, w_ref[lo:lo + width, :].astype(BF16))

    q_ref[...] = (proj(0, hk) * (GLA_DK ** -0.5)).astype(BF16)
    k_ref[...] = proj(hk, hk).astype(BF16)
    v_ref[...] = proj(2 * hk, hv).astype(BF16)
    r_ref[...] = proj(2 * hk + hv, hv).astype(BF16)
    low_rank = proj(2 * hk + 2 * hv, GLA_GATE_RANK).astype(BF16)
    z = _dot(low_rank, wgate_ref[...].astype(BF16)) + bgate_ref[...]
    log_sig = jnp.minimum(z, 0.0) - jnp.log(1.0 + jnp.exp(-jnp.abs(z)))
    la_ref[...] = log_sig / GLA_TAU


def _gla_in(x, mod_all, layer, gain, w_in, j, w_gate, b_gate):
    b, s, d = x.shape
    hk = GLA_HEADS * GLA_DK
    hv = GLA_HEADS * GLA_DV
    return pl.pallas_call(
        _gla_in_kernel,
        grid=(b, s // ROW_TILE),
        in_specs=[
            _row_spec(d), _const_spec((1, d)), _mod_spec(layer, 0), _mod_spec(layer, 1),
            _layer_spec((w_in.shape[-1], d), j), _layer_spec((GLA_GATE_RANK, hk), j),
            _const_spec((1, hk)),
        ],
        out_specs=[_row_spec(hk), _row_spec(hk), _row_spec(hv), _row_spec(hv), _row_spec(hk)],
        out_shape=[
            jax.ShapeDtypeStruct((b, s, hk), BF16), jax.ShapeDtypeStruct((b, s, hk), BF16),
            jax.ShapeDtypeStruct((b, s, hv), BF16), jax.ShapeDtypeStruct((b, s, hv), BF16),
            jax.ShapeDtypeStruct((b, s, hk), F32),
        ],
        compiler_params=_cparams(("arbitrary", "arbitrary")),
        name="gla_in",
    )(x, gain.reshape(1, d), mod_all, mod_all, jnp.swapaxes(w_in, 1, 2), w_gate,
      b_gate.reshape(1, hk))


def _gla_kernel(q_ref, k_ref, v_ref, r_ref, la_ref, go_ref, o_ref, state_ref):
    c, sub = GLA_CHUNK, GLA_SUB
    nsub = c // sub
    assert nsub * c == 2 * LANE

    @pl.when(pl.program_id(2) == 0)
    def _():
        state_ref[...] = jnp.zeros_like(state_ref)

    row = lax.broadcasted_iota(jnp.int32, (c, c), 0)
    col = lax.broadcasted_iota(jnp.int32, (c, c), 1)
    tri = jnp.where(row >= col, 1.0, 0.0).astype(BF16)
    qrow = lax.broadcasted_iota(jnp.int32, (c, LANE), 0)
    lane = lax.broadcasted_iota(jnp.int32, (c, LANE), 1)
    causal = (lane % c) <= qrow
    keep_lo = causal & ((lane // c) == (qrow // sub))
    keep_hi = causal & ((lane // c) + LANE // c == (qrow // sub))

    n_chunks = GLA_TIME_BLOCK // c
    cs = [slice(ci * c, (ci + 1) * c) for ci in range(n_chunks)]

    bs = []
    for sl in cs:
        la = la_ref[sl, :]
        la_hi = la.astype(BF16)
        la_lo = (la - la_hi.astype(F32)).astype(BF16)
        bs.append(_dot(tri, la_hi) + _dot(tri, la_lo))

    qes, attns, kd_ts, decays = [], [], [], []
    for sl, b in zip(cs, bs):
        q = q_ref[sl, :].astype(F32)
        k = k_ref[sl, :].astype(F32)
        qes.append((q * jnp.exp(b)).astype(BF16))
        q_parts, k_parts = [], []
        for i in range(nsub):
            rows = slice(i * sub, (i + 1) * sub)
            live = (i + 1) * sub
            if i == 0:
                q_parts.append(q[rows] * jnp.exp(b[rows]))
                k_var = k[:live] * jnp.exp(-b[:live])
            else:
                anchor = b[i * sub - 1:i * sub, :]
                q_parts.append(q[rows] * jnp.exp(b[rows] - anchor))
                k_var = k[:live] * jnp.exp(anchor - b[:live])
            k_parts.append(k_var)
            if live < c:
                k_parts.append(jnp.zeros((c - live, GLA_DK), F32))
        q_anch = jnp.concatenate(q_parts, axis=0).astype(BF16)
        k_stack = jnp.concatenate(k_parts, axis=0).astype(BF16)
        prod = _dot_nt(q_anch, k_stack)
        attns.append(jnp.where(keep_lo, prod[:, :LANE],
                               jnp.where(keep_hi, prod[:, LANE:], 0.0)).astype(BF16))
        b_last = b[c - 1:c, :]
        kd_ts.append((k * jnp.exp(b_last - b)).T.astype(BF16))
        decays.append(jnp.broadcast_to(jnp.exp(b_last), (LANE, GLA_DK)).T)

    state = state_ref[...]
    gain = go_ref[...]
    for ci, sl in enumerate(cs):
        v = v_ref[sl, :]
        v2 = jnp.concatenate([v, v], axis=0)
        o = _dot(qes[ci], state.astype(BF16)) + _dot(attns[ci], v2)
        decay = jnp.concatenate([decays[ci]] * (GLA_DV // LANE), axis=1)
        state = state * decay + _dot(kd_ts[ci], v)
        r = r_ref[sl, :].astype(F32)
        o_ref[sl, :] = (_rms(o, gain) * (r * _sigmoid(r))).astype(BF16)
    state_ref[...] = state


def _gla(q, k, v, r, la, g_out):
    b, s, _ = q.shape
    t = GLA_TIME_BLOCK

    def spec(width):
        return pl.BlockSpec((None, t, width), lambda bi, h, ti: (bi, ti, h))

    return pl.pallas_call(
        _gla_kernel,
        grid=(b, GLA_HEADS, s // t),
        in_specs=[spec(GLA_DK), spec(GLA_DK), spec(GLA_DV), spec(GLA_DV), spec(GLA_DK),
                  pl.BlockSpec((1, GLA_DV), lambda bi, h, ti: (0, 0))],
        out_specs=spec(GLA_DV),
        out_shape=jax.ShapeDtypeStruct((b, s, GLA_HEADS * GLA_DV), BF16),
        scratch_shapes=[pltpu.VMEM((GLA_DK, GLA_DV), F32)],
        compiler_params=_cparams(("arbitrary", "arbitrary", "arbitrary")),
        name="gla_scan",
    )(q, k, v, r, la, g_out.reshape(1, GLA_DV))


def _out_res_kernel(a_ref, w_ref, x_ref, gate_ref, *rest, final_norm):
    o_ref = rest[-1]
    y = x_ref[...] + _mod_row(gate_ref) * _dot(a_ref[...], w_ref[...].astype(BF16))
    if final_norm:
        y = _rms(y, rest[0][...])
    o_ref[...] = y


def _out_res(a, w, j, x, mod_all, layer, gate_idx, final_gain=None):
    b, s, d = x.shape
    kdim = a.shape[-1]
    in_specs = [_row_spec(kdim), _layer_spec((kdim, d), j), _row_spec(d), _mod_spec(layer, gate_idx)]
    args = [a, w, x, mod_all]
    if final_gain is not None:
        in_specs.append(_const_spec((1, d)))
        args.append(final_gain.reshape(1, d))
    return pl.pallas_call(
        functools.partial(_out_res_kernel, final_norm=final_gain is not None),
        grid=(b, s // ROW_TILE),
        in_specs=in_specs,
        out_specs=_row_spec(d),
        out_shape=jax.ShapeDtypeStruct((b, s, d), F32),
        compiler_params=_cparams(("arbitrary", "arbitrary")),
        name="out_res",
    )(*args)


def _mix_out_ffn_in_kernel(m_ref, wo_ref, x_ref, g1_ref, g_ref, sh_ref, sc_ref, wg_ref, wu_ref,
                           x1_ref, a_ref):
    x1 = x_ref[...] + _mod_row(g1_ref) * _dot(m_ref[...], wo_ref[...].astype(BF16))
    x1_ref[...] = x1
    h = _norm_mod(x1, g_ref[...], _mod_row(sh_ref), _mod_row(sc_ref)).astype(BF16)
    for j in range(0, D_FF, FFN_COL_CHUNK):
        cols = slice(j, j + FFN_COL_CHUNK)
        gate = _dot(h, wg_ref[:, cols].astype(BF16))
        up = _dot(h, wu_ref[:, cols].astype(BF16))
        a_ref[:, cols] = (gate * _sigmoid(gate) * up).astype(BF16)


def _mix_out_ffn_in(mixed, w_o, j, x, mod_all, layer, gain, w_in):
    b, s, d = x.shape
    kdim = mixed.shape[-1]
    return pl.pallas_call(
        _mix_out_ffn_in_kernel,
        grid=(b, s // ROW_TILE),
        in_specs=[_row_spec(kdim), _layer_spec((kdim, d), j), _row_spec(d), _mod_spec(layer, 2),
                  _const_spec((1, d)), _mod_spec(layer, 3), _mod_spec(layer, 4),
                  _layer_spec((d, D_FF), layer, 0), _layer_spec((d, D_FF), layer, 1)],
        out_specs=[_row_spec(d), _row_spec(D_FF)],
        out_shape=[jax.ShapeDtypeStruct((b, s, d), F32), jax.ShapeDtypeStruct((b, s, D_FF), BF16)],
        compiler_params=_cparams(("arbitrary", "arbitrary")),
        name="mix_out_ffn_in",
    )(mixed, w_o, x, mod_all, gain.reshape(1, d), mod_all, mod_all, w_in, w_in)


def _mla_in_kernel(x_ref, g_ref, sh_ref, sc_ref, pos_ref, freq_ref, phase_ref, win_ref, gq_ref,
                   wq_ref, gkv_ref, wkn_ref, wvt_ref, q_ref, kc_ref, vt_ref, *, q_scale):
    h = _norm_mod(x_ref[...], g_ref[...], _mod_row(sh_ref), _mod_row(sc_ref)).astype(BF16)
    proj = _dot(h, win_ref[...])
    cq = _rms(proj[:, :MLA_Q_RANK], gq_ref[...]).astype(BF16)
    ckv = _rms(proj[:, MLA_Q_RANK:MLA_Q_RANK + MLA_KV_RANK], gkv_ref[...]).astype(BF16)

    pos = jnp.broadcast_to(pos_ref[...].astype(F32), (LANE, ROW_TILE)).T
    table = jnp.cos(pos * freq_ref[...] + phase_ref[...])
    rope_lanes = lax.broadcasted_iota(jnp.int32, (ROW_TILE, LANE), 1) < MLA_ROPE

    def rope(t):
        u = t * table
        return jnp.where(rope_lanes, u + pltpu.roll(u, MLA_ROPE, axis=1), 0.0)

    k_rope = rope(proj[:, MLA_Q_RANK + MLA_KV_RANK:]).astype(BF16)
    v_t = _dot_nt(wvt_ref[...], ckv)
    ones = jnp.ones((MLA_VT_ROWS - MLA_VDIM, v_t.shape[1]), BF16)
    k_nope = _dot(ckv, wkn_ref[...])
    q_all = _dot(cq, wq_ref[...])
    for hd in range(MLA_HEADS):
        lo = hd * MLA_QK_PAD
        mid = lo + MLA_NOPE
        hi = lo + MLA_QK_PAD
        q_ref[:, lo:mid] = (q_all[:, lo:mid] * q_scale).astype(BF16)
        q_ref[:, mid:hi] = (rope(q_all[:, mid:hi]) * q_scale).astype(BF16)
        kc_ref[:, lo:mid] = k_nope[:, hd * MLA_NOPE:(hd + 1) * MLA_NOPE].astype(BF16)
        kc_ref[:, mid:hi] = k_rope
        vt_ref[hd, :MLA_VDIM, :] = v_t[hd * MLA_VDIM:(hd + 1) * MLA_VDIM, :].astype(BF16)
        vt_ref[hd, MLA_VDIM:, :] = ones


def _half_split(w_rope):
    even, odd = w_rope[..., 0::2], w_rope[..., 1::2]
    return jnp.concatenate([even, odd, odd, even], axis=-1)


def _mla_in(x, mod_all, layer, gain, positions, w_in, g_q, w_q_up, g_kv, w_kv_up):
    b, s, d = x.shape
    hq = MLA_HEADS * MLA_QK_PAD
    hn = MLA_HEADS * MLA_NOPE
    hv = MLA_HEADS * MLA_VDIM
    lat = MLA_Q_RANK + MLA_KV_RANK
    half = MLA_ROPE // 2
    win = jnp.concatenate([w_in[:, :lat], _half_split(w_in[:, lat:])], axis=-1).astype(BF16)
    wq3 = w_q_up.reshape(MLA_Q_RANK, MLA_HEADS, MLA_NOPE + MLA_ROPE)
    wq = jnp.concatenate([wq3[..., :MLA_NOPE], _half_split(wq3[..., MLA_NOPE:])], axis=-1)
    wq = wq.reshape(MLA_Q_RANK, hq).astype(BF16)
    wkv3 = w_kv_up.reshape(MLA_KV_RANK, MLA_HEADS, MLA_NOPE + MLA_VDIM)
    wkn = wkv3[..., :MLA_NOPE].reshape(MLA_KV_RANK, hn).astype(BF16)
    wvt = wkv3[..., MLA_NOPE:].reshape(MLA_KV_RANK, hv).T.astype(BF16)
    win_w = win.shape[-1]
    inv_freq = ROPE_THETA ** (-jnp.arange(0, MLA_ROPE, 2, dtype=F32) / MLA_ROPE)
    freq_row = jnp.tile(inv_freq, LANE // half).reshape(1, LANE)
    quarter = jnp.full((half,), math.pi / 2, F32)
    phase_row = jnp.concatenate([jnp.zeros((2 * half,), F32), quarter, -quarter]).reshape(1, LANE)
    q_scale = (MLA_NOPE + MLA_ROPE) ** -0.5 * math.log2(math.e)
    return pl.pallas_call(
        functools.partial(_mla_in_kernel, q_scale=q_scale),
        grid=(b, s // ROW_TILE),
        in_specs=[
            _row_spec(d), _const_spec((1, d)), _mod_spec(layer, 0), _mod_spec(layer, 1),
            pl.BlockSpec((None, 1, ROW_TILE), lambda bi, i: (bi, 0, i)),
            _const_spec((1, LANE)), _const_spec((1, LANE)),
            _const_spec((d, win_w)), _const_spec((1, MLA_Q_RANK)), _const_spec((MLA_Q_RANK, hq)),
            _const_spec((1, MLA_KV_RANK)), _const_spec((MLA_KV_RANK, hn)),
            _const_spec((hv, MLA_KV_RANK)),
        ],
        out_specs=[_row_spec(hq), _row_spec(hq),
                   pl.BlockSpec((None, MLA_HEADS, MLA_VT_ROWS, ROW_TILE), lambda bi, i: (bi, 0, 0, i))],
        out_shape=[jax.ShapeDtypeStruct((b, s, hq), BF16), jax.ShapeDtypeStruct((b, s, hq), BF16),
                   jax.ShapeDtypeStruct((b, MLA_HEADS, MLA_VT_ROWS, s), BF16)],
        compiler_params=_cparams(("arbitrary", "arbitrary")),
        name="mla_in",
    )(x, gain.reshape(1, d), mod_all, mod_all, positions.reshape(b, 1, s), freq_row, phase_row,
      win, g_q.reshape(1, -1), wq, g_kv.reshape(1, -1), wkn, wvt)


def _attn_kernel(q_ref, k_ref, vt_ref, o_ref, s0_ref, s1_ref, mb0_ref, mb1_ref, m_ref, acc_ref):
    tq, tk = ATT_Q_BLOCK, ATT_KV_BLOCK
    per_q = tq // tk
    assert per_q == 2
    n_q = q_ref.shape[0] // tq
    n_strips = tq // ATT_STRIP
    all_strips = tuple(range(n_strips))

    def lanes(c):
        return slice(c * ATT_STRIP, (c + 1) * ATT_STRIP)

    def scores(qi, kj, s_ref, mb_ref, diag_offset=None, strips=all_strips):
        start = pl.multiple_of(kj * tk, tk)
        k_blk = k_ref[pl.ds(start, tk), :]
        for c in strips:
            q_lo = c * ATT_STRIP
            if diag_offset is not None and diag_offset > q_lo + ATT_STRIP - 1:
                s_ref[c] = jnp.full((tk, ATT_STRIP), -jnp.inf, F32)
                mb_ref[:, lanes(c)] = jnp.full((1, ATT_STRIP), -jnp.inf, F32)
                continue
            q_strip = q_ref[pl.ds(pl.multiple_of(qi * tq + q_lo, ATT_STRIP), ATT_STRIP), :]
            s_t = _dot_nt(k_blk, q_strip)
            if diag_offset is not None and diag_offset + tk - 1 > q_lo:
                key = lax.broadcasted_iota(jnp.int32, (tk, ATT_STRIP), 0) + diag_offset
                qry = lax.broadcasted_iota(jnp.int32, (tk, ATT_STRIP), 1) + q_lo
                s_t = jnp.where(key <= qry, s_t, -jnp.inf)
            s_ref[c] = s_t
            mb_ref[:, lanes(c)] = jnp.max(s_t, axis=0, keepdims=True)

    def accumulate(kj, s_ref, mb_ref, strips=all_strips):
        start = pl.multiple_of(kj * tk, tk)
        vt_blk = vt_ref[:, pl.ds(start, tk)]
        for c in strips:
            cols = lanes(c)
            m_prev = m_ref[:, cols]
            m_new = jnp.maximum(m_prev, mb_ref[:, cols])
            p_t = jnp.exp2((s_ref[c] - m_new).astype(BF16))
            alpha = jnp.exp2(m_prev - m_new)
            acc_ref[:, cols] = alpha * acc_ref[:, cols] + _dot(vt_blk, p_t)
            m_ref[:, cols] = m_new

    def reset():
        m_ref[...] = jnp.full_like(m_ref, -jnp.inf)
        acc_ref[...] = jnp.zeros_like(acc_ref)

    def finish(qi, pending):
        accumulate(pending, s1_ref, mb1_ref)
        acc = acc_ref[...]
        o_t = acc[:MLA_VDIM, :] / acc[MLA_VDIM:MLA_VDIM + 1, :]
        o_ref[pl.ds(pl.multiple_of(qi * tq, tq), tq), :] = o_t.T.astype(BF16)

    def run_block(qi):
        first = per_q * qi
        reset()
        scores(qi, first + 1, s1_ref, mb1_ref, diag_offset=tk)
        accumulate(first, s0_ref, mb0_ref)

        def pair(p, pending):
            for c in all_strips:
                accumulate(pending, s1_ref, mb1_ref, strips=(c,))
                scores(qi, 2 * p, s0_ref, mb0_ref, strips=(c,))
            for c in all_strips:
                accumulate(2 * p, s0_ref, mb0_ref, strips=(c,))
                scores(qi, 2 * p + 1, s1_ref, mb1_ref, strips=(c,))
            return 2 * p + 1

        def quad(u, pending):
            return pair(2 * u + 1, pair(2 * u, pending))

        pending = lax.fori_loop(0, qi // 2, quad, first + 1)
        return lax.fori_loop(0, qi % 2, lambda _, pend: pair(qi - 1, pend), pending)

    scores(0, 0, s0_ref, mb0_ref, diag_offset=0)
    pending0 = run_block(0)

    def q_block(qi, pending):
        scores(qi, per_q * qi, s0_ref, mb0_ref, diag_offset=0)
        finish(qi - 1, pending)
        return run_block(qi)

    pending_last = lax.fori_loop(1, n_q, q_block, pending0)
    finish(n_q - 1, pending_last)


def _attention(q, kc, vt):
    b, _, _, s = vt.shape
    tq, tk = ATT_Q_BLOCK, ATT_KV_BLOCK
    return pl.pallas_call(
        _attn_kernel,
        grid=(b, MLA_HEADS),
        in_specs=[
            pl.BlockSpec((None, s, MLA_QK_PAD), lambda bi, h: (bi, 0, h)),
            pl.BlockSpec((None, s, MLA_QK_PAD), lambda bi, h: (bi, 0, h)),
            pl.BlockSpec((None, None, MLA_VT_ROWS, s), lambda bi, h: (bi, h, 0, 0)),
        ],
        out_specs=pl.BlockSpec((None, s, MLA_VDIM), lambda bi, h: (bi, 0, h)),
        out_shape=jax.ShapeDtypeStruct((b, s, MLA_HEADS * MLA_VDIM), BF16),
        scratch_shapes=[pltpu.VMEM((tq // ATT_STRIP, tk, ATT_STRIP), F32),
                        pltpu.VMEM((tq // ATT_STRIP, tk, ATT_STRIP), F32),
                        pltpu.VMEM((1, tq), F32), pltpu.VMEM((1, tq), F32),
                        pltpu.VMEM((1, tq), F32), pltpu.VMEM((MLA_VT_ROWS, tq), F32)],
        compiler_params=_cparams(("arbitrary", "arbitrary")),
        name="mla_attention",
    )(q, kc, vt)


def kernel(x, c, positions, ada_w, ada_b, norm_mix, norm_ffn, gla_w_in, gla_w_gate, gla_b_gate,
           gla_g_out, gla_w_out, mla_w_in, mla_g_q, mla_w_q_up, mla_g_kv, mla_w_kv_up, mla_w_out,
           ffn_w_in, ffn_w_out, final_norm):
    batch = x.shape[0]
    depth = ada_w.shape[0]
    assert batch <= SUBLANE and ada_w.shape[-1] == N_MOD * D_MODEL
    c_pad = jnp.pad(c, ((0, SUBLANE - batch), (0, 0)))
    mod_all = _adaln(c_pad, ada_w, ada_b)

    for i in range(depth):
        j = i // 2
        if i % 2 == 0:
            q, k, v, r, la = _gla_in(x, mod_all, i, norm_mix[i], gla_w_in, j, gla_w_gate,
                                     gla_b_gate[j])
            mixed = _gla(q, k, v, r, la, gla_g_out[j])
            w_o = gla_w_out
        else:
            q, kc, vt = _mla_in(x, mod_all, i, norm_mix[i], positions, mla_w_in[j], mla_g_q[j],
                                mla_w_q_up[j], mla_g_kv[j], mla_w_kv_up[j])
            mixed = _attention(q, kc, vt)
            w_o = mla_w_out
        x, act = _mix_out_ffn_in(mixed, w_o, j, x, mod_all, i, norm_ffn[i], ffn_w_in)
        last = i == depth - 1
        x = _out_res(act, ffn_w_out, i, x, mod_all, i, 5, final_gain=final_norm if last else None)
    return x
```

```python
import functools
import math

import jax
import jax.numpy as jnp
from jax import lax
from jax.experimental import pallas as pl
from jax.experimental.pallas import tpu as pltpu

F32 = jnp.float32
BF16 = jnp.bfloat16

D_MODEL = 1024
EPS = 1e-6

GLA_HEADS = 4
GLA_DK = 128
GLA_DV = 256
GLA_GATE_RANK = 16
GLA_TAU = 16.0
GLA_CHUNK = 64
GLA_SUB = 16

MLA_HEADS = 8
MLA_NOPE = 128
MLA_ROPE = 64
MLA_VDIM = 128
MLA_Q_RANK = 384
MLA_KV_RANK = 256
ROPE_THETA = 10000.0
MLA_QK_PAD = 256
MLA_VT_ROWS = MLA_VDIM + 16

D_FF = 2816
N_MOD = 6

LANE = 128
SUBLANE = 8
ROW_TILE = 512
GLA_TIME_BLOCK = 2048
ATT_Q_BLOCK = 1024
ATT_KV_BLOCK = 512
ATT_STRIP = 256
FFN_COL_CHUNK = 256
VMEM_LIMIT = 56 * 1024 * 1024


def _cparams(sem):
    return pltpu.CompilerParams(dimension_semantics=sem, vmem_limit_bytes=VMEM_LIMIT)


def _dot(a, b):
    return jnp.dot(a, b, preferred_element_type=F32)


def _dot_nt(a, b):
    return lax.dot_general(a, b, (((1,), (1,)), ((), ())), preferred_element_type=F32)


def _sigmoid(x):
    return 1.0 / (1.0 + jnp.exp(-x))


def _rms(x, g):
    return x * lax.rsqrt(jnp.mean(x * x, axis=-1, keepdims=True) + EPS) * g


def _norm_mod(x, g, shift, scale):
    return _rms(x, g) * (1.0 + scale) + shift


def _adaln_kernel(c_ref, w_ref, b_ref, o_ref):
    c = c_ref[...]
    o_ref[...] = _dot(c * _sigmoid(c), w_ref[...]) + b_ref[...]


def _adaln(c_pad, ada_w, ada_b):
    depth, d, n = ada_w.shape
    tn = 1536
    return pl.pallas_call(
        _adaln_kernel,
        grid=(depth, n // tn),
        in_specs=[
            pl.BlockSpec((SUBLANE, d), lambda l, j: (0, 0)),
            pl.BlockSpec((None, d, tn), lambda l, j: (l, 0, j)),
            pl.BlockSpec((None, 1, tn), lambda l, j: (l, 0, j)),
        ],
        out_specs=pl.BlockSpec((None, SUBLANE, tn), lambda l, j: (l, 0, j)),
        out_shape=jax.ShapeDtypeStruct((depth, SUBLANE, n), F32),
        compiler_params=_cparams(("arbitrary", "arbitrary")),
        name="adaln",
    )(c_pad, ada_w, ada_b.reshape(depth, 1, n))


def _row_spec(width):
    return pl.BlockSpec((None, ROW_TILE, width), lambda b, i: (b, i, 0))


def _mod_spec(layer, k):
    return pl.BlockSpec((None, SUBLANE, D_MODEL), lambda b, i: (layer, 0, k))


def _mod_row(ref):
    return ref[pl.ds(pl.program_id(0), 1), :]


def _const_spec(shape):
    return pl.BlockSpec(shape, lambda b, i: (0,) * len(shape))


def _layer_spec(shape, layer, col_block=0):
    return pl.BlockSpec((None,) + shape, lambda b, i: (layer, 0, col_block),
                        pipeline_mode=pl.Buffered(1))


def _gla_in_kernel(x_ref, g_ref, sh_ref, sc_ref, w_ref, wgate_ref, bgate_ref,
                   q_ref, k_ref, v_ref, r_ref, la_ref):
    hk = GLA_HEADS * GLA_DK
    hv = GLA_HEADS * GLA_DV
    h = _norm_mod(x_ref[...], g_ref[...], _mod_row(sh_ref), _mod_row(sc_ref)).astype(BF16)

    def proj(lo, width):
        return _dot_nt(h, w_ref[lo:lo + width, :].astype(BF16))

    q_ref[...] = (proj(0, hk) * (GLA_DK ** -0.5)).astype(BF16)
    k_ref[...] = proj(hk, hk).astype(BF16)
    v_ref[...] = proj(2 * hk, hv).astype(BF16)
    r_ref[...] = proj(2 * hk + hv, hv).astype(BF16)
    low_rank = proj(2 * hk + 2 * hv, GLA_GATE_RANK).astype(BF16)
    z = _dot(low_rank, wgate_ref[...].astype(BF16)) + bgate_ref[...]
    log_sig = jnp.minimum(z, 0.0) - jnp.log(1.0 + jnp.exp(-jnp.abs(z)))
    la_ref[...] = log_sig * (math.log2(math.e) / GLA_TAU)


def _gla_in(x, mod_all, layer, gain, w_in, j, w_gate, b_gate):
    b, s, d = x.shape
    hk = GLA_HEADS * GLA_DK
    hv = GLA_HEADS * GLA_DV
    return pl.pallas_call(
        _gla_in_kernel,
        grid=(b, s // ROW_TILE),
        in_specs=[
            _row_spec(d), _const_spec((1, d)), _mod_spec(layer, 0), _mod_spec(layer, 1),
            _layer_spec((w_in.shape[-1], d), j), _layer_spec((GLA_GATE_RANK, hk), j),
            _const_spec((1, hk)),
        ],
        out_specs=[_row_spec(hk), _row_spec(hk), _row_spec(hv), _row_spec(hv), _row_spec(hk)],
        out_shape=[
            jax.ShapeDtypeStruct((b, s, hk), BF16), jax.ShapeDtypeStruct((b, s, hk), BF16),
            jax.ShapeDtypeStruct((b, s, hv), BF16), jax.ShapeDtypeStruct((b, s, hv), BF16),
            jax.ShapeDtypeStruct((b, s, hk), F32),
        ],
        compiler_params=_cparams(("arbitrary", "arbitrary")),
        name="gla_in",
    )(x, gain.reshape(1, d), mod_all, mod_all, jnp.swapaxes(w_in, 1, 2), w_gate,
      b_gate.reshape(1, hk))


def _gla_kernel(q_ref, k_ref, v_ref, r_ref, la_ref, go_ref, o_ref, state_ref):
    c, sub = GLA_CHUNK, GLA_SUB
    nsub = c // sub
    assert nsub * c == 2 * LANE

    @pl.when(pl.program_id(2) == 0)
    def _():
        state_ref[...] = jnp.zeros_like(state_ref)

    row = lax.broadcasted_iota(jnp.int32, (c, c), 0)
    col = lax.broadcasted_iota(jnp.int32, (c, c), 1)
    tri = jnp.where(row >= col, 1.0, 0.0).astype(BF16)
    qrow = lax.broadcasted_iota(jnp.int32, (c, LANE), 0)
    lane = lax.broadcasted_iota(jnp.int32, (c, LANE), 1)
    causal = (lane % c) <= qrow
    keep_lo = causal & ((lane // c) == (qrow // sub))
    keep_hi = causal & ((lane // c) + LANE // c == (qrow // sub))

    n_chunks = GLA_TIME_BLOCK // c
    cs = [slice(ci * c, (ci + 1) * c) for ci in range(n_chunks)]

    bs = []
    for sl in cs:
        la = la_ref[sl, :]
        la_hi = la.astype(BF16)
        la_lo = (la - la_hi.astype(F32)).astype(BF16)
        bs.append(_dot(tri, la_hi) + _dot(tri, la_lo))

    qes, attns, kd_ts, decays = [], [], [], []
    for sl, b in zip(cs, bs):
        q = q_ref[sl, :].astype(F32)
        k = k_ref[sl, :].astype(F32)
        qes.append((q * jnp.exp2(b)).astype(BF16))
        q_parts, k_parts = [], []
        for i in range(nsub):
            rows = slice(i * sub, (i + 1) * sub)
            live = (i + 1) * sub
            if i == 0:
                q_parts.append(q[rows] * jnp.exp2(b[rows]))
                k_var = k[:live] * jnp.exp2(-b[:live])
            else:
                anchor = b[i * sub - 1:i * sub, :]
                q_parts.append(q[rows] * jnp.exp2(b[rows] - anchor))
                k_var = k[:live] * jnp.exp2(anchor - b[:live])
            k_parts.append(k_var)
            if live < c:
                k_parts.append(jnp.zeros((c - live, GLA_DK), F32))
        q_anch = jnp.concatenate(q_parts, axis=0).astype(BF16)
        k_stack = jnp.concatenate(k_parts, axis=0).astype(BF16)
        prod = _dot_nt(q_anch, k_stack)
        attns.append(jnp.where(keep_lo, prod[:, :LANE],
                               jnp.where(keep_hi, prod[:, LANE:], 0.0)).astype(BF16))
        b_last = b[c - 1:c, :]
        kd_ts.append((k * jnp.exp2(b_last - b)).T.astype(BF16))
        decays.append(jnp.broadcast_to(jnp.exp2(b_last), (LANE, GLA_DK)).T)

    state = state_ref[...]
    gain = go_ref[...]
    for ci, sl in enumerate(cs):
        v = v_ref[sl, :]
        v2 = jnp.concatenate([v, v], axis=0)
        o = _dot(qes[ci], state.astype(BF16)) + _dot(attns[ci], v2)
        decay = jnp.concatenate([decays[ci]] * (GLA_DV // LANE), axis=1)
        state = state * decay + _dot(kd_ts[ci], v)
        half_r = 0.5 * r_ref[sl, :].astype(F32)
        silu_r = half_r + half_r * jnp.tanh(half_r)
        o_ref[sl, :] = (_rms(o, gain) * silu_r).astype(BF16)
    state_ref[...] = state


def _gla(q, k, v, r, la, g_out):
    b, s, _ = q.shape
    t = GLA_TIME_BLOCK

    def spec(width):
        return pl.BlockSpec((None, t, width), lambda bi, h, ti: (bi, ti, h))

    return pl.pallas_call(
        _gla_kernel,
        grid=(b, GLA_HEADS, s // t),
        in_specs=[spec(GLA_DK), spec(GLA_DK), spec(GLA_DV), spec(GLA_DV), spec(GLA_DK),
                  pl.BlockSpec((1, GLA_DV), lambda bi, h, ti: (0, 0))],
        out_specs=spec(GLA_DV),
        out_shape=jax.ShapeDtypeStruct((b, s, GLA_HEADS * GLA_DV), BF16),
        scratch_shapes=[pltpu.VMEM((GLA_DK, GLA_DV), F32)],
        compiler_params=_cparams(("arbitrary", "arbitrary", "arbitrary")),
        name="gla_scan",
    )(q, k, v, r, la, g_out.reshape(1, GLA_DV))


def _out_res_kernel(a_ref, w_ref, x_ref, gate_ref, *rest, final_norm):
    o_ref = rest[-1]
    y = x_ref[...] + _mod_row(gate_ref) * _dot(a_ref[...], w_ref[...].astype(BF16))
    if final_norm:
        y = _rms(y, rest[0][...])
    o_ref[...] = y


def _out_res(a, w, j, x, mod_all, layer, gate_idx, final_gain=None):
    b, s, d = x.shape
    kdim = a.shape[-1]
    in_specs = [_row_spec(kdim), _layer_spec((kdim, d), j), _row_spec(d), _mod_spec(layer, gate_idx)]
    args = [a, w, x, mod_all]
    if final_gain is not None:
        in_specs.append(_const_spec((1, d)))
        args.append(final_gain.reshape(1, d))
    return pl.pallas_call(
        functools.partial(_out_res_kernel, final_norm=final_gain is not None),
        grid=(b, s // ROW_TILE),
        in_specs=in_specs,
        out_specs=_row_spec(d),
        out_shape=jax.ShapeDtypeStruct((b, s, d), F32),
        compiler_params=_cparams(("arbitrary", "arbitrary")),
        name="out_res",
    )(*args)


def _mix_out_ffn_in_kernel(m_ref, wo_ref, x_ref, g1_ref, g_ref, sh_ref, sc_ref, wg_ref, wu_ref,
                           x1_ref, a_ref):
    x1 = x_ref[...] + _mod_row(g1_ref) * _dot(m_ref[...], wo_ref[...].astype(BF16))
    x1_ref[...] = x1
    h = _norm_mod(x1, g_ref[...], _mod_row(sh_ref), _mod_row(sc_ref)).astype(BF16)
    for j in range(0, D_FF, FFN_COL_CHUNK):
        cols = slice(j, j + FFN_COL_CHUNK)
        gate = _dot(h, wg_ref[:, cols].astype(BF16))
        up = _dot(h, wu_ref[:, cols].astype(BF16))
        a_ref[:, cols] = (gate * _sigmoid(gate) * up).astype(BF16)


def _mix_out_ffn_in(mixed, w_o, j, x, mod_all, layer, gain, w_in):
    b, s, d = x.shape
    kdim = mixed.shape[-1]
    return pl.pallas_call(
        _mix_out_ffn_in_kernel,
        grid=(b, s // ROW_TILE),
        in_specs=[_row_spec(kdim), _layer_spec((kdim, d), j), _row_spec(d), _mod_spec(layer, 2),
                  _const_spec((1, d)), _mod_spec(layer, 3), _mod_spec(layer, 4),
                  _layer_spec((d, D_FF), layer, 0), _layer_spec((d, D_FF), layer, 1)],
        out_specs=[_row_spec(d), _row_spec(D_FF)],
        out_shape=[jax.ShapeDtypeStruct((b, s, d), F32), jax.ShapeDtypeStruct((b, s, D_FF), BF16)],
        compiler_params=_cparams(("arbitrary", "arbitrary")),
        name="mix_out_ffn_in",
    )(mixed, w_o, x, mod_all, gain.reshape(1, d), mod_all, mod_all, w_in, w_in)


def _mla_in_kernel(x_ref, g_ref, sh_ref, sc_ref, pos_ref, freq_ref, phase_ref, win_ref, gq_ref,
                   wq_ref, gkv_ref, wkn_ref, wvt_ref, q_ref, kc_ref, vt_ref, *, q_scale):
    h = _norm_mod(x_ref[...], g_ref[...], _mod_row(sh_ref), _mod_row(sc_ref)).astype(BF16)
    proj = _dot(h, win_ref[...])
    cq = _rms(proj[:, :MLA_Q_RANK], gq_ref[...]).astype(BF16)
    ckv = _rms(proj[:, MLA_Q_RANK:MLA_Q_RANK + MLA_KV_RANK], gkv_ref[...]).astype(BF16)

    pos = jnp.broadcast_to(pos_ref[...].astype(F32), (LANE, ROW_TILE)).T
    table = jnp.cos(pos * freq_ref[...] + phase_ref[...])
    rope_lanes = lax.broadcasted_iota(jnp.int32, (ROW_TILE, LANE), 1) < MLA_ROPE

    def rope(t):
        u = t * table
        return jnp.where(rope_lanes, u + pltpu.roll(u, MLA_ROPE, axis=1), 0.0)

    k_rope = rope(proj[:, MLA_Q_RANK + MLA_KV_RANK:]).astype(BF16)
    v_t = _dot_nt(wvt_ref[...], ckv)
    ones = jnp.ones((MLA_VT_ROWS - MLA_VDIM, v_t.shape[1]), BF16)
    k_nope = _dot(ckv, wkn_ref[...])
    q_all = _dot(cq, wq_ref[...])
    for hd in range(MLA_HEADS):
        lo = hd * MLA_QK_PAD
        mid = lo + MLA_NOPE
        hi = lo + MLA_QK_PAD
        q_ref[:, lo:mid] = (q_all[:, lo:mid] * q_scale).astype(BF16)
        q_ref[:, mid:hi] = (rope(q_all[:, mid:hi]) * q_scale).astype(BF16)
        kc_ref[:, lo:mid] = k_nope[:, hd * MLA_NOPE:(hd + 1) * MLA_NOPE].astype(BF16)
        kc_ref[:, mid:hi] = k_rope
        vt_ref[hd, :MLA_VDIM, :] = v_t[hd * MLA_VDIM:(hd + 1) * MLA_VDIM, :].astype(BF16)
        vt_ref[hd, MLA_VDIM:, :] = ones


def _half_split(w_rope):
    even, odd = w_rope[..., 0::2], w_rope[..., 1::2]
    return jnp.concatenate([even, odd, odd, even], axis=-1)


def _mla_in(x, mod_all, layer, gain, positions, w_in, g_q, w_q_up, g_kv, w_kv_up):
    b, s, d = x.shape
    hq = MLA_HEADS * MLA_QK_PAD
    hn = MLA_HEADS * MLA_NOPE
    hv = MLA_HEADS * MLA_VDIM
    lat = MLA_Q_RANK + MLA_KV_RANK
    half = MLA_ROPE // 2
    win = jnp.concatenate([w_in[:, :lat], _half_split(w_in[:, lat:])], axis=-1).astype(BF16)
    wq3 = w_q_up.reshape(MLA_Q_RANK, MLA_HEADS, MLA_NOPE + MLA_ROPE)
    wq = jnp.concatenate([wq3[..., :MLA_NOPE], _half_split(wq3[..., MLA_NOPE:])], axis=-1)
    wq = wq.reshape(MLA_Q_RANK, hq).astype(BF16)
    wkv3 = w_kv_up.reshape(MLA_KV_RANK, MLA_HEADS, MLA_NOPE + MLA_VDIM)
    wkn = wkv3[..., :MLA_NOPE].reshape(MLA_KV_RANK, hn).astype(BF16)
    wvt = wkv3[..., MLA_NOPE:].reshape(MLA_KV_RANK, hv).T.astype(BF16)
    win_w = win.shape[-1]
    inv_freq = ROPE_THETA ** (-jnp.arange(0, MLA_ROPE, 2, dtype=F32) / MLA_ROPE)
    freq_row = jnp.tile(inv_freq, LANE // half).reshape(1, LANE)
    quarter = jnp.full((half,), math.pi / 2, F32)
    phase_row = jnp.concatenate([jnp.zeros((2 * half,), F32), quarter, -quarter]).reshape(1, LANE)
    q_scale = (MLA_NOPE + MLA_ROPE) ** -0.5 * math.log2(math.e)
    return pl.pallas_call(
        functools.partial(_mla_in_kernel, q_scale=q_scale),
        grid=(b, s // ROW_TILE),
        in_specs=[
            _row_spec(d), _const_spec((1, d)), _mod_spec(layer, 0), _mod_spec(layer, 1),
            pl.BlockSpec((None, 1, ROW_TILE), lambda bi, i: (bi, 0, i)),
            _const_spec((1, LANE)), _const_spec((1, LANE)),
            _const_spec((d, win_w)), _const_spec((1, MLA_Q_RANK)), _const_spec((MLA_Q_RANK, hq)),
            _const_spec((1, MLA_KV_RANK)), _const_spec((MLA_KV_RANK, hn)),
            _const_spec((hv, MLA_KV_RANK)),
        ],
        out_specs=[_row_spec(hq), _row_spec(hq),
                   pl.BlockSpec((None, MLA_HEADS, MLA_VT_ROWS, ROW_TILE), lambda bi, i: (bi, 0, 0, i))],
        out_shape=[jax.ShapeDtypeStruct((b, s, hq), BF16), jax.ShapeDtypeStruct((b, s, hq), BF16),
                   jax.ShapeDtypeStruct((b, MLA_HEADS, MLA_VT_ROWS, s), BF16)],
        compiler_params=_cparams(("arbitrary", "arbitrary")),
        name="mla_in",
    )(x, gain.reshape(1, d), mod_all, mod_all, positions.reshape(b, 1, s), freq_row, phase_row,
      win, g_q.reshape(1, -1), wq, g_kv.reshape(1, -1), wkn, wvt)


def _attn_kernel(q_ref, k_ref, vt_ref, o_ref, s0_ref, s1_ref, mb0_ref, mb1_ref, m_ref, acc_ref):
    tq, tk = ATT_Q_BLOCK, ATT_KV_BLOCK
    per_q = tq // tk
    assert per_q == 2
    n_q = q_ref.shape[0] // tq
    n_strips = tq // ATT_STRIP
    all_strips = tuple(range(n_strips))

    def lanes(c):
        return slice(c * ATT_STRIP, (c + 1) * ATT_STRIP)

    def scores(qi, kj, s_ref, mb_ref, diag_offset=None, strips=all_strips):
        start = pl.multiple_of(kj * tk, tk)
        k_blk = k_ref[pl.ds(start, tk), :]
        for c in strips:
            q_lo = c * ATT_STRIP
            if diag_offset is not None and diag_offset > q_lo + ATT_STRIP - 1:
                s_ref[c] = jnp.full((tk, ATT_STRIP), -jnp.inf, F32)
                mb_ref[:, lanes(c)] = jnp.full((1, ATT_STRIP), -jnp.inf, F32)
                continue
            q_strip = q_ref[pl.ds(pl.multiple_of(qi * tq + q_lo, ATT_STRIP), ATT_STRIP), :]
            s_t = _dot_nt(k_blk, q_strip)
            if diag_offset is not None and diag_offset + tk - 1 > q_lo:
                key = lax.broadcasted_iota(jnp.int32, (tk, ATT_STRIP), 0) + diag_offset
                qry = lax.broadcasted_iota(jnp.int32, (tk, ATT_STRIP), 1) + q_lo
                s_t = jnp.where(key <= qry, s_t, -jnp.inf)
            s_ref[c] = s_t
            mb_ref[:, lanes(c)] = jnp.max(s_t, axis=0, keepdims=True)

    def accumulate(kj, s_ref, mb_ref, strips=all_strips):
        start = pl.multiple_of(kj * tk, tk)
        vt_blk = vt_ref[:, pl.ds(start, tk)]
        for c in strips:
            cols = lanes(c)
            m_prev = m_ref[:, cols]
            m_new = jnp.maximum(m_prev, mb_ref[:, cols])
            p_t = jnp.exp2((s_ref[c] - m_new).astype(BF16))
            alpha = jnp.exp2(m_prev - m_new)
            acc_ref[:, cols] = alpha * acc_ref[:, cols] + _dot(vt_blk, p_t)
            m_ref[:, cols] = m_new

    def reset():
        m_ref[...] = jnp.full_like(m_ref, -jnp.inf)
        acc_ref[...] = jnp.zeros_like(acc_ref)

    def finish(qi, pending):
        accumulate(pending, s1_ref, mb1_ref)
        acc = acc_ref[...]
        o_t = acc[:MLA_VDIM, :] / acc[MLA_VDIM:MLA_VDIM + 1, :]
        o_ref[pl.ds(pl.multiple_of(qi * tq, tq), tq), :] = o_t.T.astype(BF16)

    def run_block(qi):
        first = per_q * qi
        reset()
        scores(qi, first + 1, s1_ref, mb1_ref, diag_offset=tk)
        accumulate(first, s0_ref, mb0_ref)

        def pair(p, pending):
            for c in all_strips:
                accumulate(pending, s1_ref, mb1_ref, strips=(c,))
                scores(qi, 2 * p, s0_ref, mb0_ref, strips=(c,))
            for c in all_strips:
                accumulate(2 * p, s0_ref, mb0_ref, strips=(c,))
                scores(qi, 2 * p + 1, s1_ref, mb1_ref, strips=(c,))
            return 2 * p + 1

        def quad(u, pending):
            return pair(2 * u + 1, pair(2 * u, pending))

        pending = lax.fori_loop(0, qi // 2, quad, first + 1)
        return lax.fori_loop(0, qi % 2, lambda _, pend: pair(qi - 1, pend), pending)

    scores(0, 0, s0_ref, mb0_ref, diag_offset=0)
    pending0 = run_block(0)

    def q_block(qi, pending):
        scores(qi, per_q * qi, s0_ref, mb0_ref, diag_offset=0)
        finish(qi - 1, pending)
        return run_block(qi)

    pending_last = lax.fori_loop(1, n_q, q_block, pending0)
    finish(n_q - 1, pending_last)


def _attention(q, kc, vt):
    b, _, _, s = vt.shape
    tq, tk = ATT_Q_BLOCK, ATT_KV_BLOCK
    return pl.pallas_call(
        _attn_kernel,
        grid=(b, MLA_HEADS),
        in_specs=[
            pl.BlockSpec((None, s, MLA_QK_PAD), lambda bi, h: (bi, 0, h)),
            pl.BlockSpec((None, s, MLA_QK_PAD), lambda bi, h: (bi, 0, h)),
            pl.BlockSpec((None, None, MLA_VT_ROWS, s), lambda bi, h: (bi, h, 0, 0)),
        ],
        out_specs=pl.BlockSpec((None, s, MLA_VDIM), lambda bi, h: (bi, 0, h)),
        out_shape=jax.ShapeDtypeStruct((b, s, MLA_HEADS * MLA_VDIM), BF16),
        scratch_shapes=[pltpu.VMEM((tq // ATT_STRIP, tk, ATT_STRIP), F32),
                        pltpu.VMEM((tq // ATT_STRIP, tk, ATT_STRIP), F32),
                        pltpu.VMEM((1, tq), F32), pltpu.VMEM((1, tq), F32),
                        pltpu.VMEM((1, tq), F32), pltpu.VMEM((MLA_VT_ROWS, tq), F32)],
        compiler_params=_cparams(("arbitrary", "arbitrary")),
        name="mla_attention",
    )(q, kc, vt)


def kernel(x, c, positions, ada_w, ada_b, norm_mix, norm_ffn, gla_w_in, gla_w_gate, gla_b_gate,
           gla_g_out, gla_w_out, mla_w_in, mla_g_q, mla_w_q_up, mla_g_kv, mla_w_kv_up, mla_w_out,
           ffn_w_in, ffn_w_out, final_norm):
    batch = x.shape[0]
    depth = ada_w.shape[0]
    assert batch <= SUBLANE and ada_w.shape[-1] == N_MOD * D_MODEL
    c_pad = jnp.pad(c, ((0, SUBLANE - batch), (0, 0)))
    mod_all = _adaln(c_pad, ada_w, ada_b)

    for i in range(depth):
        j = i // 2
        if i % 2 == 0:
            q, k, v, r, la = _gla_in(x, mod_all, i, norm_mix[i], gla_w_in, j, gla_w_gate,
                                     gla_b_gate[j])
            mixed = _gla(q, k, v, r, la, gla_g_out[j])
            w_o = gla_w_out
        else:
            q, kc, vt = _mla_in(x, mod_all, i, norm_mix[i], positions, mla_w_in[j], mla_g_q[j],
                                mla_w_q_up[j], mla_g_kv[j], mla_w_kv_up[j])
            mixed = _attention(q, kc, vt)
            w_o = mla_w_out
        x, act = _mix_out_ffn_in(mixed, w_o, j, x, mod_all, i, norm_ffn[i], ffn_w_in)
        last = i == depth - 1
        x = _out_res(act, ffn_w_out, i, x, mod_all, i, 5, final_gain=final_norm if last else None)
    return x
```

```python
import functools
import math

import jax
import jax.numpy as jnp
from jax import lax
from jax.experimental import pallas as pl
from jax.experimental.pallas import tpu as pltpu

F32 = jnp.float32
BF16 = jnp.bfloat16

D_MODEL = 1024
EPS = 1e-6

GLA_HEADS = 4
GLA_DK = 128
GLA_DV = 256
GLA_GATE_RANK = 16
GLA_TAU = 16.0
GLA_CHUNK = 64
GLA_SUB = 16

MLA_HEADS = 8
MLA_NOPE = 128
MLA_ROPE = 64
MLA_VDIM = 128
MLA_Q_RANK = 384
MLA_KV_RANK = 256
ROPE_THETA = 10000.0
MLA_QK_PAD = 256
MLA_VT_ROWS = MLA_VDIM + 16

D_FF = 2816
N_MOD = 6

LANE = 128
SUBLANE = 8
ROW_TILE = 512
IN_TILE = 1024
GLA_TIME_BLOCK = 2048
ATT_Q_BLOCK = 1024
ATT_KV_BLOCK = 512
ATT_STRIP = 256
FFN_COL_CHUNK = 256
VMEM_LIMIT = 56 * 1024 * 1024


def _cparams(sem):
    return pltpu.CompilerParams(dimension_semantics=sem, vmem_limit_bytes=VMEM_LIMIT)


def _dot(a, b):
    return jnp.dot(a, b, preferred_element_type=F32)


def _dot_nt(a, b):
    return lax.dot_general(a, b, (((1,), (1,)), ((), ())), preferred_element_type=F32)


def _sigmoid(x):
    return 1.0 / (1.0 + jnp.exp(-x))


def _rms(x, g):
    return x * lax.rsqrt(jnp.mean(x * x, axis=-1, keepdims=True) + EPS) * g


def _norm_mod(x, g, shift, scale):
    return _rms(x, g) * (1.0 + scale) + shift


def _adaln_kernel(c_ref, w_ref, b_ref, o_ref):
    c = c_ref[...]
    o_ref[...] = _dot(c * _sigmoid(c), w_ref[...]) + b_ref[...]


def _adaln(c_pad, ada_w, ada_b):
    depth, d, n = ada_w.shape
    tn = 1536
    return pl.pallas_call(
        _adaln_kernel,
        grid=(depth, n // tn),
        in_specs=[
            pl.BlockSpec((SUBLANE, d), lambda l, j: (0, 0)),
            pl.BlockSpec((None, d, tn), lambda l, j: (l, 0, j)),
            pl.BlockSpec((None, 1, tn), lambda l, j: (l, 0, j)),
        ],
        out_specs=pl.BlockSpec((None, SUBLANE, tn), lambda l, j: (l, 0, j)),
        out_shape=jax.ShapeDtypeStruct((depth, SUBLANE, n), F32),
        compiler_params=_cparams(("arbitrary", "arbitrary")),
        name="adaln",
    )(c_pad, ada_w, ada_b.reshape(depth, 1, n))


def _row_spec(width, tile=ROW_TILE):
    return pl.BlockSpec((None, tile, width), lambda b, i: (b, i, 0))


def _mod_spec(layer, k):
    return pl.BlockSpec((None, SUBLANE, D_MODEL), lambda b, i: (layer, 0, k))


def _mod_row(ref):
    return ref[pl.ds(pl.program_id(0), 1), :]


def _const_spec(shape):
    return pl.BlockSpec(shape, lambda b, i: (0,) * len(shape))


def _layer_spec(shape, layer, col_block=0):
    return pl.BlockSpec((None,) + shape, lambda b, i: (layer, 0, col_block),
                        pipeline_mode=pl.Buffered(1))


def _gla_in_kernel(x_ref, g_ref, sh_ref, sc_ref, w_ref, wgate_ref, bgate_ref,
                   q_ref, k_ref, v_ref, r_ref, la_ref):
    hk = GLA_HEADS * GLA_DK
    hv = GLA_HEADS * GLA_DV
    h = _norm_mod(x_ref[...], g_ref[...], _mod_row(sh_ref), _mod_row(sc_ref)).astype(BF16)

    def proj(lo, width):
        return _dot_nt(h, w_ref[lo:lo + width, :].astype(BF16))

    q_ref[...] = (proj(0, hk) * (GLA_DK ** -0.5)).astype(BF16)
    k_ref[...] = proj(hk, hk).astype(BF16)
    v_ref[...] = proj(2 * hk, hv).astype(BF16)
    r_ref[...] = proj(2 * hk + hv, hv).astype(BF16)
    low_rank = proj(2 * hk + 2 * hv, GLA_GATE_RANK).astype(BF16)
    z = _dot(low_rank, wgate_ref[...].astype(BF16)) + bgate_ref[...]
    log_sig = jnp.minimum(z, 0.0) - jnp.log(1.0 + jnp.exp(-jnp.abs(z)))
    la_ref[...] = log_sig * (math.log2(math.e) / GLA_TAU)


def _gla_in(x, mod_all, layer, gain, w_in, j, w_gate, b_gate):
    b, s, d = x.shape
    hk = GLA_HEADS * GLA_DK
    hv = GLA_HEADS * GLA_DV
    return pl.pallas_call(
        _gla_in_kernel,
        grid=(b, s // IN_TILE),
        in_specs=[
            _row_spec(d, IN_TILE), _const_spec((1, d)), _mod_spec(layer, 0), _mod_spec(layer, 1),
            _layer_spec((w_in.shape[-1], d), j), _layer_spec((GLA_GATE_RANK, hk), j),
            _const_spec((1, hk)),
        ],
        out_specs=[_row_spec(hk, IN_TILE), _row_spec(hk, IN_TILE), _row_spec(hv, IN_TILE),
                   _row_spec(hv, IN_TILE), _row_spec(hk, IN_TILE)],
        out_shape=[
            jax.ShapeDtypeStruct((b, s, hk), BF16), jax.ShapeDtypeStruct((b, s, hk), BF16),
            jax.ShapeDtypeStruct((b, s, hv), BF16), jax.ShapeDtypeStruct((b, s, hv), BF16),
            jax.ShapeDtypeStruct((b, s, hk), F32),
        ],
        compiler_params=_cparams(("arbitrary", "arbitrary")),
        name="gla_in",
    )(x, gain.reshape(1, d), mod_all, mod_all, jnp.swapaxes(w_in, 1, 2), w_gate,
      b_gate.reshape(1, hk))


def _gla_kernel(q_ref, k_ref, v_ref, r_ref, la_ref, go_ref, o_ref, state_ref):
    c, sub = GLA_CHUNK, GLA_SUB
    nsub = c // sub
    assert nsub * c == 2 * LANE

    @pl.when(pl.program_id(2) == 0)
    def _():
        state_ref[...] = jnp.zeros_like(state_ref)

    row = lax.broadcasted_iota(jnp.int32, (c, c), 0)
    col = lax.broadcasted_iota(jnp.int32, (c, c), 1)
    tri = jnp.where(row >= col, 1.0, 0.0).astype(BF16)
    qrow = lax.broadcasted_iota(jnp.int32, (c, LANE), 0)
    lane = lax.broadcasted_iota(jnp.int32, (c, LANE), 1)
    causal = (lane % c) <= qrow
    keep_lo = causal & ((lane // c) == (qrow // sub))
    keep_hi = causal & ((lane // c) + LANE // c == (qrow // sub))

    n_chunks = GLA_TIME_BLOCK // c
    cs = [slice(ci * c, (ci + 1) * c) for ci in range(n_chunks)]

    bs = []
    for sl in cs:
        la = la_ref[sl, :]
        la_hi = la.astype(BF16)
        la_lo = (la - la_hi.astype(F32)).astype(BF16)
        bs.append(_dot(tri, la_hi) + _dot(tri, la_lo))

    qes, attns, kd_ts, decays = [], [], [], []
    for sl, b in zip(cs, bs):
        q = q_ref[sl, :].astype(F32)
        k = k_ref[sl, :].astype(F32)
        qes.append((q * jnp.exp2(b)).astype(BF16))
        q_parts, k_parts = [], []
        for i in range(nsub):
            rows = slice(i * sub, (i + 1) * sub)
            live = (i + 1) * sub
            if i == 0:
                q_parts.append(q[rows] * jnp.exp2(b[rows]))
                k_var = k[:live] * jnp.exp2(-b[:live])
            else:
                anchor = b[i * sub - 1:i * sub, :]
                q_parts.append(q[rows] * jnp.exp2(b[rows] - anchor))
                k_var = k[:live] * jnp.exp2(anchor - b[:live])
            k_parts.append(k_var)
            if live < c:
                k_parts.append(jnp.zeros((c - live, GLA_DK), F32))
        q_anch = jnp.concatenate(q_parts, axis=0).astype(BF16)
        k_stack = jnp.concatenate(k_parts, axis=0).astype(BF16)
        prod = _dot_nt(q_anch, k_stack)
        attns.append(jnp.where(keep_lo, prod[:, :LANE],
                               jnp.where(keep_hi, prod[:, LANE:], 0.0)).astype(BF16))
        b_last = b[c - 1:c, :]
        kd_ts.append((k * jnp.exp2(b_last - b)).T.astype(BF16))
        decays.append(jnp.broadcast_to(jnp.exp2(b_last), (LANE, GLA_DK)).T)

    state = state_ref[...]
    gain = go_ref[...]
    for ci, sl in enumerate(cs):
        v = v_ref[sl, :]
        v2 = jnp.concatenate([v, v], axis=0)
        o = _dot(qes[ci], state.astype(BF16)) + _dot(attns[ci], v2)
        decay = jnp.concatenate([decays[ci]] * (GLA_DV // LANE), axis=1)
        state = state * decay + _dot(kd_ts[ci], v)
        half_r = 0.5 * r_ref[sl, :].astype(F32)
        silu_r = half_r + half_r * jnp.tanh(half_r)
        o_ref[sl, :] = (_rms(o, gain) * silu_r).astype(BF16)
    state_ref[...] = state


def _gla(q, k, v, r, la, g_out):
    b, s, _ = q.shape
    t = GLA_TIME_BLOCK

    def spec(width):
        return pl.BlockSpec((None, t, width), lambda bi, h, ti: (bi, ti, h))

    return pl.pallas_call(
        _gla_kernel,
        grid=(b, GLA_HEADS, s // t),
        in_specs=[spec(GLA_DK), spec(GLA_DK), spec(GLA_DV), spec(GLA_DV), spec(GLA_DK),
                  pl.BlockSpec((1, GLA_DV), lambda bi, h, ti: (0, 0))],
        out_specs=spec(GLA_DV),
        out_shape=jax.ShapeDtypeStruct((b, s, GLA_HEADS * GLA_DV), BF16),
        scratch_shapes=[pltpu.VMEM((GLA_DK, GLA_DV), F32)],
        compiler_params=_cparams(("arbitrary", "arbitrary", "arbitrary")),
        name="gla_scan",
    )(q, k, v, r, la, g_out.reshape(1, GLA_DV))


def _out_res_kernel(a_ref, w_ref, x_ref, gate_ref, *rest, final_norm):
    o_ref = rest[-1]
    y = x_ref[...] + _mod_row(gate_ref) * _dot(a_ref[...], w_ref[...].astype(BF16))
    if final_norm:
        y = _rms(y, rest[0][...])
    o_ref[...] = y


def _out_res(a, w, j, x, mod_all, layer, gate_idx, final_gain=None):
    b, s, d = x.shape
    kdim = a.shape[-1]
    in_specs = [_row_spec(kdim, IN_TILE), _layer_spec((kdim, d), j), _row_spec(d, IN_TILE),
                _mod_spec(layer, gate_idx)]
    args = [a, w, x, mod_all]
    if final_gain is not None:
        in_specs.append(_const_spec((1, d)))
        args.append(final_gain.reshape(1, d))
    return pl.pallas_call(
        functools.partial(_out_res_kernel, final_norm=final_gain is not None),
        grid=(b, s // IN_TILE),
        in_specs=in_specs,
        out_specs=_row_spec(d, IN_TILE),
        out_shape=jax.ShapeDtypeStruct((b, s, d), F32),
        compiler_params=_cparams(("arbitrary", "arbitrary")),
        name="out_res",
    )(*args)


def _mix_out_ffn_in_kernel(m_ref, wo_ref, x_ref, g1_ref, g_ref, sh_ref, sc_ref, wg_ref, wu_ref,
                           x1_ref, a_ref):
    x1 = x_ref[...] + _mod_row(g1_ref) * _dot(m_ref[...], wo_ref[...].astype(BF16))
    x1_ref[...] = x1
    h = _norm_mod(x1, g_ref[...], _mod_row(sh_ref), _mod_row(sc_ref)).astype(BF16)
    for j in range(0, D_FF, FFN_COL_CHUNK):
        cols = slice(j, j + FFN_COL_CHUNK)
        gate = _dot(h, wg_ref[:, cols].astype(BF16))
        up = _dot(h, wu_ref[:, cols].astype(BF16))
        a_ref[:, cols] = (gate * _sigmoid(gate) * up).astype(BF16)


def _mix_out_ffn_in(mixed, w_o, j, x, mod_all, layer, gain, w_in):
    b, s, d = x.shape
    kdim = mixed.shape[-1]
    return pl.pallas_call(
        _mix_out_ffn_in_kernel,
        grid=(b, s // ROW_TILE),
        in_specs=[_row_spec(kdim), _layer_spec((kdim, d), j), _row_spec(d), _mod_spec(layer, 2),
                  _const_spec((1, d)), _mod_spec(layer, 3), _mod_spec(layer, 4),
                  _layer_spec((d, D_FF), layer, 0), _layer_spec((d, D_FF), layer, 1)],
        out_specs=[_row_spec(d), _row_spec(D_FF)],
        out_shape=[jax.ShapeDtypeStruct((b, s, d), F32), jax.ShapeDtypeStruct((b, s, D_FF), BF16)],
        compiler_params=_cparams(("arbitrary", "arbitrary")),
        name="mix_out_ffn_in",
    )(mixed, w_o, x, mod_all, gain.reshape(1, d), mod_all, mod_all, w_in, w_in)


def _mla_in_kernel(x_ref, g_ref, sh_ref, sc_ref, pos_ref, freq_ref, phase_ref, win_ref, gq_ref,
                   wq_ref, gkv_ref, wkn_ref, wvt_ref, q_ref, kc_ref, vt_ref, *, q_scale):
    h = _norm_mod(x_ref[...], g_ref[...], _mod_row(sh_ref), _mod_row(sc_ref)).astype(BF16)
    proj = _dot(h, win_ref[...])
    cq = _rms(proj[:, :MLA_Q_RANK], gq_ref[...]).astype(BF16)
    ckv = _rms(proj[:, MLA_Q_RANK:MLA_Q_RANK + MLA_KV_RANK], gkv_ref[...]).astype(BF16)

    pos = jnp.broadcast_to(pos_ref[...].astype(F32), (LANE, IN_TILE)).T
    table = jnp.cos(pos * freq_ref[...] + phase_ref[...])
    rope_lanes = lax.broadcasted_iota(jnp.int32, (IN_TILE, LANE), 1) < MLA_ROPE

    def rope(t):
        u = t * table
        return jnp.where(rope_lanes, u + pltpu.roll(u, MLA_ROPE, axis=1), 0.0)

    k_rope = rope(proj[:, MLA_Q_RANK + MLA_KV_RANK:]).astype(BF16)
    v_t = _dot_nt(wvt_ref[...], ckv)
    ones = jnp.ones((MLA_VT_ROWS - MLA_VDIM, v_t.shape[1]), BF16)
    k_nope = _dot(ckv, wkn_ref[...])
    q_all = _dot(cq, wq_ref[...])
    for hd in range(MLA_HEADS):
        lo = hd * MLA_QK_PAD
        mid = lo + MLA_NOPE
        hi = lo + MLA_QK_PAD
        q_ref[:, lo:mid] = (q_all[:, lo:mid] * q_scale).astype(BF16)
        q_ref[:, mid:hi] = (rope(q_all[:, mid:hi]) * q_scale).astype(BF16)
        kc_ref[:, lo:mid] = k_nope[:, hd * MLA_NOPE:(hd + 1) * MLA_NOPE].astype(BF16)
        kc_ref[:, mid:hi] = k_rope
        vt_ref[hd, :MLA_VDIM, :] = v_t[hd * MLA_VDIM:(hd + 1) * MLA_VDIM, :].astype(BF16)
        vt_ref[hd, MLA_VDIM:, :] = ones


def _half_split(w_rope):
    even, odd = w_rope[..., 0::2], w_rope[..., 1::2]
    return jnp.concatenate([even, odd, odd, even], axis=-1)


def _mla_in(x, mod_all, layer, gain, positions, w_in, g_q, w_q_up, g_kv, w_kv_up):
    b, s, d = x.shape
    hq = MLA_HEADS * MLA_QK_PAD
    hn = MLA_HEADS * MLA_NOPE
    hv = MLA_HEADS * MLA_VDIM
    lat = MLA_Q_RANK + MLA_KV_RANK
    half = MLA_ROPE // 2
    win = jnp.concatenate([w_in[:, :lat], _half_split(w_in[:, lat:])], axis=-1).astype(BF16)
    wq3 = w_q_up.reshape(MLA_Q_RANK, MLA_HEADS, MLA_NOPE + MLA_ROPE)
    wq = jnp.concatenate([wq3[..., :MLA_NOPE], _half_split(wq3[..., MLA_NOPE:])], axis=-1)
    wq = wq.reshape(MLA_Q_RANK, hq).astype(BF16)
    wkv3 = w_kv_up.reshape(MLA_KV_RANK, MLA_HEADS, MLA_NOPE + MLA_VDIM)
    wkn = wkv3[..., :MLA_NOPE].reshape(MLA_KV_RANK, hn).astype(BF16)
    wvt = wkv3[..., MLA_NOPE:].reshape(MLA_KV_RANK, hv).T.astype(BF16)
    win_w = win.shape[-1]
    inv_freq = ROPE_THETA ** (-jnp.arange(0, MLA_ROPE, 2, dtype=F32) / MLA_ROPE)
    freq_row = jnp.tile(inv_freq, LANE // half).reshape(1, LANE)
    quarter = jnp.full((half,), math.pi / 2, F32)
    phase_row = jnp.concatenate([jnp.zeros((2 * half,), F32), quarter, -quarter]).reshape(1, LANE)
    q_scale = (MLA_NOPE + MLA_ROPE) ** -0.5 * math.log2(math.e)
    return pl.pallas_call(
        functools.partial(_mla_in_kernel, q_scale=q_scale),
        grid=(b, s // IN_TILE),
        in_specs=[
            _row_spec(d, IN_TILE), _const_spec((1, d)), _mod_spec(layer, 0), _mod_spec(layer, 1),
            pl.BlockSpec((None, 1, IN_TILE), lambda bi, i: (bi, 0, i)),
            _const_spec((1, LANE)), _const_spec((1, LANE)),
            _const_spec((d, win_w)), _const_spec((1, MLA_Q_RANK)), _const_spec((MLA_Q_RANK, hq)),
            _const_spec((1, MLA_KV_RANK)), _const_spec((MLA_KV_RANK, hn)),
            _const_spec((hv, MLA_KV_RANK)),
        ],
        out_specs=[_row_spec(hq, IN_TILE), _row_spec(hq, IN_TILE),
                   pl.BlockSpec((None, MLA_HEADS, MLA_VT_ROWS, IN_TILE), lambda bi, i: (bi, 0, 0, i))],
        out_shape=[jax.ShapeDtypeStruct((b, s, hq), BF16), jax.ShapeDtypeStruct((b, s, hq), BF16),
                   jax.ShapeDtypeStruct((b, MLA_HEADS, MLA_VT_ROWS, s), BF16)],
        compiler_params=_cparams(("arbitrary", "arbitrary")),
        name="mla_in",
    )(x, gain.reshape(1, d), mod_all, mod_all, positions.reshape(b, 1, s), freq_row, phase_row,
      win, g_q.reshape(1, -1), wq, g_kv.reshape(1, -1), wkn, wvt)


def _attn_kernel(q_ref, k_ref, vt_ref, o_ref, s0_ref, s1_ref, mb0_ref, mb1_ref, m_ref, acc_ref):
    tq, tk = ATT_Q_BLOCK, ATT_KV_BLOCK
    per_q = tq // tk
    assert per_q == 2
    n_q = q_ref.shape[0] // tq
    n_strips = tq // ATT_STRIP
    all_strips = tuple(range(n_strips))

    def lanes(c):
        return slice(c * ATT_STRIP, (c + 1) * ATT_STRIP)

    def scores(qi, kj, s_ref, mb_ref, diag_offset=None, strips=all_strips):
        start = pl.multiple_of(kj * tk, tk)
        k_blk = k_ref[pl.ds(start, tk), :]
        for c in strips:
            q_lo = c * ATT_STRIP
            if diag_offset is not None and diag_offset > q_lo + ATT_STRIP - 1:
                s_ref[c] = jnp.full((tk, ATT_STRIP), -jnp.inf, F32)
                mb_ref[:, lanes(c)] = jnp.full((1, ATT_STRIP), -jnp.inf, F32)
                continue
            q_strip = q_ref[pl.ds(pl.multiple_of(qi * tq + q_lo, ATT_STRIP), ATT_STRIP), :]
            s_t = _dot_nt(k_blk, q_strip)
            if diag_offset is not None and diag_offset + tk - 1 > q_lo:
                key = lax.broadcasted_iota(jnp.int32, (tk, ATT_STRIP), 0) + diag_offset
                qry = lax.broadcasted_iota(jnp.int32, (tk, ATT_STRIP), 1) + q_lo
                s_t = jnp.where(key <= qry, s_t, -jnp.inf)
            s_ref[c] = s_t
            mb_ref[:, lanes(c)] = jnp.max(s_t, axis=0, keepdims=True)

    def accumulate(kj, s_ref, mb_ref, strips=all_strips):
        start = pl.multiple_of(kj * tk, tk)
        vt_blk = vt_ref[:, pl.ds(start, tk)]
        for c in strips:
            cols = lanes(c)
            m_prev = m_ref[:, cols]
            m_new = jnp.maximum(m_prev, mb_ref[:, cols])
            p_t = jnp.exp2((s_ref[c] - m_new).astype(BF16))
            alpha = jnp.exp2(m_prev - m_new)
            acc_ref[:, cols] = alpha * acc_ref[:, cols] + _dot(vt_blk, p_t)
            m_ref[:, cols] = m_new

    def reset():
        m_ref[...] = jnp.full_like(m_ref, -jnp.inf)
        acc_ref[...] = jnp.zeros_like(acc_ref)

    def finish(qi, pending):
        accumulate(pending, s1_ref, mb1_ref)
        acc = acc_ref[...]
        o_t = acc[:MLA_VDIM, :] / acc[MLA_VDIM:MLA_VDIM + 1, :]
        o_ref[pl.ds(pl.multiple_of(qi * tq, tq), tq), :] = o_t.T.astype(BF16)

    def run_block(qi):
        first = per_q * qi
        reset()
        scores(qi, first + 1, s1_ref, mb1_ref, diag_offset=tk)
        accumulate(first, s0_ref, mb0_ref)

        def pair(p, pending):
            for c in all_strips:
                accumulate(pending, s1_ref, mb1_ref, strips=(c,))
                scores(qi, 2 * p, s0_ref, mb0_ref, strips=(c,))
            for c in all_strips:
                accumulate(2 * p, s0_ref, mb0_ref, strips=(c,))
                scores(qi, 2 * p + 1, s1_ref, mb1_ref, strips=(c,))
            return 2 * p + 1

        def quad(u, pending):
            return pair(2 * u + 1, pair(2 * u, pending))

        pending = lax.fori_loop(0, qi // 2, quad, first + 1)
        return lax.fori_loop(0, qi % 2, lambda _, pend: pair(qi - 1, pend), pending)

    scores(0, 0, s0_ref, mb0_ref, diag_offset=0)
    pending0 = run_block(0)

    def q_block(qi, pending):
        scores(qi, per_q * qi, s0_ref, mb0_ref, diag_offset=0)
        finish(qi - 1, pending)
        return run_block(qi)

    pending_last = lax.fori_loop(1, n_q, q_block, pending0)
    finish(n_q - 1, pending_last)


def _attention(q, kc, vt):
    b, _, _, s = vt.shape
    tq, tk = ATT_Q_BLOCK, ATT_KV_BLOCK
    return pl.pallas_call(
        _attn_kernel,
        grid=(b, MLA_HEADS),
        in_specs=[
            pl.BlockSpec((None, s, MLA_QK_PAD), lambda bi, h: (bi, 0, h)),
            pl.BlockSpec((None, s, MLA_QK_PAD), lambda bi, h: (bi, 0, h)),
            pl.BlockSpec((None, None, MLA_VT_ROWS, s), lambda bi, h: (bi, h, 0, 0)),
        ],
        out_specs=pl.BlockSpec((None, s, MLA_VDIM), lambda bi, h: (bi, 0, h)),
        out_shape=jax.ShapeDtypeStruct((b, s, MLA_HEADS * MLA_VDIM), BF16),
        scratch_shapes=[pltpu.VMEM((tq // ATT_STRIP, tk, ATT_STRIP), F32),
                        pltpu.VMEM((tq // ATT_STRIP, tk, ATT_STRIP), F32),
                        pltpu.VMEM((1, tq), F32), pltpu.VMEM((1, tq), F32),
                        pltpu.VMEM((1, tq), F32), pltpu.VMEM((MLA_VT_ROWS, tq), F32)],
        compiler_params=_cparams(("arbitrary", "arbitrary")),
        name="mla_attention",
    )(q, kc, vt)


def kernel(x, c, positions, ada_w, ada_b, norm_mix, norm_ffn, gla_w_in, gla_w_gate, gla_b_gate,
           gla_g_out, gla_w_out, mla_w_in, mla_g_q, mla_w_q_up, mla_g_kv, mla_w_kv_up, mla_w_out,
           ffn_w_in, ffn_w_out, final_norm):
    batch = x.shape[0]
    depth = ada_w.shape[0]
    assert batch <= SUBLANE and ada_w.shape[-1] == N_MOD * D_MODEL
    c_pad = jnp.pad(c, ((0, SUBLANE - batch), (0, 0)))
    mod_all = _adaln(c_pad, ada_w, ada_b)

    for i in range(depth):
        j = i // 2
        if i % 2 == 0:
            q, k, v, r, la = _gla_in(x, mod_all, i, norm_mix[i], gla_w_in, j, gla_w_gate,
                                     gla_b_gate[j])
            mixed = _gla(q, k, v, r, la, gla_g_out[j])
            w_o = gla_w_out
        else:
            q, kc, vt = _mla_in(x, mod_all, i, norm_mix[i], positions, mla_w_in[j], mla_g_q[j],
                                mla_w_q_up[j], mla_g_kv[j], mla_w_kv_up[j])
            mixed = _attention(q, kc, vt)
            w_o = mla_w_out
        x, act = _mix_out_ffn_in(mixed, w_o, j, x, mod_all, i, norm_ffn[i], ffn_w_in)
        last = i == depth - 1
        x = _out_res(act, ffn_w_out, i, x, mod_all, i, 5, final_gain=final_norm if last else None)
    return x
```

```python
import functools
import math

import jax
import jax.numpy as jnp
from jax import lax
from jax.experimental import pallas as pl
from jax.experimental.pallas import tpu as pltpu

F32 = jnp.float32
BF16 = jnp.bfloat16

D_MODEL = 1024
EPS = 1e-6

GLA_HEADS = 4
GLA_DK = 128
GLA_DV = 256
GLA_GATE_RANK = 16
GLA_TAU = 16.0
GLA_CHUNK = 64
GLA_SUB = 16

MLA_HEADS = 8
MLA_NOPE = 128
MLA_ROPE = 64
MLA_VDIM = 128
MLA_Q_RANK = 384
MLA_KV_RANK = 256
ROPE_THETA = 10000.0
MLA_QK_PAD = 256
MLA_VT_ROWS = MLA_VDIM + 16

D_FF = 2816
N_MOD = 6

LANE = 128
SUBLANE = 8
ROW_TILE = 512
IN_TILE = 1024
GLA_TIME_BLOCK = 2048
ATT_Q_BLOCK = 1024
ATT_KV_BLOCK = 512
ATT_STRIP = 256
FFN_COL_CHUNK = 256
VMEM_LIMIT = 56 * 1024 * 1024


def _cparams(sem):
    return pltpu.CompilerParams(dimension_semantics=sem, vmem_limit_bytes=VMEM_LIMIT)


def _dot(a, b):
    return jnp.dot(a, b, preferred_element_type=F32)


def _dot_nt(a, b):
    return lax.dot_general(a, b, (((1,), (1,)), ((), ())), preferred_element_type=F32)


def _sigmoid(x):
    return 1.0 / (1.0 + jnp.exp(-x))


def _rms(x, g):
    return x * lax.rsqrt(jnp.mean(x * x, axis=-1, keepdims=True) + EPS) * g


def _norm_mod(x, g, shift, scale):
    return _rms(x, g) * (1.0 + scale) + shift


def _adaln_kernel(c_ref, w_ref, b_ref, o_ref):
    c = c_ref[...]
    o_ref[...] = _dot(c * _sigmoid(c), w_ref[...]) + b_ref[...]


def _adaln(c_pad, ada_w, ada_b):
    depth, d, n = ada_w.shape
    tn = 1536
    return pl.pallas_call(
        _adaln_kernel,
        grid=(depth, n // tn),
        in_specs=[
            pl.BlockSpec((SUBLANE, d), lambda l, j: (0, 0)),
            pl.BlockSpec((None, d, tn), lambda l, j: (l, 0, j)),
            pl.BlockSpec((None, 1, tn), lambda l, j: (l, 0, j)),
        ],
        out_specs=pl.BlockSpec((None, SUBLANE, tn), lambda l, j: (l, 0, j)),
        out_shape=jax.ShapeDtypeStruct((depth, SUBLANE, n), F32),
        compiler_params=_cparams(("arbitrary", "arbitrary")),
        name="adaln",
    )(c_pad, ada_w, ada_b.reshape(depth, 1, n))


def _row_spec(width, tile=ROW_TILE):
    return pl.BlockSpec((None, tile, width), lambda b, i: (b, i, 0))


def _mod_spec(layer, k):
    return pl.BlockSpec((None, SUBLANE, D_MODEL), lambda b, i: (layer, 0, k))


def _mod_row(ref):
    return ref[pl.ds(pl.program_id(0), 1), :]


def _const_spec(shape):
    return pl.BlockSpec(shape, lambda b, i: (0,) * len(shape))


def _layer_spec(shape, layer, col_block=0):
    return pl.BlockSpec((None,) + shape, lambda b, i: (layer, 0, col_block),
                        pipeline_mode=pl.Buffered(1))


def _gla_in_kernel(x_ref, g_ref, sh_ref, sc_ref, w_ref, wgate_ref, bgate_ref,
                   q_ref, k_ref, v_ref, r_ref, la_ref):
    hk = GLA_HEADS * GLA_DK
    hv = GLA_HEADS * GLA_DV
    h = _norm_mod(x_ref[...], g_ref[...], _mod_row(sh_ref), _mod_row(sc_ref)).astype(BF16)

    def proj(lo, width):
        return _dot_nt(h, w_ref[lo:lo + width, :].astype(BF16))

    q_ref[...] = (proj(0, hk) * (GLA_DK ** -0.5)).astype(BF16)
    k_ref[...] = proj(hk, hk).astype(BF16)
    v_ref[...] = proj(2 * hk, hv).astype(BF16)
    r_ref[...] = proj(2 * hk + hv, hv).astype(BF16)
    low_rank = proj(2 * hk + 2 * hv, GLA_GATE_RANK).astype(BF16)
    z = _dot(low_rank, wgate_ref[...].astype(BF16)) + bgate_ref[...]
    log_sig = jnp.minimum(z, 0.0) - jnp.log(1.0 + jnp.exp(-jnp.abs(z)))
    la_ref[...] = log_sig * (math.log2(math.e) / GLA_TAU)


def _gla_in(x, mod_all, layer, gain, w_in, j, w_gate, b_gate):
    b, s, d = x.shape
    hk = GLA_HEADS * GLA_DK
    hv = GLA_HEADS * GLA_DV
    return pl.pallas_call(
        _gla_in_kernel,
        grid=(b, s // IN_TILE),
        in_specs=[
            _row_spec(d, IN_TILE), _const_spec((1, d)), _mod_spec(layer, 0), _mod_spec(layer, 1),
            _layer_spec((w_in.shape[-1], d), j), _layer_spec((GLA_GATE_RANK, hk), j),
            _const_spec((1, hk)),
        ],
        out_specs=[_row_spec(hk, IN_TILE), _row_spec(hk, IN_TILE), _row_spec(hv, IN_TILE),
                   _row_spec(hv, IN_TILE), _row_spec(hk, IN_TILE)],
        out_shape=[
            jax.ShapeDtypeStruct((b, s, hk), BF16), jax.ShapeDtypeStruct((b, s, hk), BF16),
            jax.ShapeDtypeStruct((b, s, hv), BF16), jax.ShapeDtypeStruct((b, s, hv), BF16),
            jax.ShapeDtypeStruct((b, s, hk), F32),
        ],
        compiler_params=_cparams(("arbitrary", "arbitrary")),
        name="gla_in",
    )(x, gain.reshape(1, d), mod_all, mod_all, jnp.swapaxes(w_in, 1, 2), w_gate,
      b_gate.reshape(1, hk))


def _gla_kernel(q_ref, k_ref, v_ref, r_ref, la_ref, go_ref, o_ref, state_ref):
    c, sub = GLA_CHUNK, GLA_SUB
    nsub = c // sub
    assert nsub * c == 2 * LANE

    @pl.when(pl.program_id(2) == 0)
    def _():
        state_ref[...] = jnp.zeros_like(state_ref)

    row = lax.broadcasted_iota(jnp.int32, (c, c), 0)
    col = lax.broadcasted_iota(jnp.int32, (c, c), 1)
    tri = jnp.where(row >= col, 1.0, 0.0).astype(BF16)
    qrow = lax.broadcasted_iota(jnp.int32, (c, LANE), 0)
    lane = lax.broadcasted_iota(jnp.int32, (c, LANE), 1)
    causal = (lane % c) <= qrow
    keep_lo = causal & ((lane // c) == (qrow // sub))
    keep_hi = causal & ((lane // c) + LANE // c == (qrow // sub))

    n_chunks = GLA_TIME_BLOCK // c
    cs = [slice(ci * c, (ci + 1) * c) for ci in range(n_chunks)]

    bs = []
    for sl in cs:
        la = la_ref[sl, :]
        la_hi = la.astype(BF16)
        la_lo = (la - la_hi.astype(F32)).astype(BF16)
        bs.append(_dot(tri, la_hi) + _dot(tri, la_lo))

    qes, attns, kd_ts, decays = [], [], [], []
    for sl, b in zip(cs, bs):
        q = q_ref[sl, :].astype(F32)
        k = k_ref[sl, :].astype(F32)
        qes.append((q * jnp.exp2(b)).astype(BF16))
        q_parts, k_parts = [], []
        for i in range(nsub):
            rows = slice(i * sub, (i + 1) * sub)
            live = (i + 1) * sub
            if i == 0:
                q_parts.append(q[rows] * jnp.exp2(b[rows]))
                k_var = k[:live] * jnp.exp2(-b[:live])
            else:
                anchor = b[i * sub - 1:i * sub, :]
                q_parts.append(q[rows] * jnp.exp2(b[rows] - anchor))
                k_var = k[:live] * jnp.exp2(anchor - b[:live])
            k_parts.append(k_var)
            if live < c:
                k_parts.append(jnp.zeros((c - live, GLA_DK), F32))
        q_anch = jnp.concatenate(q_parts, axis=0).astype(BF16)
        k_stack = jnp.concatenate(k_parts, axis=0).astype(BF16)
        prod = _dot_nt(q_anch, k_stack)
        attns.append(jnp.where(keep_lo, prod[:, :LANE],
                               jnp.where(keep_hi, prod[:, LANE:], 0.0)).astype(BF16))
        b_last = b[c - 1:c, :]
        kd_ts.append((k * jnp.exp2(b_last - b)).T.astype(BF16))
        decays.append(jnp.broadcast_to(jnp.exp2(b_last), (LANE, GLA_DK)).T)

    state = state_ref[...]
    gain = go_ref[...]
    for ci, sl in enumerate(cs):
        v = v_ref[sl, :]
        v2 = jnp.concatenate([v, v], axis=0)
        o = _dot(qes[ci], state.astype(BF16)) + _dot(attns[ci], v2)
        decay = jnp.concatenate([decays[ci]] * (GLA_DV // LANE), axis=1)
        state = state * decay + _dot(kd_ts[ci], v)
        half_r = 0.5 * r_ref[sl, :].astype(F32)
        silu_r = half_r + half_r * jnp.tanh(half_r)
        o_ref[sl, :] = (_rms(o, gain) * silu_r).astype(BF16)
    state_ref[...] = state


def _gla(q, k, v, r, la, g_out):
    b, s, _ = q.shape
    t = GLA_TIME_BLOCK

    def spec(width):
        return pl.BlockSpec((None, t, width), lambda bi, h, ti: (bi, ti, h))

    return pl.pallas_call(
        _gla_kernel,
        grid=(b, GLA_HEADS, s // t),
        in_specs=[spec(GLA_DK), spec(GLA_DK), spec(GLA_DV), spec(GLA_DV), spec(GLA_DK),
                  pl.BlockSpec((1, GLA_DV), lambda bi, h, ti: (0, 0))],
        out_specs=spec(GLA_DV),
        out_shape=jax.ShapeDtypeStruct((b, s, GLA_HEADS * GLA_DV), BF16),
        scratch_shapes=[pltpu.VMEM((GLA_DK, GLA_DV), F32)],
        compiler_params=_cparams(("arbitrary", "arbitrary", "arbitrary")),
        name="gla_scan",
    )(q, k, v, r, la, g_out.reshape(1, GLA_DV))


def _out_res_kernel(a_ref, w_ref, x_ref, gate_ref, *rest, final_norm):
    o_ref = rest[-1]
    y = x_ref[...] + _mod_row(gate_ref) * _dot(a_ref[...], w_ref[...].astype(BF16))
    if final_norm:
        y = _rms(y, rest[0][...])
    o_ref[...] = y


def _out_res(a, w, j, x, mod_all, layer, gate_idx, final_gain=None):
    b, s, d = x.shape
    kdim = a.shape[-1]
    in_specs = [_row_spec(kdim, IN_TILE), _layer_spec((kdim, d), j), _row_spec(d, IN_TILE),
                _mod_spec(layer, gate_idx)]
    args = [a, w, x, mod_all]
    if final_gain is not None:
        in_specs.append(_const_spec((1, d)))
        args.append(final_gain.reshape(1, d))
    return pl.pallas_call(
        functools.partial(_out_res_kernel, final_norm=final_gain is not None),
        grid=(b, s // IN_TILE),
        in_specs=in_specs,
        out_specs=_row_spec(d, IN_TILE),
        out_shape=jax.ShapeDtypeStruct((b, s, d), F32),
        compiler_params=_cparams(("arbitrary", "arbitrary")),
        name="out_res",
    )(*args)


def _mix_out_ffn_in_kernel(m_ref, wo_ref, x_ref, g1_ref, g_ref, sh_ref, sc_ref, wg_ref, wu_ref,
                           x1_ref, a_ref):
    x1 = x_ref[...] + _mod_row(g1_ref) * _dot(m_ref[...], wo_ref[...].astype(BF16))
    x1_ref[...] = x1
    h = _norm_mod(x1, g_ref[...], _mod_row(sh_ref), _mod_row(sc_ref)).astype(BF16)
    for j in range(0, D_FF, FFN_COL_CHUNK):
        cols = slice(j, j + FFN_COL_CHUNK)
        gate = _dot(h, wg_ref[:, cols].astype(BF16))
        up = _dot(h, wu_ref[:, cols].astype(BF16))
        a_ref[:, cols] = (gate * _sigmoid(gate) * up).astype(BF16)


def _mix_out_ffn_in(mixed, w_o, j, x, mod_all, layer, gain, w_in):
    b, s, d = x.shape
    kdim = mixed.shape[-1]
    return pl.pallas_call(
        _mix_out_ffn_in_kernel,
        grid=(b, s // ROW_TILE),
        in_specs=[_row_spec(kdim), _layer_spec((kdim, d), j), _row_spec(d), _mod_spec(layer, 2),
                  _const_spec((1, d)), _mod_spec(layer, 3), _mod_spec(layer, 4),
                  _layer_spec((d, D_FF), layer, 0), _layer_spec((d, D_FF), layer, 1)],
        out_specs=[_row_spec(d), _row_spec(D_FF)],
        out_shape=[jax.ShapeDtypeStruct((b, s, d), F32), jax.ShapeDtypeStruct((b, s, D_FF), BF16)],
        compiler_params=_cparams(("arbitrary", "arbitrary")),
        name="mix_out_ffn_in",
    )(mixed, w_o, x, mod_all, gain.reshape(1, d), mod_all, mod_all, w_in, w_in)


def _mla_in_kernel(x_ref, g_ref, sh_ref, sc_ref, pos_ref, freq_ref, phase_ref, win_ref, gq_ref,
                   wq_ref, gkv_ref, wkn_ref, wvt_ref, q_ref, kc_ref, vt_ref, *, q_scale):
    h = _norm_mod(x_ref[...], g_ref[...], _mod_row(sh_ref), _mod_row(sc_ref)).astype(BF16)
    proj = _dot(h, win_ref[...])
    cq = _rms(proj[:, :MLA_Q_RANK], gq_ref[...]).astype(BF16)
    ckv = _rms(proj[:, MLA_Q_RANK:MLA_Q_RANK + MLA_KV_RANK], gkv_ref[...]).astype(BF16)

    pos = jnp.broadcast_to(pos_ref[...].astype(F32), (LANE, IN_TILE)).T
    table = jnp.cos(pos * freq_ref[...] + phase_ref[...])
    rope_lanes = lax.broadcasted_iota(jnp.int32, (IN_TILE, LANE), 1) < MLA_ROPE

    def rope(t):
        u = t * table
        return jnp.where(rope_lanes, u + pltpu.roll(u, MLA_ROPE, axis=1), 0.0)

    k_rope = rope(proj[:, MLA_Q_RANK + MLA_KV_RANK:]).astype(BF16)
    v_t = _dot_nt(wvt_ref[...], ckv)
    ones = jnp.ones((MLA_VT_ROWS - MLA_VDIM, v_t.shape[1]), BF16)
    k_nope = _dot(ckv, wkn_ref[...])
    q_all = _dot(cq, wq_ref[...])
    for hd in range(MLA_HEADS):
        lo = hd * MLA_QK_PAD
        mid = lo + MLA_NOPE
        hi = lo + MLA_QK_PAD
        q_ref[:, lo:mid] = (q_all[:, lo:mid] * q_scale).astype(BF16)
        q_ref[:, mid:hi] = (rope(q_all[:, mid:hi]) * q_scale).astype(BF16)
        kc_ref[:, lo:mid] = k_nope[:, hd * MLA_NOPE:(hd + 1) * MLA_NOPE].astype(BF16)
        kc_ref[:, mid:hi] = k_rope
        vt_ref[hd, :MLA_VDIM, :] = v_t[hd * MLA_VDIM:(hd + 1) * MLA_VDIM, :].astype(BF16)
        vt_ref[hd, MLA_VDIM:, :] = ones


def _half_split(w_rope):
    even, odd = w_rope[..., 0::2], w_rope[..., 1::2]
    return jnp.concatenate([even, odd, odd, even], axis=-1)


def _mla_in(x, mod_all, layer, gain, positions, w_in, g_q, w_q_up, g_kv, w_kv_up):
    b, s, d = x.shape
    hq = MLA_HEADS * MLA_QK_PAD
    hn = MLA_HEADS * MLA_NOPE
    hv = MLA_HEADS * MLA_VDIM
    lat = MLA_Q_RANK + MLA_KV_RANK
    half = MLA_ROPE // 2
    win = jnp.concatenate([w_in[:, :lat], _half_split(w_in[:, lat:])], axis=-1).astype(BF16)
    wq3 = w_q_up.reshape(MLA_Q_RANK, MLA_HEADS, MLA_NOPE + MLA_ROPE)
    wq = jnp.concatenate([wq3[..., :MLA_NOPE], _half_split(wq3[..., MLA_NOPE:])], axis=-1)
    wq = wq.reshape(MLA_Q_RANK, hq).astype(BF16)
    wkv3 = w_kv_up.reshape(MLA_KV_RANK, MLA_HEADS, MLA_NOPE + MLA_VDIM)
    wkn = wkv3[..., :MLA_NOPE].reshape(MLA_KV_RANK, hn).astype(BF16)
    wvt = wkv3[..., MLA_NOPE:].reshape(MLA_KV_RANK, hv).T.astype(BF16)
    win_w = win.shape[-1]
    inv_freq = ROPE_THETA ** (-jnp.arange(0, MLA_ROPE, 2, dtype=F32) / MLA_ROPE)
    freq_row = jnp.tile(inv_freq, LANE // half).reshape(1, LANE)
    quarter = jnp.full((half,), math.pi / 2, F32)
    phase_row = jnp.concatenate([jnp.zeros((2 * half,), F32), quarter, -quarter]).reshape(1, LANE)
    q_scale = (MLA_NOPE + MLA_ROPE) ** -0.5 * math.log2(math.e)
    return pl.pallas_call(
        functools.partial(_mla_in_kernel, q_scale=q_scale),
        grid=(b, s // IN_TILE),
        in_specs=[
            _row_spec(d, IN_TILE), _const_spec((1, d)), _mod_spec(layer, 0), _mod_spec(layer, 1),
            pl.BlockSpec((None, 1, IN_TILE), lambda bi, i: (bi, 0, i)),
            _const_spec((1, LANE)), _const_spec((1, LANE)),
            _const_spec((d, win_w)), _const_spec((1, MLA_Q_RANK)), _const_spec((MLA_Q_RANK, hq)),
            _const_spec((1, MLA_KV_RANK)), _const_spec((MLA_KV_RANK, hn)),
            _const_spec((hv, MLA_KV_RANK)),
        ],
        out_specs=[_row_spec(hq, IN_TILE), _row_spec(hq, IN_TILE),
                   pl.BlockSpec((None, MLA_HEADS, MLA_VT_ROWS, IN_TILE), lambda bi, i: (bi, 0, 0, i))],
        out_shape=[jax.ShapeDtypeStruct((b, s, hq), BF16), jax.ShapeDtypeStruct((b, s, hq), BF16),
                   jax.ShapeDtypeStruct((b, MLA_HEADS, MLA_VT_ROWS, s), BF16)],
        compiler_params=_cparams(("arbitrary", "arbitrary")),
        name="mla_in",
    )(x, gain.reshape(1, d), mod_all, mod_all, positions.reshape(b, 1, s), freq_row, phase_row,
      win, g_q.reshape(1, -1), wq, g_kv.reshape(1, -1), wkn, wvt)


def _attn_kernel(q_ref, k_ref, vt_ref, o_ref, s0_ref, s1_ref, mb0_ref, mb1_ref, m_ref, acc_ref):
    tq, tk = ATT_Q_BLOCK, ATT_KV_BLOCK
    per_q = tq // tk
    assert per_q == 2
    n_q = q_ref.shape[0] // tq
    n_strips = tq // ATT_STRIP
    all_strips = tuple(range(n_strips))

    def lanes(c):
        return slice(c * ATT_STRIP, (c + 1) * ATT_STRIP)

    def scores(qi, kj, s_ref, mb_ref, diag_offset=None, strips=all_strips):
        start = pl.multiple_of(kj * tk, tk)
        k_blk = k_ref[pl.ds(start, tk), :]
        for c in strips:
            q_lo = c * ATT_STRIP
            if diag_offset is not None and diag_offset > q_lo + ATT_STRIP - 1:
                s_ref[c] = jnp.full((tk, ATT_STRIP), -jnp.inf, F32)
                mb_ref[:, lanes(c)] = jnp.full((1, ATT_STRIP), -jnp.inf, F32)
                continue
            q_strip = q_ref[pl.ds(pl.multiple_of(qi * tq + q_lo, ATT_STRIP), ATT_STRIP), :]
            s_t = _dot_nt(k_blk, q_strip)
            if diag_offset is not None and diag_offset + tk - 1 > q_lo:
                key = lax.broadcasted_iota(jnp.int32, (tk, ATT_STRIP), 0) + diag_offset
                qry = lax.broadcasted_iota(jnp.int32, (tk, ATT_STRIP), 1) + q_lo
                s_t = jnp.where(key <= qry, s_t, -jnp.inf)
            s_ref[c] = s_t
            mb_ref[:, lanes(c)] = jnp.max(s_t, axis=0, keepdims=True)

    def accumulate(kj, s_ref, mb_ref, strips=all_strips):
        start = pl.multiple_of(kj * tk, tk)
        vt_blk = vt_ref[:, pl.ds(start, tk)]
        for c in strips:
            cols = lanes(c)
            m_prev = m_ref[:, cols]
            m_new = jnp.maximum(m_prev, mb_ref[:, cols])
            p_t = jnp.exp2((s_ref[c] - m_new).astype(BF16))
            alpha = jnp.exp2(m_prev - m_new)
            acc_ref[:, cols] = alpha * acc_ref[:, cols] + _dot(vt_blk, p_t)
            m_ref[:, cols] = m_new

    def reset():
        m_ref[...] = jnp.full_like(m_ref, -jnp.inf)
        acc_ref[...] = jnp.zeros_like(acc_ref)

    def finish(qi, pending):
        accumulate(pending, s1_ref, mb1_ref)
        acc = acc_ref[...]
        o_t = acc[:MLA_VDIM, :] / acc[MLA_VDIM:MLA_VDIM + 1, :]
        o_ref[pl.ds(pl.multiple_of(qi * tq, tq), tq), :] = o_t.T.astype(BF16)

    def run_block(qi):
        first = per_q * qi
        reset()
        scores(qi, first + 1, s1_ref, mb1_ref, diag_offset=tk)
        accumulate(first, s0_ref, mb0_ref)
        if isinstance(qi, int) and qi == 0:
            return first + 1

        for c in all_strips:
            if tk <= c * ATT_STRIP + ATT_STRIP - 1:
                accumulate(first + 1, s1_ref, mb1_ref, strips=(c,))
            scores(qi, 0, s0_ref, mb0_ref, strips=(c,))

        def pair(p, pending):
            for c in all_strips:
                accumulate(pending, s0_ref, mb0_ref, strips=(c,))
                scores(qi, 2 * p + 1, s1_ref, mb1_ref, strips=(c,))
            for c in all_strips:
                accumulate(2 * p + 1, s1_ref, mb1_ref, strips=(c,))
                scores(qi, 2 * p + 2, s0_ref, mb0_ref, strips=(c,))
            return 2 * p + 2

        def quad(u, pending):
            return pair(2 * u + 1, pair(2 * u, pending))

        n_pairs = qi - 1
        pending = lax.fori_loop(0, n_pairs // 2, quad, 0)
        pending = lax.fori_loop(0, n_pairs % 2, lambda _, pend: pair(n_pairs - 1, pend), pending)
        last = 2 * qi - 1
        for c in all_strips:
            accumulate(pending, s0_ref, mb0_ref, strips=(c,))
            scores(qi, last, s1_ref, mb1_ref, strips=(c,))
        return last

    scores(0, 0, s0_ref, mb0_ref, diag_offset=0)
    pending0 = run_block(0)

    def q_block(qi, pending):
        scores(qi, per_q * qi, s0_ref, mb0_ref, diag_offset=0)
        finish(qi - 1, pending)
        return run_block(qi)

    pending_last = lax.fori_loop(1, n_q, q_block, pending0)
    finish(n_q - 1, pending_last)


def _attention(q, kc, vt):
    b, _, _, s = vt.shape
    tq, tk = ATT_Q_BLOCK, ATT_KV_BLOCK
    return pl.pallas_call(
        _attn_kernel,
        grid=(b, MLA_HEADS),
        in_specs=[
            pl.BlockSpec((None, s, MLA_QK_PAD), lambda bi, h: (bi, 0, h)),
            pl.BlockSpec((None, s, MLA_QK_PAD), lambda bi, h: (bi, 0, h)),
            pl.BlockSpec((None, None, MLA_VT_ROWS, s), lambda bi, h: (bi, h, 0, 0)),
        ],
        out_specs=pl.BlockSpec((None, s, MLA_VDIM), lambda bi, h: (bi, 0, h)),
        out_shape=jax.ShapeDtypeStruct((b, s, MLA_HEADS * MLA_VDIM), BF16),
        scratch_shapes=[pltpu.VMEM((tq // ATT_STRIP, tk, ATT_STRIP), F32),
                        pltpu.VMEM((tq // ATT_STRIP, tk, ATT_STRIP), F32),
                        pltpu.VMEM((1, tq), F32), pltpu.VMEM((1, tq), F32),
                        pltpu.VMEM((1, tq), F32), pltpu.VMEM((MLA_VT_ROWS, tq), F32)],
        compiler_params=_cparams(("arbitrary", "arbitrary")),
        name="mla_attention",
    )(q, kc, vt)


def kernel(x, c, positions, ada_w, ada_b, norm_mix, norm_ffn, gla_w_in, gla_w_gate, gla_b_gate,
           gla_g_out, gla_w_out, mla_w_in, mla_g_q, mla_w_q_up, mla_g_kv, mla_w_kv_up, mla_w_out,
           ffn_w_in, ffn_w_out, final_norm):
    batch = x.shape[0]
    depth = ada_w.shape[0]
    assert batch <= SUBLANE and ada_w.shape[-1] == N_MOD * D_MODEL
    c_pad = jnp.pad(c, ((0, SUBLANE - batch), (0, 0)))
    mod_all = _adaln(c_pad, ada_w, ada_b)

    for i in range(depth):
        j = i // 2
        if i % 2 == 0:
            q, k, v, r, la = _gla_in(x, mod_all, i, norm_mix[i], gla_w_in, j, gla_w_gate,
                                     gla_b_gate[j])
            mixed = _gla(q, k, v, r, la, gla_g_out[j])
            w_o = gla_w_out
        else:
            q, kc, vt = _mla_in(x, mod_all, i, norm_mix[i], positions, mla_w_in[j], mla_g_q[j],
                                mla_w_q_up[j], mla_g_kv[j], mla_w_kv_up[j])
            mixed = _attention(q, kc, vt)
            w_o = mla_w_out
        x, act = _mix_out_ffn_in(mixed, w_o, j, x, mod_all, i, norm_ffn[i], ffn_w_in)
        last = i == depth - 1
        x = _out_res(act, ffn_w_out, i, x, mod_all, i, 5, final_gain=final_norm if last else None)
    return x
```

```python
import functools
import math

import jax
import jax.numpy as jnp
from jax import lax
from jax.experimental import pallas as pl
from jax.experimental.pallas import tpu as pltpu

F32 = jnp.float32
BF16 = jnp.bfloat16

D_MODEL = 1024
EPS = 1e-6

GLA_HEADS = 4
GLA_DK = 128
GLA_DV = 256
GLA_GATE_RANK = 16
GLA_TAU = 16.0
GLA_CHUNK = 64
GLA_SUB = 16

MLA_HEADS = 8
MLA_NOPE = 128
MLA_ROPE = 64
MLA_VDIM = 128
MLA_Q_RANK = 384
MLA_KV_RANK = 256
ROPE_THETA = 10000.0
MLA_QK_PAD = 256
MLA_VT_ROWS = MLA_VDIM + 16

D_FF = 2816
N_MOD = 6

LANE = 128
SUBLANE = 8
ROW_TILE = 512
IN_TILE = 1024
GLA_TIME_BLOCK = 4096
ATT_Q_BLOCK = 1024
ATT_KV_BLOCK = 512
ATT_STRIP = 256
FFN_COL_CHUNK = 256
IN_ROW_CHUNK = 256
VMEM_LIMIT = 56 * 1024 * 1024


def _cparams(sem):
    return pltpu.CompilerParams(dimension_semantics=sem, vmem_limit_bytes=VMEM_LIMIT)


def _dot(a, b):
    return jnp.dot(a, b, preferred_element_type=F32)


def _dot_nt(a, b):
    return lax.dot_general(a, b, (((1,), (1,)), ((), ())), preferred_element_type=F32)


def _sigmoid(x):
    return 1.0 / (1.0 + jnp.exp(-x))


def _rms(x, g):
    return x * lax.rsqrt(jnp.mean(x * x, axis=-1, keepdims=True) + EPS) * g


def _norm_mod(x, g, shift, scale):
    return _rms(x, g * (1.0 + scale)) + shift


def _adaln_kernel(c_ref, w_ref, b_ref, o_ref):
    c = c_ref[...]
    o_ref[...] = _dot(c * _sigmoid(c), w_ref[...]) + b_ref[...]


def _adaln(c_pad, ada_w, ada_b):
    depth, d, n = ada_w.shape
    tn = 1536
    return pl.pallas_call(
        _adaln_kernel,
        grid=(depth, n // tn),
        in_specs=[
            pl.BlockSpec((SUBLANE, d), lambda l, j: (0, 0)),
            pl.BlockSpec((None, d, tn), lambda l, j: (l, 0, j)),
            pl.BlockSpec((None, 1, tn), lambda l, j: (l, 0, j)),
        ],
        out_specs=pl.BlockSpec((None, SUBLANE, tn), lambda l, j: (l, 0, j)),
        out_shape=jax.ShapeDtypeStruct((depth, SUBLANE, n), F32),
        compiler_params=_cparams(("arbitrary", "arbitrary")),
        name="adaln",
    )(c_pad, ada_w, ada_b.reshape(depth, 1, n))


def _row_spec(width, tile=ROW_TILE):
    return pl.BlockSpec((None, tile, width), lambda b, i: (b, i, 0))


def _mod_spec(layer, k):
    return pl.BlockSpec((None, SUBLANE, D_MODEL), lambda b, i: (layer, 0, k))


def _mod_row(ref):
    return ref[pl.ds(pl.program_id(0), 1), :]


def _const_spec(shape):
    return pl.BlockSpec(shape, lambda b, i: (0,) * len(shape))


def _layer_spec(shape, layer, col_block=0):
    return pl.BlockSpec((None,) + shape, lambda b, i: (layer, 0, col_block),
                        pipeline_mode=pl.Buffered(1))


def _gla_in_kernel(x_ref, g_ref, sh_ref, sc_ref, w_ref, wgate_ref, bgate_ref,
                   q_ref, k_ref, v_ref, r_ref, la_ref):
    hk = GLA_HEADS * GLA_DK
    hv = GLA_HEADS * GLA_DV
    shift, scale = _mod_row(sh_ref), _mod_row(sc_ref)
    chunks = [slice(r0, r0 + IN_ROW_CHUNK) for r0 in range(0, IN_TILE, IN_ROW_CHUNK)]
    hs = [_norm_mod(x_ref[rows, :], g_ref[...], shift, scale).astype(BF16) for rows in chunks]

    def proj(lo, width):
        w = w_ref[lo:lo + width, :].astype(BF16)
        return [_dot_nt(h, w) for h in hs]

    w_gate = wgate_ref[...].astype(BF16)
    for rows, q in zip(chunks, proj(0, hk)):
        q_ref[rows, :] = (q * (GLA_DK ** -0.5)).astype(BF16)
    for rows, k in zip(chunks, proj(hk, hk)):
        k_ref[rows, :] = k.astype(BF16)
    for rows, v in zip(chunks, proj(2 * hk, hv)):
        v_ref[rows, :] = v.astype(BF16)
    for rows, r in zip(chunks, proj(2 * hk + hv, hv)):
        r_ref[rows, :] = r.astype(BF16)
    for rows, low_rank in zip(chunks, proj(2 * hk + 2 * hv, GLA_GATE_RANK)):
        z = _dot(low_rank.astype(BF16), w_gate) + bgate_ref[...]
        log_sig = jnp.minimum(z, 0.0) - jnp.log(1.0 + jnp.exp(-jnp.abs(z)))
        la_ref[rows, :] = log_sig * (math.log2(math.e) / GLA_TAU)


def _gla_in(x, mod_all, layer, gain, w_in, j, w_gate, b_gate):
    b, s, d = x.shape
    hk = GLA_HEADS * GLA_DK
    hv = GLA_HEADS * GLA_DV
    return pl.pallas_call(
        _gla_in_kernel,
        grid=(b, s // IN_TILE),
        in_specs=[
            _row_spec(d, IN_TILE), _const_spec((1, d)), _mod_spec(layer, 0), _mod_spec(layer, 1),
            _layer_spec((w_in.shape[-1], d), j), _layer_spec((GLA_GATE_RANK, hk), j),
            _const_spec((1, hk)),
        ],
        out_specs=[_row_spec(hk, IN_TILE), _row_spec(hk, IN_TILE), _row_spec(hv, IN_TILE),
                   _row_spec(hv, IN_TILE), _row_spec(hk, IN_TILE)],
        out_shape=[
            jax.ShapeDtypeStruct((b, s, hk), BF16), jax.ShapeDtypeStruct((b, s, hk), BF16),
            jax.ShapeDtypeStruct((b, s, hv), BF16), jax.ShapeDtypeStruct((b, s, hv), BF16),
            jax.ShapeDtypeStruct((b, s, hk), F32),
        ],
        compiler_params=_cparams(("arbitrary", "arbitrary")),
        name="gla_in",
    )(x, gain.reshape(1, d), mod_all, mod_all, jnp.swapaxes(w_in, 1, 2), w_gate,
      b_gate.reshape(1, hk))


def _gla_kernel(q_ref, k_ref, v_ref, r_ref, la_ref, go_ref, o_ref, state_ref):
    c, sub = GLA_CHUNK, GLA_SUB
    nsub = c // sub
    assert nsub * c == 2 * LANE

    @pl.when(pl.program_id(2) == 0)
    def _():
        state_ref[...] = jnp.zeros_like(state_ref)

    row = lax.broadcasted_iota(jnp.int32, (c, c), 0)
    col = lax.broadcasted_iota(jnp.int32, (c, c), 1)
    tri = jnp.where(row >= col, 1.0, 0.0).astype(BF16)
    qrow = lax.broadcasted_iota(jnp.int32, (c, LANE), 0)
    lane = lax.broadcasted_iota(jnp.int32, (c, LANE), 1)
    causal = (lane % c) <= qrow
    keep_lo = causal & ((lane // c) == (qrow // sub))
    keep_hi = causal & ((lane // c) + LANE // c == (qrow // sub))

    n_chunks = GLA_TIME_BLOCK // c
    cs = [slice(ci * c, (ci + 1) * c) for ci in range(n_chunks)]

    bs = []
    for sl in cs:
        la = la_ref[sl, :]
        la_hi = la.astype(BF16)
        la_lo = (la - la_hi.astype(F32)).astype(BF16)
        bs.append(_dot(tri, la_hi) + _dot(tri, la_lo))

    qes, attns, kd_ts, decays = [], [], [], []
    for sl, b in zip(cs, bs):
        q = q_ref[sl, :].astype(F32)
        k = k_ref[sl, :].astype(F32)
        qes.append((q * jnp.exp2(b)).astype(BF16))
        q_parts, k_parts = [], []
        for i in range(nsub):
            rows = slice(i * sub, (i + 1) * sub)
            live = (i + 1) * sub
            if i == 0:
                q_parts.append(q[rows] * jnp.exp2(b[rows]))
                k_var = k[:live] * jnp.exp2(-b[:live])
            else:
                anchor = b[i * sub - 1:i * sub, :]
                q_parts.append(q[rows] * jnp.exp2(b[rows] - anchor))
                k_var = k[:live] * jnp.exp2(anchor - b[:live])
            k_parts.append(k_var)
            if live < c:
                k_parts.append(jnp.zeros((c - live, GLA_DK), F32))
        q_anch = jnp.concatenate(q_parts, axis=0).astype(BF16)
        k_stack = jnp.concatenate(k_parts, axis=0).astype(BF16)
        prod = _dot_nt(q_anch, k_stack)
        attns.append(jnp.where(keep_lo, prod[:, :LANE],
                               jnp.where(keep_hi, prod[:, LANE:], 0.0)).astype(BF16))
        b_last = b[c - 1:c, :]
        kd_ts.append((k * jnp.exp2(b_last - b)).T.astype(BF16))
        decays.append(jnp.broadcast_to(jnp.exp2(b_last), (LANE, GLA_DK)).T)

    state = state_ref[...]
    gain = go_ref[...]
    for ci, sl in enumerate(cs):
        v = v_ref[sl, :]
        v2 = jnp.concatenate([v, v], axis=0)
        o = _dot(qes[ci], state.astype(BF16)) + _dot(attns[ci], v2)
        decay = jnp.concatenate([decays[ci]] * (GLA_DV // LANE), axis=1)
        state = state * decay + _dot(kd_ts[ci], v)
        half_r = 0.5 * r_ref[sl, :].astype(F32)
        silu_r = half_r + half_r * jnp.tanh(half_r)
        o_ref[sl, :] = (_rms(o, gain) * silu_r).astype(BF16)
    state_ref[...] = state


def _gla(q, k, v, r, la, g_out):
    b, s, _ = q.shape
    t = GLA_TIME_BLOCK

    def spec(width):
        return pl.BlockSpec((None, t, width), lambda bi, h, ti: (bi, ti, h))

    return pl.pallas_call(
        _gla_kernel,
        grid=(b, GLA_HEADS, s // t),
        in_specs=[spec(GLA_DK), spec(GLA_DK), spec(GLA_DV), spec(GLA_DV), spec(GLA_DK),
                  pl.BlockSpec((1, GLA_DV), lambda bi, h, ti: (0, 0))],
        out_specs=spec(GLA_DV),
        out_shape=jax.ShapeDtypeStruct((b, s, GLA_HEADS * GLA_DV), BF16),
        scratch_shapes=[pltpu.VMEM((GLA_DK, GLA_DV), F32)],
        compiler_params=_cparams(("arbitrary", "arbitrary", "arbitrary")),
        name="gla_scan",
    )(q, k, v, r, la, g_out.reshape(1, GLA_DV))


def _out_res_kernel(a_ref, w_ref, x_ref, gate_ref, *rest, final_norm):
    o_ref = rest[-1]
    y = x_ref[...] + _mod_row(gate_ref) * _dot(a_ref[...], w_ref[...].astype(BF16))
    if final_norm:
        y = _rms(y, rest[0][...])
    o_ref[...] = y


def _out_res(a, w, j, x, mod_all, layer, gate_idx, final_gain=None):
    b, s, d = x.shape
    kdim = a.shape[-1]
    in_specs = [_row_spec(kdim, IN_TILE), _layer_spec((kdim, d), j), _row_spec(d, IN_TILE),
                _mod_spec(layer, gate_idx)]
    args = [a, w, x, mod_all]
    if final_gain is not None:
        in_specs.append(_const_spec((1, d)))
        args.append(final_gain.reshape(1, d))
    return pl.pallas_call(
        functools.partial(_out_res_kernel, final_norm=final_gain is not None),
        grid=(b, s // IN_TILE),
        in_specs=in_specs,
        out_specs=_row_spec(d, IN_TILE),
        out_shape=jax.ShapeDtypeStruct((b, s, d), F32),
        compiler_params=_cparams(("arbitrary", "arbitrary")),
        name="out_res",
    )(*args)


def _mix_out_ffn_in_kernel(m_ref, wo_ref, x_ref, g1_ref, g_ref, sh_ref, sc_ref, wg_ref, wu_ref,
                           x1_ref, a_ref):
    x1 = x_ref[...] + _mod_row(g1_ref) * _dot(m_ref[...], wo_ref[...].astype(BF16))
    x1_ref[...] = x1
    h = _norm_mod(x1, g_ref[...], _mod_row(sh_ref), _mod_row(sc_ref)).astype(BF16)
    for j in range(0, D_FF, FFN_COL_CHUNK):
        cols = slice(j, j + FFN_COL_CHUNK)
        gate = _dot(h, wg_ref[:, cols].astype(BF16))
        up = _dot(h, wu_ref[:, cols].astype(BF16))
        a_ref[:, cols] = (gate * _sigmoid(gate) * up).astype(BF16)


def _mix_out_ffn_in(mixed, w_o, j, x, mod_all, layer, gain, w_in):
    b, s, d = x.shape
    kdim = mixed.shape[-1]
    return pl.pallas_call(
        _mix_out_ffn_in_kernel,
        grid=(b, s // ROW_TILE),
        in_specs=[_row_spec(kdim), _layer_spec((kdim, d), j), _row_spec(d), _mod_spec(layer, 2),
                  _const_spec((1, d)), _mod_spec(layer, 3), _mod_spec(layer, 4),
                  _layer_spec((d, D_FF), layer, 0), _layer_spec((d, D_FF), layer, 1)],
        out_specs=[_row_spec(d), _row_spec(D_FF)],
        out_shape=[jax.ShapeDtypeStruct((b, s, d), F32), jax.ShapeDtypeStruct((b, s, D_FF), BF16)],
        compiler_params=_cparams(("arbitrary", "arbitrary")),
        name="mix_out_ffn_in",
    )(mixed, w_o, x, mod_all, gain.reshape(1, d), mod_all, mod_all, w_in, w_in)


def _mla_in_kernel(x_ref, g_ref, sh_ref, sc_ref, pos_ref, freq_ref, phase_ref, win_ref, gq_ref,
                   wq_ref, gkv_ref, wkn_ref, wvt_ref, q_ref, kc_ref, vt_ref):
    shift, scale = _mod_row(sh_ref), _mod_row(sc_ref)
    rope_lanes = lax.broadcasted_iota(jnp.int32, (IN_ROW_CHUNK, LANE), 1) < MLA_ROPE
    ones = jnp.ones((MLA_VT_ROWS - MLA_VDIM, IN_ROW_CHUNK), BF16)
    for r0 in range(0, IN_TILE, IN_ROW_CHUNK):
        rows = slice(r0, r0 + IN_ROW_CHUNK)
        h = _norm_mod(x_ref[rows, :], g_ref[...], shift, scale).astype(BF16)
        proj = _dot(h, win_ref[...])
        cq = _rms(proj[:, :MLA_Q_RANK], gq_ref[...]).astype(BF16)
        ckv = _rms(proj[:, MLA_Q_RANK:MLA_Q_RANK + MLA_KV_RANK], gkv_ref[...]).astype(BF16)

        pos = jnp.broadcast_to(pos_ref[:, rows].astype(F32), (LANE, IN_ROW_CHUNK)).T
        table = jnp.cos(pos * freq_ref[...] + phase_ref[...])

        def rotate(t):
            u = t * table
            return u + pltpu.roll(u, MLA_ROPE, axis=1)

        k_rope = jnp.where(rope_lanes, rotate(proj[:, MLA_Q_RANK + MLA_KV_RANK:]), 0.0)
        k_rope = k_rope.astype(BF16)
        v_t = _dot_nt(wvt_ref[...], ckv)
        k_nope = _dot(ckv, wkn_ref[...])
        q_all = _dot(cq, wq_ref[...])
        for hd in range(MLA_HEADS):
            lo = hd * MLA_QK_PAD
            mid = lo + MLA_NOPE
            hi = lo + MLA_QK_PAD
            q_ref[rows, lo:mid] = q_all[:, lo:mid].astype(BF16)
            q_ref[rows, mid:hi] = rotate(q_all[:, mid:hi]).astype(BF16)
            kc_ref[rows, lo:mid] = k_nope[:, hd * MLA_NOPE:(hd + 1) * MLA_NOPE].astype(BF16)
            kc_ref[rows, mid:hi] = k_rope
            vt_ref[hd, :MLA_VDIM, rows] = v_t[hd * MLA_VDIM:(hd + 1) * MLA_VDIM, :].astype(BF16)
            vt_ref[hd, MLA_VDIM:, rows] = ones


def _half_split(w_rope):
    even, odd = w_rope[..., 0::2], w_rope[..., 1::2]
    return jnp.concatenate([even, odd, odd, even], axis=-1)


def _mla_in(x, mod_all, layer, gain, positions, w_in, g_q, w_q_up, g_kv, w_kv_up):
    b, s, d = x.shape
    hq = MLA_HEADS * MLA_QK_PAD
    hn = MLA_HEADS * MLA_NOPE
    hv = MLA_HEADS * MLA_VDIM
    lat = MLA_Q_RANK + MLA_KV_RANK
    half = MLA_ROPE // 2
    win = jnp.concatenate([w_in[:, :lat], _half_split(w_in[:, lat:])], axis=-1).astype(BF16)
    wq3 = w_q_up.reshape(MLA_Q_RANK, MLA_HEADS, MLA_NOPE + MLA_ROPE)
    wq = jnp.concatenate([wq3[..., :MLA_NOPE], _half_split(wq3[..., MLA_NOPE:])], axis=-1)
    q_scale = (MLA_NOPE + MLA_ROPE) ** -0.5 * math.log2(math.e)
    wq = (wq * q_scale).reshape(MLA_Q_RANK, hq).astype(BF16)
    wkv3 = w_kv_up.reshape(MLA_KV_RANK, MLA_HEADS, MLA_NOPE + MLA_VDIM)
    wkn = wkv3[..., :MLA_NOPE].reshape(MLA_KV_RANK, hn).astype(BF16)
    wvt = wkv3[..., MLA_NOPE:].reshape(MLA_KV_RANK, hv).T.astype(BF16)
    win_w = win.shape[-1]
    inv_freq = ROPE_THETA ** (-jnp.arange(0, MLA_ROPE, 2, dtype=F32) / MLA_ROPE)
    freq_row = jnp.tile(inv_freq, LANE // half).reshape(1, LANE)
    quarter = jnp.full((half,), math.pi / 2, F32)
    phase_row = jnp.concatenate([jnp.zeros((2 * half,), F32), quarter, -quarter]).reshape(1, LANE)
    return pl.pallas_call(
        _mla_in_kernel,
        grid=(b, s // IN_TILE),
        in_specs=[
            _row_spec(d, IN_TILE), _const_spec((1, d)), _mod_spec(layer, 0), _mod_spec(layer, 1),
            pl.BlockSpec((None, 1, IN_TILE), lambda bi, i: (bi, 0, i)),
            _const_spec((1, LANE)), _const_spec((1, LANE)),
            _const_spec((d, win_w)), _const_spec((1, MLA_Q_RANK)), _const_spec((MLA_Q_RANK, hq)),
            _const_spec((1, MLA_KV_RANK)), _const_spec((MLA_KV_RANK, hn)),
            _const_spec((hv, MLA_KV_RANK)),
        ],
        out_specs=[_row_spec(hq, IN_TILE), _row_spec(hq, IN_TILE),
                   pl.BlockSpec((None, MLA_HEADS, MLA_VT_ROWS, IN_TILE), lambda bi, i: (bi, 0, 0, i))],
        out_shape=[jax.ShapeDtypeStruct((b, s, hq), BF16), jax.ShapeDtypeStruct((b, s, hq), BF16),
                   jax.ShapeDtypeStruct((b, MLA_HEADS, MLA_VT_ROWS, s), BF16)],
        compiler_params=_cparams(("arbitrary", "arbitrary")),
        name="mla_in",
    )(x, gain.reshape(1, d), mod_all, mod_all, positions.reshape(b, 1, s), freq_row, phase_row,
      win, g_q.reshape(1, -1), wq, g_kv.reshape(1, -1), wkn, wvt)


def _attn_kernel(q_ref, k_ref, vt_ref, o_ref, s0_ref, s1_ref, mb0_ref, mb1_ref, m_ref, acc_ref):
    tq, tk = ATT_Q_BLOCK, ATT_KV_BLOCK
    per_q = tq // tk
    assert per_q == 2
    n_q = q_ref.shape[0] // tq
    n_strips = tq // ATT_STRIP
    all_strips = tuple(range(n_strips))

    def lanes(c):
        return slice(c * ATT_STRIP, (c + 1) * ATT_STRIP)

    def scores(qi, kj, s_ref, mb_ref, diag_offset=None, strips=all_strips):
        start = pl.multiple_of(kj * tk, tk)
        k_blk = k_ref[pl.ds(start, tk), :]
        for c in strips:
            q_lo = c * ATT_STRIP
            if diag_offset is not None and diag_offset > q_lo + ATT_STRIP - 1:
                s_ref[c] = jnp.full((tk, ATT_STRIP), -jnp.inf, F32)
                mb_ref[:, lanes(c)] = jnp.full((1, ATT_STRIP), -jnp.inf, F32)
                continue
            q_strip = q_ref[pl.ds(pl.multiple_of(qi * tq + q_lo, ATT_STRIP), ATT_STRIP), :]
            s_t = _dot_nt(k_blk, q_strip)
            if diag_offset is not None and diag_offset + tk - 1 > q_lo:
                key = lax.broadcasted_iota(jnp.int32, (tk, ATT_STRIP), 0) + diag_offset
                qry = lax.broadcasted_iota(jnp.int32, (tk, ATT_STRIP), 1) + q_lo
                s_t = jnp.where(key <= qry, s_t, -jnp.inf)
            s_ref[c] = s_t
            mb_ref[:, lanes(c)] = jnp.max(s_t, axis=0, keepdims=True)

    def accumulate(kj, s_ref, mb_ref, strips=all_strips):
        start = pl.multiple_of(kj * tk, tk)
        vt_blk = vt_ref[:, pl.ds(start, tk)]
        for c in strips:
            cols = lanes(c)
            m_prev = m_ref[:, cols]
            m_new = jnp.maximum(m_prev, mb_ref[:, cols])
            p_t = jnp.exp2((s_ref[c] - m_new).astype(BF16))
            alpha = jnp.exp2(m_prev - m_new)
            acc_ref[:, cols] = alpha * acc_ref[:, cols] + _dot(vt_blk, p_t)
            m_ref[:, cols] = m_new

    def reset():
        m_ref[...] = jnp.full_like(m_ref, -jnp.inf)
        acc_ref[...] = jnp.zeros_like(acc_ref)

    def finish(qi, pending):
        accumulate(pending, s1_ref, mb1_ref)
        acc = acc_ref[...]
        o_t = acc[:MLA_VDIM, :] / acc[MLA_VDIM:MLA_VDIM + 1, :]
        o_ref[pl.ds(pl.multiple_of(qi * tq, tq), tq), :] = o_t.T.astype(BF16)

    def run_block(qi):
        first = per_q * qi
        reset()
        scores(qi, first + 1, s1_ref, mb1_ref, diag_offset=tk)
        accumulate(first, s0_ref, mb0_ref)
        if isinstance(qi, int) and qi == 0:
            return first + 1

        for c in all_strips:
            if tk <= c * ATT_STRIP + ATT_STRIP - 1:
                accumulate(first + 1, s1_ref, mb1_ref, strips=(c,))
            scores(qi, 0, s0_ref, mb0_ref, strips=(c,))

        def pair(p, pending):
            for c in all_strips:
                accumulate(pending, s0_ref, mb0_ref, strips=(c,))
                scores(qi, 2 * p + 1, s1_ref, mb1_ref, strips=(c,))
            for c in all_strips:
                accumulate(2 * p + 1, s1_ref, mb1_ref, strips=(c,))
                scores(qi, 2 * p + 2, s0_ref, mb0_ref, strips=(c,))
            return 2 * p + 2

        def quad(u, pending):
            return pair(2 * u + 1, pair(2 * u, pending))

        n_pairs = qi - 1
        pending = lax.fori_loop(0, n_pairs // 2, quad, 0)
        pending = lax.fori_loop(0, n_pairs % 2, lambda _, pend: pair(n_pairs - 1, pend), pending)
        last = 2 * qi - 1
        for c in all_strips:
            accumulate(pending, s0_ref, mb0_ref, strips=(c,))
            scores(qi, last, s1_ref, mb1_ref, strips=(c,))
        return last

    scores(0, 0, s0_ref, mb0_ref, diag_offset=0)
    pending0 = run_block(0)

    def q_block(qi, pending):
        scores(qi, per_q * qi, s0_ref, mb0_ref, diag_offset=0)
        finish(qi - 1, pending)
        return run_block(qi)

    pending_last = lax.fori_loop(1, n_q, q_block, pending0)
    finish(n_q - 1, pending_last)


def _attention(q, kc, vt):
    b, _, _, s = vt.shape
    tq, tk = ATT_Q_BLOCK, ATT_KV_BLOCK
    return pl.pallas_call(
        _attn_kernel,
        grid=(b, MLA_HEADS),
        in_specs=[
            pl.BlockSpec((None, s, MLA_QK_PAD), lambda bi, h: (bi, 0, h)),
            pl.BlockSpec((None, s, MLA_QK_PAD), lambda bi, h: (bi, 0, h)),
            pl.BlockSpec((None, None, MLA_VT_ROWS, s), lambda bi, h: (bi, h, 0, 0)),
        ],
        out_specs=pl.BlockSpec((None, s, MLA_VDIM), lambda bi, h: (bi, 0, h)),
        out_shape=jax.ShapeDtypeStruct((b, s, MLA_HEADS * MLA_VDIM), BF16),
        scratch_shapes=[pltpu.VMEM((tq // ATT_STRIP, tk, ATT_STRIP), F32),
                        pltpu.VMEM((tq // ATT_STRIP, tk, ATT_STRIP), F32),
                        pltpu.VMEM((1, tq), F32), pltpu.VMEM((1, tq), F32),
                        pltpu.VMEM((1, tq), F32), pltpu.VMEM((MLA_VT_ROWS, tq), F32)],
        compiler_params=_cparams(("arbitrary", "arbitrary")),
        name="mla_attention",
    )(q, kc, vt)


def kernel(x, c, positions, ada_w, ada_b, norm_mix, norm_ffn, gla_w_in, gla_w_gate, gla_b_gate,
           gla_g_out, gla_w_out, mla_w_in, mla_g_q, mla_w_q_up, mla_g_kv, mla_w_kv_up, mla_w_out,
           ffn_w_in, ffn_w_out, final_norm):
    batch = x.shape[0]
    depth = ada_w.shape[0]
    assert batch <= SUBLANE and ada_w.shape[-1] == N_MOD * D_MODEL
    c_pad = jnp.pad(c, ((0, SUBLANE - batch), (0, 0)))
    mod_all = _adaln(c_pad, ada_w, ada_b)

    for i in range(depth):
        j = i // 2
        if i % 2 == 0:
            q, k, v, r, la = _gla_in(x, mod_all, i, norm_mix[i], gla_w_in, j, gla_w_gate,
                                     gla_b_gate[j])
            mixed = _gla(q, k, v, r, la, gla_g_out[j])
            w_o = gla_w_out
        else:
            q, kc, vt = _mla_in(x, mod_all, i, norm_mix[i], positions, mla_w_in[j], mla_g_q[j],
                                mla_w_q_up[j], mla_g_kv[j], mla_w_kv_up[j])
            mixed = _attention(q, kc, vt)
            w_o = mla_w_out
        x, act = _mix_out_ffn_in(mixed, w_o, j, x, mod_all, i, norm_ffn[i], ffn_w_in)
        last = i == depth - 1
        x = _out_res(act, ffn_w_out, i, x, mod_all, i, 5, final_gain=final_norm if last else None)
    return x
```

```python
import functools
import math

import jax
import jax.numpy as jnp
from jax import lax
from jax.experimental import pallas as pl
from jax.experimental.pallas import tpu as pltpu

F32 = jnp.float32
BF16 = jnp.bfloat16

D_MODEL = 1024
EPS = 1e-6

GLA_HEADS = 4
GLA_DK = 128
GLA_DV = 256
GLA_GATE_RANK = 16
GLA_TAU = 16.0
GLA_CHUNK = 64
GLA_SUB = 16

MLA_HEADS = 8
MLA_NOPE = 128
MLA_ROPE = 64
MLA_VDIM = 128
MLA_Q_RANK = 384
MLA_KV_RANK = 256
ROPE_THETA = 10000.0
MLA_QK_PAD = 256
MLA_VT_ROWS = MLA_VDIM + 16

D_FF = 2816
N_MOD = 6

LANE = 128
SUBLANE = 8
ROW_TILE = 512
IN_TILE = 1024
GLA_TIME_BLOCK = 4096
ATT_Q_BLOCK = 1024
ATT_KV_BLOCK = 512
ATT_STRIP = 256
FFN_COL_CHUNK = 256
IN_ROW_CHUNK = 256
VMEM_LIMIT = 56 * 1024 * 1024


def _cparams(sem):
    return pltpu.CompilerParams(dimension_semantics=sem, vmem_limit_bytes=VMEM_LIMIT)


def _dot(a, b):
    return jnp.dot(a, b, preferred_element_type=F32)


def _dot_nt(a, b):
    return lax.dot_general(a, b, (((1,), (1,)), ((), ())), preferred_element_type=F32)


def _sigmoid(x):
    return 1.0 / (1.0 + jnp.exp(-x))


def _rms(x, g):
    return x * lax.rsqrt(jnp.mean(x * x, axis=-1, keepdims=True) + EPS) * g


def _norm_mod(x, g, shift, scale):
    return _rms(x, g * (1.0 + scale)) + shift


def _adaln_kernel(c_ref, w_ref, b_ref, o_ref):
    c = c_ref[...]
    o_ref[...] = _dot(c * _sigmoid(c), w_ref[...]) + b_ref[...]


def _adaln(c_pad, ada_w, ada_b):
    depth, d, n = ada_w.shape
    tn = 1536
    return pl.pallas_call(
        _adaln_kernel,
        grid=(depth, n // tn),
        in_specs=[
            pl.BlockSpec((SUBLANE, d), lambda l, j: (0, 0)),
            pl.BlockSpec((None, d, tn), lambda l, j: (l, 0, j)),
            pl.BlockSpec((None, 1, tn), lambda l, j: (l, 0, j)),
        ],
        out_specs=pl.BlockSpec((None, SUBLANE, tn), lambda l, j: (l, 0, j)),
        out_shape=jax.ShapeDtypeStruct((depth, SUBLANE, n), F32),
        compiler_params=_cparams(("arbitrary", "arbitrary")),
        name="adaln",
    )(c_pad, ada_w, ada_b.reshape(depth, 1, n))


def _row_spec(width, tile=ROW_TILE):
    return pl.BlockSpec((None, tile, width), lambda b, i: (b, i, 0))


def _mod_spec(layer, k):
    return pl.BlockSpec((None, SUBLANE, D_MODEL), lambda b, i: (layer, 0, k))


def _mod_row(ref):
    return ref[pl.ds(pl.program_id(0), 1), :]


def _const_spec(shape):
    return pl.BlockSpec(shape, lambda b, i: (0,) * len(shape))


def _layer_spec(shape, layer, col_block=0):
    return pl.BlockSpec((None,) + shape, lambda b, i: (layer, 0, col_block),
                        pipeline_mode=pl.Buffered(1))


def _gla_in_kernel(x_ref, g_ref, sh_ref, sc_ref, w_ref, wgate_ref, bgate_ref,
                   q_ref, k_ref, v_ref, r_ref, la_ref):
    hk = GLA_HEADS * GLA_DK
    hv = GLA_HEADS * GLA_DV
    shift, scale = _mod_row(sh_ref), _mod_row(sc_ref)
    chunks = [slice(r0, r0 + IN_ROW_CHUNK) for r0 in range(0, IN_TILE, IN_ROW_CHUNK)]
    hs = [_norm_mod(x_ref[rows, :], g_ref[...], shift, scale).astype(BF16) for rows in chunks]

    def proj(lo, width):
        w = w_ref[lo:lo + width, :].astype(BF16)
        return [_dot_nt(h, w) for h in hs]

    w_gate = wgate_ref[...].astype(BF16)
    for rows, q in zip(chunks, proj(0, hk)):
        q_ref[rows, :] = (q * (GLA_DK ** -0.5)).astype(BF16)
    for rows, k in zip(chunks, proj(hk, hk)):
        k_ref[rows, :] = k.astype(BF16)
    for rows, v in zip(chunks, proj(2 * hk, hv)):
        v_ref[rows, :] = v.astype(BF16)
    for rows, r in zip(chunks, proj(2 * hk + hv, hv)):
        r_ref[rows, :] = r.astype(BF16)
    for rows, low_rank in zip(chunks, proj(2 * hk + 2 * hv, GLA_GATE_RANK)):
        z = _dot(low_rank.astype(BF16), w_gate) + bgate_ref[...]
        log_sig = jnp.minimum(z, 0.0) - jnp.log(1.0 + jnp.exp(-jnp.abs(z)))
        la_ref[rows, :] = log_sig * (math.log2(math.e) / GLA_TAU)


def _gla_in(x, mod_all, layer, gain, w_in, j, w_gate, b_gate):
    b, s, d = x.shape
    hk = GLA_HEADS * GLA_DK
    hv = GLA_HEADS * GLA_DV
    return pl.pallas_call(
        _gla_in_kernel,
        grid=(b, s // IN_TILE),
        in_specs=[
            _row_spec(d, IN_TILE), _const_spec((1, d)), _mod_spec(layer, 0), _mod_spec(layer, 1),
            _layer_spec((w_in.shape[-1], d), j), _layer_spec((GLA_GATE_RANK, hk), j),
            _const_spec((1, hk)),
        ],
        out_specs=[_row_spec(hk, IN_TILE), _row_spec(hk, IN_TILE), _row_spec(hv, IN_TILE),
                   _row_spec(hv, IN_TILE), _row_spec(hk, IN_TILE)],
        out_shape=[
            jax.ShapeDtypeStruct((b, s, hk), BF16), jax.ShapeDtypeStruct((b, s, hk), BF16),
            jax.ShapeDtypeStruct((b, s, hv), BF16), jax.ShapeDtypeStruct((b, s, hv), BF16),
            jax.ShapeDtypeStruct((b, s, hk), F32),
        ],
        compiler_params=_cparams(("arbitrary", "arbitrary")),
        name="gla_in",
    )(x, gain.reshape(1, d), mod_all, mod_all, jnp.swapaxes(w_in, 1, 2), w_gate,
      b_gate.reshape(1, hk))


def _gla_kernel(q_ref, k_ref, v_ref, r_ref, la_ref, go_ref, o_ref, state_ref):
    c, sub = GLA_CHUNK, GLA_SUB
    nsub = c // sub
    assert nsub * c == 2 * LANE

    @pl.when(pl.program_id(2) == 0)
    def _():
        state_ref[...] = jnp.zeros_like(state_ref)

    row = lax.broadcasted_iota(jnp.int32, (c, c), 0)
    col = lax.broadcasted_iota(jnp.int32, (c, c), 1)
    tri = jnp.where(row >= col, 1.0, 0.0).astype(BF16)
    qrow = lax.broadcasted_iota(jnp.int32, (c, LANE), 0)
    lane = lax.broadcasted_iota(jnp.int32, (c, LANE), 1)
    causal = (lane % c) <= qrow
    keep_lo = causal & ((lane // c) == (qrow // sub))
    keep_hi = causal & ((lane // c) + LANE // c == (qrow // sub))

    n_chunks = GLA_TIME_BLOCK // c
    cs = [slice(ci * c, (ci + 1) * c) for ci in range(n_chunks)]

    bs = []
    for sl in cs:
        la = la_ref[sl, :]
        la_hi = la.astype(BF16)
        la_lo = (la - la_hi.astype(F32)).astype(BF16)
        bs.append(_dot(tri, la_hi) + _dot(tri, la_lo))

    qes, attns, kd_ts, decays = [], [], [], []
    for sl, b in zip(cs, bs):
        q = q_ref[sl, :].astype(F32)
        k = k_ref[sl, :].astype(F32)
        qes.append((q * jnp.exp2(b)).astype(BF16))
        q_parts, k_parts = [], []
        for i in range(nsub):
            rows = slice(i * sub, (i + 1) * sub)
            live = (i + 1) * sub
            if i == 0:
                q_parts.append(q[rows] * jnp.exp2(b[rows]))
                k_var = k[:live] * jnp.exp2(-b[:live])
            else:
                anchor = b[i * sub - 1:i * sub, :]
                q_parts.append(q[rows] * jnp.exp2(b[rows] - anchor))
                k_var = k[:live] * jnp.exp2(anchor - b[:live])
            k_parts.append(k_var)
            if live < c:
                k_parts.append(jnp.zeros((c - live, GLA_DK), F32))
        q_anch = jnp.concatenate(q_parts, axis=0).astype(BF16)
        k_stack = jnp.concatenate(k_parts, axis=0).astype(BF16)
        prod = _dot_nt(q_anch, k_stack)
        attns.append(jnp.where(keep_lo, prod[:, :LANE],
                               jnp.where(keep_hi, prod[:, LANE:], 0.0)).astype(BF16))
        b_last = b[c - 1:c, :]
        kd_ts.append((k * jnp.exp2(b_last - b)).T.astype(BF16))
        decays.append(jnp.broadcast_to(jnp.exp2(b_last), (LANE, GLA_DK)).T)

    state = state_ref[...]
    gain = go_ref[...]
    for ci, sl in enumerate(cs):
        v = v_ref[sl, :]
        v2 = jnp.concatenate([v, v], axis=0)
        o = _dot(qes[ci], state.astype(BF16)) + _dot(attns[ci], v2)
        decay = jnp.concatenate([decays[ci]] * (GLA_DV // LANE), axis=1)
        state = state * decay + _dot(kd_ts[ci], v)
        half_r = 0.5 * r_ref[sl, :].astype(F32)
        silu_r = half_r + half_r * jnp.tanh(half_r)
        o_ref[sl, :] = (_rms(o, gain) * silu_r).astype(BF16)
    state_ref[...] = state


def _gla(q, k, v, r, la, g_out):
    b, s, _ = q.shape
    t = GLA_TIME_BLOCK

    def spec(width):
        return pl.BlockSpec((None, t, width), lambda bi, h, ti: (bi, ti, h))

    return pl.pallas_call(
        _gla_kernel,
        grid=(b, GLA_HEADS, s // t),
        in_specs=[spec(GLA_DK), spec(GLA_DK), spec(GLA_DV), spec(GLA_DV), spec(GLA_DK),
                  pl.BlockSpec((1, GLA_DV), lambda bi, h, ti: (0, 0))],
        out_specs=spec(GLA_DV),
        out_shape=jax.ShapeDtypeStruct((b, s, GLA_HEADS * GLA_DV), BF16),
        scratch_shapes=[pltpu.VMEM((GLA_DK, GLA_DV), F32)],
        compiler_params=_cparams(("arbitrary", "arbitrary", "arbitrary")),
        name="gla_scan",
    )(q, k, v, r, la, g_out.reshape(1, GLA_DV))


def _out_res_kernel(a_ref, w_ref, x_ref, gate_ref, *rest, final_norm):
    o_ref = rest[-1]
    y = x_ref[...] + _mod_row(gate_ref) * _dot(a_ref[...], w_ref[...].astype(BF16))
    if final_norm:
        y = _rms(y, rest[0][...])
    o_ref[...] = y


def _out_res(a, w, j, x, mod_all, layer, gate_idx, final_gain=None):
    b, s, d = x.shape
    kdim = a.shape[-1]
    in_specs = [_row_spec(kdim, IN_TILE), _layer_spec((kdim, d), j), _row_spec(d, IN_TILE),
                _mod_spec(layer, gate_idx)]
    args = [a, w, x, mod_all]
    if final_gain is not None:
        in_specs.append(_const_spec((1, d)))
        args.append(final_gain.reshape(1, d))
    return pl.pallas_call(
        functools.partial(_out_res_kernel, final_norm=final_gain is not None),
        grid=(b, s // IN_TILE),
        in_specs=in_specs,
        out_specs=_row_spec(d, IN_TILE),
        out_shape=jax.ShapeDtypeStruct((b, s, d), F32),
        compiler_params=_cparams(("arbitrary", "arbitrary")),
        name="out_res",
    )(*args)


def _mix_out_ffn_in_kernel(m_ref, wo_ref, x_ref, g1_ref, g_ref, sh_ref, sc_ref, wg_ref, wu_ref,
                           x1_ref, a_ref):
    x1 = x_ref[...] + _mod_row(g1_ref) * _dot(m_ref[...], wo_ref[...].astype(BF16))
    x1_ref[...] = x1
    h = _norm_mod(x1, g_ref[...], _mod_row(sh_ref), _mod_row(sc_ref)).astype(BF16)
    for j in range(0, D_FF, FFN_COL_CHUNK):
        cols = slice(j, j + FFN_COL_CHUNK)
        gate = _dot(h, wg_ref[:, cols].astype(BF16))
        up = _dot(h, wu_ref[:, cols].astype(BF16))
        a_ref[:, cols] = (gate * _sigmoid(gate) * up).astype(BF16)


def _mix_out_ffn_in(mixed, w_o, j, x, mod_all, layer, gain, w_in):
    b, s, d = x.shape
    kdim = mixed.shape[-1]
    return pl.pallas_call(
        _mix_out_ffn_in_kernel,
        grid=(b, s // ROW_TILE),
        in_specs=[_row_spec(kdim), _layer_spec((kdim, d), j), _row_spec(d), _mod_spec(layer, 2),
                  _const_spec((1, d)), _mod_spec(layer, 3), _mod_spec(layer, 4),
                  _layer_spec((d, D_FF), layer, 0), _layer_spec((d, D_FF), layer, 1)],
        out_specs=[_row_spec(d), _row_spec(D_FF)],
        out_shape=[jax.ShapeDtypeStruct((b, s, d), F32), jax.ShapeDtypeStruct((b, s, D_FF), BF16)],
        compiler_params=_cparams(("arbitrary", "arbitrary")),
        name="mix_out_ffn_in",
    )(mixed, w_o, x, mod_all, gain.reshape(1, d), mod_all, mod_all, w_in, w_in)


def _mla_in_kernel(x_ref, g_ref, sh_ref, sc_ref, pos_ref, freq_ref, phase_ref, cosr_ref, sinr_ref,
                   win_ref, gq_ref,
                   wq_ref, gkv_ref, wkn_ref, wvt_ref, q_ref, kc_ref, vt_ref, table_ref):
    shift, scale = _mod_row(sh_ref), _mod_row(sc_ref)
    chunk_starts = range(0, IN_TILE, IN_ROW_CHUNK)

    pos_row = pos_ref[...]
    first_pos = pos_row[:, 0:1]
    ramp = lax.broadcasted_iota(jnp.int32, (1, IN_TILE), 1)
    consecutive = jnp.max(jnp.abs(pos_row - first_pos - ramp)) == 0

    @pl.when(consecutive)
    def _():
        for r0 in chunk_starts:
            a0 = (first_pos + r0).astype(F32) * freq_ref[...] + phase_ref[...]
            table_ref[r0:r0 + IN_ROW_CHUNK, :] = (jnp.cos(a0) * cosr_ref[...]
                                                  - jnp.sin(a0) * sinr_ref[...])

    @pl.when(jnp.logical_not(consecutive))
    def _():
        pos = jnp.broadcast_to(pos_row.astype(F32), (LANE, IN_TILE)).T
        table_ref[...] = jnp.cos(pos * freq_ref[...] + phase_ref[...])

    rope_lanes = lax.broadcasted_iota(jnp.int32, (IN_ROW_CHUNK, LANE), 1) < MLA_ROPE
    ones = jnp.ones((MLA_VT_ROWS - MLA_VDIM, IN_ROW_CHUNK), BF16)
    for r0 in chunk_starts:
        rows = slice(r0, r0 + IN_ROW_CHUNK)
        h = _norm_mod(x_ref[rows, :], g_ref[...], shift, scale).astype(BF16)
        proj = _dot(h, win_ref[...])
        cq = _rms(proj[:, :MLA_Q_RANK], gq_ref[...]).astype(BF16)
        ckv = _rms(proj[:, MLA_Q_RANK:MLA_Q_RANK + MLA_KV_RANK], gkv_ref[...]).astype(BF16)

        table = table_ref[rows, :]

        def rotate(t):
            u = t * table
            return u + pltpu.roll(u, MLA_ROPE, axis=1)

        k_rope = jnp.where(rope_lanes, rotate(proj[:, MLA_Q_RANK + MLA_KV_RANK:]), 0.0)
        k_rope = k_rope.astype(BF16)
        v_t = _dot_nt(wvt_ref[...], ckv)
        k_nope = _dot(ckv, wkn_ref[...])
        q_all = _dot(cq, wq_ref[...])
        for hd in range(MLA_HEADS):
            lo = hd * MLA_QK_PAD
            mid = lo + MLA_NOPE
            hi = lo + MLA_QK_PAD
            q_ref[rows, lo:mid] = q_all[:, lo:mid].astype(BF16)
            q_ref[rows, mid:hi] = rotate(q_all[:, mid:hi]).astype(BF16)
            kc_ref[rows, lo:mid] = k_nope[:, hd * MLA_NOPE:(hd + 1) * MLA_NOPE].astype(BF16)
            kc_ref[rows, mid:hi] = k_rope
            vt_ref[hd, :MLA_VDIM, rows] = v_t[hd * MLA_VDIM:(hd + 1) * MLA_VDIM, :].astype(BF16)
            vt_ref[hd, MLA_VDIM:, rows] = ones


def _half_split(w_rope):
    even, odd = w_rope[..., 0::2], w_rope[..., 1::2]
    return jnp.concatenate([even, odd, odd, even], axis=-1)


def _mla_in(x, mod_all, layer, gain, positions, w_in, g_q, w_q_up, g_kv, w_kv_up):
    b, s, d = x.shape
    hq = MLA_HEADS * MLA_QK_PAD
    hn = MLA_HEADS * MLA_NOPE
    hv = MLA_HEADS * MLA_VDIM
    lat = MLA_Q_RANK + MLA_KV_RANK
    half = MLA_ROPE // 2
    win = jnp.concatenate([w_in[:, :lat], _half_split(w_in[:, lat:])], axis=-1).astype(BF16)
    wq3 = w_q_up.reshape(MLA_Q_RANK, MLA_HEADS, MLA_NOPE + MLA_ROPE)
    wq = jnp.concatenate([wq3[..., :MLA_NOPE], _half_split(wq3[..., MLA_NOPE:])], axis=-1)
    q_scale = (MLA_NOPE + MLA_ROPE) ** -0.5 * math.log2(math.e)
    wq = (wq * q_scale).reshape(MLA_Q_RANK, hq).astype(BF16)
    wkv3 = w_kv_up.reshape(MLA_KV_RANK, MLA_HEADS, MLA_NOPE + MLA_VDIM)
    wkn = wkv3[..., :MLA_NOPE].reshape(MLA_KV_RANK, hn).astype(BF16)
    wvt = wkv3[..., MLA_NOPE:].reshape(MLA_KV_RANK, hv).T.astype(BF16)
    win_w = win.shape[-1]
    inv_freq = ROPE_THETA ** (-jnp.arange(0, MLA_ROPE, 2, dtype=F32) / MLA_ROPE)
    freq_row = jnp.tile(inv_freq, LANE // half).reshape(1, LANE)
    quarter = jnp.full((half,), math.pi / 2, F32)
    phase_row = jnp.concatenate([jnp.zeros((2 * half,), F32), quarter, -quarter]).reshape(1, LANE)
    offsets = jnp.arange(IN_ROW_CHUNK, dtype=F32).reshape(IN_ROW_CHUNK, 1) * freq_row
    cos_off, sin_off = jnp.cos(offsets), jnp.sin(offsets)
    return pl.pallas_call(
        _mla_in_kernel,
        grid=(b, s // IN_TILE),
        in_specs=[
            _row_spec(d, IN_TILE), _const_spec((1, d)), _mod_spec(layer, 0), _mod_spec(layer, 1),
            pl.BlockSpec((None, 1, IN_TILE), lambda bi, i: (bi, 0, i)),
            _const_spec((1, LANE)), _const_spec((1, LANE)),
            _const_spec((IN_ROW_CHUNK, LANE)), _const_spec((IN_ROW_CHUNK, LANE)),
            _const_spec((d, win_w)), _const_spec((1, MLA_Q_RANK)), _const_spec((MLA_Q_RANK, hq)),
            _const_spec((1, MLA_KV_RANK)), _const_spec((MLA_KV_RANK, hn)),
            _const_spec((hv, MLA_KV_RANK)),
        ],
        out_specs=[_row_spec(hq, IN_TILE), _row_spec(hq, IN_TILE),
                   pl.BlockSpec((None, MLA_HEADS, MLA_VT_ROWS, IN_TILE), lambda bi, i: (bi, 0, 0, i))],
        out_shape=[jax.ShapeDtypeStruct((b, s, hq), BF16), jax.ShapeDtypeStruct((b, s, hq), BF16),
                   jax.ShapeDtypeStruct((b, MLA_HEADS, MLA_VT_ROWS, s), BF16)],
        scratch_shapes=[pltpu.VMEM((IN_TILE, LANE), F32)],
        compiler_params=_cparams(("arbitrary", "arbitrary")),
        name="mla_in",
    )(x, gain.reshape(1, d), mod_all, mod_all, positions.reshape(b, 1, s), freq_row, phase_row,
      cos_off, sin_off, win, g_q.reshape(1, -1), wq, g_kv.reshape(1, -1), wkn, wvt)


def _attn_kernel(q_ref, k_ref, vt_ref, o_ref, s0_ref, s1_ref, mb0_ref, mb1_ref, m_ref, acc_ref):
    tq, tk = ATT_Q_BLOCK, ATT_KV_BLOCK
    per_q = tq // tk
    assert per_q == 2
    n_q = q_ref.shape[0] // tq
    n_strips = tq // ATT_STRIP
    all_strips = tuple(range(n_strips))

    def lanes(c):
        return slice(c * ATT_STRIP, (c + 1) * ATT_STRIP)

    def scores(qi, kj, s_ref, mb_ref, diag_offset=None, strips=all_strips):
        start = pl.multiple_of(kj * tk, tk)
        k_blk = k_ref[pl.ds(start, tk), :]
        for c in strips:
            q_lo = c * ATT_STRIP
            if diag_offset is not None and diag_offset > q_lo + ATT_STRIP - 1:
                s_ref[c] = jnp.full((tk, ATT_STRIP), -jnp.inf, F32)
                mb_ref[:, lanes(c)] = jnp.full((1, ATT_STRIP), -jnp.inf, F32)
                continue
            q_strip = q_ref[pl.ds(pl.multiple_of(qi * tq + q_lo, ATT_STRIP), ATT_STRIP), :]
            s_t = _dot_nt(k_blk, q_strip)
            if diag_offset is not None and diag_offset + tk - 1 > q_lo:
                key = lax.broadcasted_iota(jnp.int32, (tk, ATT_STRIP), 0) + diag_offset
                qry = lax.broadcasted_iota(jnp.int32, (tk, ATT_STRIP), 1) + q_lo
                s_t = jnp.where(key <= qry, s_t, -jnp.inf)
            s_ref[c] = s_t
            mb_ref[:, lanes(c)] = jnp.max(s_t, axis=0, keepdims=True)

    def accumulate(kj, s_ref, mb_ref, strips=all_strips):
        start = pl.multiple_of(kj * tk, tk)
        vt_blk = vt_ref[:, pl.ds(start, tk)]
        for c in strips:
            cols = lanes(c)
            m_prev = m_ref[:, cols]
            m_new = jnp.maximum(m_prev, mb_ref[:, cols])
            p_t = jnp.exp2((s_ref[c] - m_new).astype(BF16))
            alpha = jnp.exp2(m_prev - m_new)
            acc_ref[:, cols] = alpha * acc_ref[:, cols] + _dot(vt_blk, p_t)
            m_ref[:, cols] = m_new

    def reset():
        m_ref[...] = jnp.full_like(m_ref, -jnp.inf)
        acc_ref[...] = jnp.zeros_like(acc_ref)

    def finish(qi, pending):
        accumulate(pending, s1_ref, mb1_ref)
        acc = acc_ref[...]
        o_t = acc[:MLA_VDIM, :] / acc[MLA_VDIM:MLA_VDIM + 1, :]
        o_ref[pl.ds(pl.multiple_of(qi * tq, tq), tq), :] = o_t.T.astype(BF16)

    def run_block(qi):
        first = per_q * qi
        reset()
        scores(qi, first + 1, s1_ref, mb1_ref, diag_offset=tk)
        accumulate(first, s0_ref, mb0_ref)
        if isinstance(qi, int) and qi == 0:
            return first + 1

        for c in all_strips:
            if tk <= c * ATT_STRIP + ATT_STRIP - 1:
                accumulate(first + 1, s1_ref, mb1_ref, strips=(c,))
            scores(qi, 0, s0_ref, mb0_ref, strips=(c,))

        def pair(p, pending):
            for c in all_strips:
                accumulate(pending, s0_ref, mb0_ref, strips=(c,))
                scores(qi, 2 * p + 1, s1_ref, mb1_ref, strips=(c,))
            for c in all_strips:
                accumulate(2 * p + 1, s1_ref, mb1_ref, strips=(c,))
                scores(qi, 2 * p + 2, s0_ref, mb0_ref, strips=(c,))
            return 2 * p + 2

        def quad(u, pending):
            return pair(2 * u + 1, pair(2 * u, pending))

        n_pairs = qi - 1
        pending = lax.fori_loop(0, n_pairs // 2, quad, 0)
        pending = lax.fori_loop(0, n_pairs % 2, lambda _, pend: pair(n_pairs - 1, pend), pending)
        last = 2 * qi - 1
        for c in all_strips:
            accumulate(pending, s0_ref, mb0_ref, strips=(c,))
            scores(qi, last, s1_ref, mb1_ref, strips=(c,))
        return last

    scores(0, 0, s0_ref, mb0_ref, diag_offset=0)
    pending0 = run_block(0)

    def q_block(qi, pending):
        scores(qi, per_q * qi, s0_ref, mb0_ref, diag_offset=0)
        finish(qi - 1, pending)
        return run_block(qi)

    pending_last = lax.fori_loop(1, n_q, q_block, pending0)
    finish(n_q - 1, pending_last)


def _attention(q, kc, vt):
    b, _, _, s = vt.shape
    tq, tk = ATT_Q_BLOCK, ATT_KV_BLOCK
    return pl.pallas_call(
        _attn_kernel,
        grid=(b, MLA_HEADS),
        in_specs=[
            pl.BlockSpec((None, s, MLA_QK_PAD), lambda bi, h: (bi, 0, h)),
            pl.BlockSpec((None, s, MLA_QK_PAD), lambda bi, h: (bi, 0, h)),
            pl.BlockSpec((None, None, MLA_VT_ROWS, s), lambda bi, h: (bi, h, 0, 0)),
        ],
        out_specs=pl.BlockSpec((None, s, MLA_VDIM), lambda bi, h: (bi, 0, h)),
        out_shape=jax.ShapeDtypeStruct((b, s, MLA_HEADS * MLA_VDIM), BF16),
        scratch_shapes=[pltpu.VMEM((tq // ATT_STRIP, tk, ATT_STRIP), F32),
                        pltpu.VMEM((tq // ATT_STRIP, tk, ATT_STRIP), F32),
                        pltpu.VMEM((1, tq), F32), pltpu.VMEM((1, tq), F32),
                        pltpu.VMEM((1, tq), F32), pltpu.VMEM((MLA_VT_ROWS, tq), F32)],
        compiler_params=_cparams(("arbitrary", "arbitrary")),
        name="mla_attention",
    )(q, kc, vt)


def kernel(x, c, positions, ada_w, ada_b, norm_mix, norm_ffn, gla_w_in, gla_w_gate, gla_b_gate,
           gla_g_out, gla_w_out, mla_w_in, mla_g_q, mla_w_q_up, mla_g_kv, mla_w_kv_up, mla_w_out,
           ffn_w_in, ffn_w_out, final_norm):
    batch = x.shape[0]
    depth = ada_w.shape[0]
    assert batch <= SUBLANE and ada_w.shape[-1] == N_MOD * D_MODEL
    c_pad = jnp.pad(c, ((0, SUBLANE - batch), (0, 0)))
    mod_all = _adaln(c_pad, ada_w, ada_b)

    for i in range(depth):
        j = i // 2
        if i % 2 == 0:
            q, k, v, r, la = _gla_in(x, mod_all, i, norm_mix[i], gla_w_in, j, gla_w_gate,
                                     gla_b_gate[j])
            mixed = _gla(q, k, v, r, la, gla_g_out[j])
            w_o = gla_w_out
        else:
            q, kc, vt = _mla_in(x, mod_all, i, norm_mix[i], positions, mla_w_in[j], mla_g_q[j],
                                mla_w_q_up[j], mla_g_kv[j], mla_w_kv_up[j])
            mixed = _attention(q, kc, vt)
            w_o = mla_w_out
        x, act = _mix_out_ffn_in(mixed, w_o, j, x, mod_all, i, norm_ffn[i], ffn_w_in)
        last = i == depth - 1
        x = _out_res(act, ffn_w_out, i, x, mod_all, i, 5, final_gain=final_norm if last else None)
    return x
```

```python
import functools
import math

import jax
import jax.numpy as jnp
from jax import lax
from jax.experimental import pallas as pl
from jax.experimental.pallas import tpu as pltpu

F32 = jnp.float32
BF16 = jnp.bfloat16

D_MODEL = 1024
EPS = 1e-6

GLA_HEADS = 4
GLA_DK = 128
GLA_DV = 256
GLA_GATE_RANK = 16
GLA_TAU = 16.0
GLA_CHUNK = 64
GLA_SUB = 16

MLA_HEADS = 8
MLA_NOPE = 128
MLA_ROPE = 64
MLA_VDIM = 128
MLA_Q_RANK = 384
MLA_KV_RANK = 256
ROPE_THETA = 10000.0
MLA_QK_PAD = 256
MLA_VT_ROWS = MLA_VDIM + 16

D_FF = 2816
N_MOD = 6

LANE = 128
SUBLANE = 8
ROW_TILE = 512
IN_TILE = 1024
GLA_TIME_BLOCK = 4096
ATT_Q_BLOCK = 1024
ATT_KV_BLOCK = 512
ATT_STRIP = 256
FFN_COL_CHUNK = 256
IN_ROW_CHUNK = 512
VMEM_LIMIT = 56 * 1024 * 1024


def _cparams(sem):
    return pltpu.CompilerParams(dimension_semantics=sem, vmem_limit_bytes=VMEM_LIMIT)


def _dot(a, b):
    return jnp.dot(a, b, preferred_element_type=F32)


def _dot_nt(a, b):
    return lax.dot_general(a, b, (((1,), (1,)), ((), ())), preferred_element_type=F32)


def _sigmoid(x):
    return 1.0 / (1.0 + jnp.exp(-x))


def _rms(x, g):
    return x * lax.rsqrt(jnp.mean(x * x, axis=-1, keepdims=True) + EPS) * g


def _norm_mod(x, g, shift, scale):
    return _rms(x, g * (1.0 + scale)) + shift


def _adaln_kernel(c_ref, w_ref, b_ref, o_ref):
    c = c_ref[...]
    o_ref[...] = _dot(c * _sigmoid(c), w_ref[...]) + b_ref[...]


def _adaln(c_pad, ada_w, ada_b):
    depth, d, n = ada_w.shape
    tn = 1536
    return pl.pallas_call(
        _adaln_kernel,
        grid=(depth, n // tn),
        in_specs=[
            pl.BlockSpec((SUBLANE, d), lambda l, j: (0, 0)),
            pl.BlockSpec((None, d, tn), lambda l, j: (l, 0, j)),
            pl.BlockSpec((None, 1, tn), lambda l, j: (l, 0, j)),
        ],
        out_specs=pl.BlockSpec((None, SUBLANE, tn), lambda l, j: (l, 0, j)),
        out_shape=jax.ShapeDtypeStruct((depth, SUBLANE, n), F32),
        compiler_params=_cparams(("arbitrary", "arbitrary")),
        name="adaln",
    )(c_pad, ada_w, ada_b.reshape(depth, 1, n))


def _row_spec(width, tile=ROW_TILE):
    return pl.BlockSpec((None, tile, width), lambda b, i: (b, i, 0))


def _mod_spec(layer, k):
    return pl.BlockSpec((None, SUBLANE, D_MODEL), lambda b, i: (layer, 0, k))


def _mod_row(ref):
    return ref[pl.ds(pl.program_id(0), 1), :]


def _const_spec(shape):
    return pl.BlockSpec(shape, lambda b, i: (0,) * len(shape))


def _layer_spec(shape, layer, col_block=0):
    return pl.BlockSpec((None,) + shape, lambda b, i: (layer, 0, col_block),
                        pipeline_mode=pl.Buffered(1))


def _gla_in_kernel(x_ref, g_ref, sh_ref, sc_ref, w_ref, wgate_ref, bgate_ref,
                   q_ref, k_ref, v_ref, r_ref, la_ref):
    hk = GLA_HEADS * GLA_DK
    hv = GLA_HEADS * GLA_DV
    shift, scale = _mod_row(sh_ref), _mod_row(sc_ref)
    chunks = [slice(r0, r0 + IN_ROW_CHUNK) for r0 in range(0, IN_TILE, IN_ROW_CHUNK)]
    hs = [_norm_mod(x_ref[rows, :], g_ref[...], shift, scale).astype(BF16) for rows in chunks]

    def proj(lo, width):
        w = w_ref[lo:lo + width, :].astype(BF16)
        return [_dot_nt(h, w) for h in hs]

    w_gate = wgate_ref[...].astype(BF16)
    for rows, q in zip(chunks, proj(0, hk)):
        q_ref[rows, :] = (q * (GLA_DK ** -0.5)).astype(BF16)
    for rows, k in zip(chunks, proj(hk, hk)):
        k_ref[rows, :] = k.astype(BF16)
    for rows, v in zip(chunks, proj(2 * hk, hv)):
        v_ref[rows, :] = v.astype(BF16)
    for rows, r in zip(chunks, proj(2 * hk + hv, hv)):
        r_ref[rows, :] = r.astype(BF16)
    for rows, low_rank in zip(chunks, proj(2 * hk + 2 * hv, GLA_GATE_RANK)):
        z = _dot(low_rank.astype(BF16), w_gate) + bgate_ref[...]
        log_sig = jnp.minimum(z, 0.0) - jnp.log(1.0 + jnp.exp(-jnp.abs(z)))
        la_ref[rows, :] = log_sig * (math.log2(math.e) / GLA_TAU)


def _gla_in(x, mod_all, layer, gain, w_in, j, w_gate, b_gate):
    b, s, d = x.shape
    hk = GLA_HEADS * GLA_DK
    hv = GLA_HEADS * GLA_DV
    return pl.pallas_call(
        _gla_in_kernel,
        grid=(b, s // IN_TILE),
        in_specs=[
            _row_spec(d, IN_TILE), _const_spec((1, d)), _mod_spec(layer, 0), _mod_spec(layer, 1),
            _layer_spec((w_in.shape[-1], d), j), _layer_spec((GLA_GATE_RANK, hk), j),
            _const_spec((1, hk)),
        ],
        out_specs=[_row_spec(hk, IN_TILE), _row_spec(hk, IN_TILE), _row_spec(hv, IN_TILE),
                   _row_spec(hv, IN_TILE), _row_spec(hk, IN_TILE)],
        out_shape=[
            jax.ShapeDtypeStruct((b, s, hk), BF16), jax.ShapeDtypeStruct((b, s, hk), BF16),
            jax.ShapeDtypeStruct((b, s, hv), BF16), jax.ShapeDtypeStruct((b, s, hv), BF16),
            jax.ShapeDtypeStruct((b, s, hk), F32),
        ],
        compiler_params=_cparams(("arbitrary", "arbitrary")),
        name="gla_in",
    )(x, gain.reshape(1, d), mod_all, mod_all, jnp.swapaxes(w_in, 1, 2), w_gate,
      b_gate.reshape(1, hk))


def _gla_kernel(q_ref, k_ref, v_ref, r_ref, la_ref, go_ref, o_ref, state_ref):
    c, sub = GLA_CHUNK, GLA_SUB
    nsub = c // sub
    assert nsub * c == 2 * LANE

    @pl.when(pl.program_id(2) == 0)
    def _():
        state_ref[...] = jnp.zeros_like(state_ref)

    row = lax.broadcasted_iota(jnp.int32, (c, c), 0)
    col = lax.broadcasted_iota(jnp.int32, (c, c), 1)
    tri = jnp.where(row >= col, 1.0, 0.0).astype(BF16)
    qrow = lax.broadcasted_iota(jnp.int32, (c, LANE), 0)
    lane = lax.broadcasted_iota(jnp.int32, (c, LANE), 1)
    causal = (lane % c) <= qrow
    keep_lo = causal & ((lane // c) == (qrow // sub))
    keep_hi = causal & ((lane // c) + LANE // c == (qrow // sub))

    n_chunks = GLA_TIME_BLOCK // c
    cs = [slice(ci * c, (ci + 1) * c) for ci in range(n_chunks)]

    bs = []
    for sl in cs:
        la = la_ref[sl, :]
        la_hi = la.astype(BF16)
        la_lo = (la - la_hi.astype(F32)).astype(BF16)
        bs.append(_dot(tri, la_hi) + _dot(tri, la_lo))

    qes, attns, kd_ts, decays = [], [], [], []
    for sl, b in zip(cs, bs):
        q = q_ref[sl, :].astype(F32)
        k = k_ref[sl, :].astype(F32)
        qes.append((q * jnp.exp2(b)).astype(BF16))
        q_parts, k_parts = [], []
        for i in range(nsub):
            rows = slice(i * sub, (i + 1) * sub)
            live = (i + 1) * sub
            if i == 0:
                q_parts.append(q[rows] * jnp.exp2(b[rows]))
                k_var = k[:live] * jnp.exp2(-b[:live])
            else:
                anchor = b[i * sub - 1:i * sub, :]
                q_parts.append(q[rows] * jnp.exp2(b[rows] - anchor))
                k_var = k[:live] * jnp.exp2(anchor - b[:live])
            k_parts.append(k_var)
            if live < c:
                k_parts.append(jnp.zeros((c - live, GLA_DK), F32))
        q_anch = jnp.concatenate(q_parts, axis=0).astype(BF16)
        k_stack = jnp.concatenate(k_parts, axis=0).astype(BF16)
        prod = _dot_nt(q_anch, k_stack)
        attns.append(jnp.where(keep_lo, prod[:, :LANE],
                               jnp.where(keep_hi, prod[:, LANE:], 0.0)).astype(BF16))
        b_last = b[c - 1:c, :]
        kd_ts.append((k * jnp.exp2(b_last - b)).T.astype(BF16))
        decays.append(jnp.broadcast_to(jnp.exp2(b_last), (LANE, GLA_DK)).T)

    state = state_ref[...]
    gain = go_ref[...]
    for ci, sl in enumerate(cs):
        v = v_ref[sl, :]
        v2 = jnp.concatenate([v, v], axis=0)
        o = _dot(qes[ci], state.astype(BF16)) + _dot(attns[ci], v2)
        decay = jnp.concatenate([decays[ci]] * (GLA_DV // LANE), axis=1)
        state = state * decay + _dot(kd_ts[ci], v)
        half_r = 0.5 * r_ref[sl, :].astype(F32)
        silu_r = half_r + half_r * jnp.tanh(half_r)
        o_ref[sl, :] = (_rms(o, gain) * silu_r).astype(BF16)
    state_ref[...] = state


def _gla(q, k, v, r, la, g_out):
    b, s, _ = q.shape
    t = GLA_TIME_BLOCK

    def spec(width):
        return pl.BlockSpec((None, t, width), lambda bi, h, ti: (bi, ti, h))

    return pl.pallas_call(
        _gla_kernel,
        grid=(b, GLA_HEADS, s // t),
        in_specs=[spec(GLA_DK), spec(GLA_DK), spec(GLA_DV), spec(GLA_DV), spec(GLA_DK),
                  pl.BlockSpec((1, GLA_DV), lambda bi, h, ti: (0, 0))],
        out_specs=spec(GLA_DV),
        out_shape=jax.ShapeDtypeStruct((b, s, GLA_HEADS * GLA_DV), BF16),
        scratch_shapes=[pltpu.VMEM((GLA_DK, GLA_DV), F32)],
        compiler_params=_cparams(("arbitrary", "arbitrary", "arbitrary")),
        name="gla_scan",
    )(q, k, v, r, la, g_out.reshape(1, GLA_DV))


def _out_res_kernel(a_ref, w_ref, x_ref, gate_ref, *rest, final_norm):
    o_ref = rest[-1]
    y = x_ref[...] + _mod_row(gate_ref) * _dot(a_ref[...], w_ref[...].astype(BF16))
    if final_norm:
        y = _rms(y, rest[0][...])
    o_ref[...] = y


def _out_res(a, w, j, x, mod_all, layer, gate_idx, final_gain=None):
    b, s, d = x.shape
    kdim = a.shape[-1]
    in_specs = [_row_spec(kdim, IN_TILE), _layer_spec((kdim, d), j), _row_spec(d, IN_TILE),
                _mod_spec(layer, gate_idx)]
    args = [a, w, x, mod_all]
    if final_gain is not None:
        in_specs.append(_const_spec((1, d)))
        args.append(final_gain.reshape(1, d))
    return pl.pallas_call(
        functools.partial(_out_res_kernel, final_norm=final_gain is not None),
        grid=(b, s // IN_TILE),
        in_specs=in_specs,
        out_specs=_row_spec(d, IN_TILE),
        out_shape=jax.ShapeDtypeStruct((b, s, d), F32),
        compiler_params=_cparams(("arbitrary", "arbitrary")),
        name="out_res",
    )(*args)


def _mix_out_ffn_in_kernel(m_ref, wo_ref, x_ref, g1_ref, g_ref, sh_ref, sc_ref, wg_ref, wu_ref,
                           x1_ref, a_ref):
    x1 = x_ref[...] + _mod_row(g1_ref) * _dot(m_ref[...], wo_ref[...].astype(BF16))
    x1_ref[...] = x1
    h = _norm_mod(x1, g_ref[...], _mod_row(sh_ref), _mod_row(sc_ref)).astype(BF16)
    for j in range(0, D_FF, FFN_COL_CHUNK):
        cols = slice(j, j + FFN_COL_CHUNK)
        gate = _dot(h, wg_ref[:, cols].astype(BF16))
        up = _dot(h, wu_ref[:, cols].astype(BF16))
        a_ref[:, cols] = (gate * _sigmoid(gate) * up).astype(BF16)


def _mix_out_ffn_in(mixed, w_o, j, x, mod_all, layer, gain, w_in):
    b, s, d = x.shape
    kdim = mixed.shape[-1]
    return pl.pallas_call(
        _mix_out_ffn_in_kernel,
        grid=(b, s // ROW_TILE),
        in_specs=[_row_spec(kdim), _layer_spec((kdim, d), j), _row_spec(d), _mod_spec(layer, 2),
                  _const_spec((1, d)), _mod_spec(layer, 3), _mod_spec(layer, 4),
                  _layer_spec((d, D_FF), layer, 0), _layer_spec((d, D_FF), layer, 1)],
        out_specs=[_row_spec(d), _row_spec(D_FF)],
        out_shape=[jax.ShapeDtypeStruct((b, s, d), F32), jax.ShapeDtypeStruct((b, s, D_FF), BF16)],
        compiler_params=_cparams(("arbitrary", "arbitrary")),
        name="mix_out_ffn_in",
    )(mixed, w_o, x, mod_all, gain.reshape(1, d), mod_all, mod_all, w_in, w_in)


def _mla_in_kernel(x_ref, g_ref, sh_ref, sc_ref, pos_ref, freq_ref, phase_ref, cosr_ref, sinr_ref,
                   win_ref, gq_ref,
                   wq_ref, gkv_ref, wkn_ref, wvt_ref, q_ref, kc_ref, vt_ref, table_ref):
    shift, scale = _mod_row(sh_ref), _mod_row(sc_ref)
    chunk_starts = range(0, IN_TILE, IN_ROW_CHUNK)

    pos_row = pos_ref[...]
    first_pos = pos_row[:, 0:1]
    ramp = lax.broadcasted_iota(jnp.int32, (1, IN_TILE), 1)
    consecutive = jnp.max(jnp.abs(pos_row - first_pos - ramp)) == 0

    @pl.when(consecutive)
    def _():
        for r0 in chunk_starts:
            a0 = (first_pos + r0).astype(F32) * freq_ref[...] + phase_ref[...]
            table_ref[r0:r0 + IN_ROW_CHUNK, :] = (jnp.cos(a0) * cosr_ref[...]
                                                  - jnp.sin(a0) * sinr_ref[...])

    @pl.when(jnp.logical_not(consecutive))
    def _():
        pos = jnp.broadcast_to(pos_row.astype(F32), (LANE, IN_TILE)).T
        table_ref[...] = jnp.cos(pos * freq_ref[...] + phase_ref[...])

    rope_lanes = lax.broadcasted_iota(jnp.int32, (IN_ROW_CHUNK, LANE), 1) < MLA_ROPE
    ones = jnp.ones((MLA_VT_ROWS - MLA_VDIM, IN_ROW_CHUNK), BF16)
    for r0 in chunk_starts:
        rows = slice(r0, r0 + IN_ROW_CHUNK)
        h = _norm_mod(x_ref[rows, :], g_ref[...], shift, scale).astype(BF16)
        proj = _dot(h, win_ref[...])
        cq = _rms(proj[:, :MLA_Q_RANK], gq_ref[...]).astype(BF16)
        ckv = _rms(proj[:, MLA_Q_RANK:MLA_Q_RANK + MLA_KV_RANK], gkv_ref[...]).astype(BF16)

        table = table_ref[rows, :]

        def rotate(t):
            u = t * table
            return u + pltpu.roll(u, MLA_ROPE, axis=1)

        k_rope = jnp.where(rope_lanes, rotate(proj[:, MLA_Q_RANK + MLA_KV_RANK:]), 0.0)
        k_rope = k_rope.astype(BF16)
        v_t = _dot_nt(wvt_ref[...], ckv)
        k_nope = _dot(ckv, wkn_ref[...])
        q_all = _dot(cq, wq_ref[...])
        for hd in range(MLA_HEADS):
            lo = hd * MLA_QK_PAD
            mid = lo + MLA_NOPE
            hi = lo + MLA_QK_PAD
            q_ref[rows, lo:mid] = q_all[:, lo:mid].astype(BF16)
            q_ref[rows, mid:hi] = rotate(q_all[:, mid:hi]).astype(BF16)
            kc_ref[rows, lo:mid] = k_nope[:, hd * MLA_NOPE:(hd + 1) * MLA_NOPE].astype(BF16)
            kc_ref[rows, mid:hi] = k_rope
            vt_ref[hd, :MLA_VDIM, rows] = v_t[hd * MLA_VDIM:(hd + 1) * MLA_VDIM, :].astype(BF16)
            vt_ref[hd, MLA_VDIM:, rows] = ones


def _half_split(w_rope):
    even, odd = w_rope[..., 0::2], w_rope[..., 1::2]
    return jnp.concatenate([even, odd, odd, even], axis=-1)


def _mla_in(x, mod_all, layer, gain, positions, w_in, g_q, w_q_up, g_kv, w_kv_up):
    b, s, d = x.shape
    hq = MLA_HEADS * MLA_QK_PAD
    hn = MLA_HEADS * MLA_NOPE
    hv = MLA_HEADS * MLA_VDIM
    lat = MLA_Q_RANK + MLA_KV_RANK
    half = MLA_ROPE // 2
    win = jnp.concatenate([w_in[:, :lat], _half_split(w_in[:, lat:])], axis=-1).astype(BF16)
    wq3 = w_q_up.reshape(MLA_Q_RANK, MLA_HEADS, MLA_NOPE + MLA_ROPE)
    wq = jnp.concatenate([wq3[..., :MLA_NOPE], _half_split(wq3[..., MLA_NOPE:])], axis=-1)
    q_scale = (MLA_NOPE + MLA_ROPE) ** -0.5 * math.log2(math.e)
    wq = (wq * q_scale).reshape(MLA_Q_RANK, hq).astype(BF16)
    wkv3 = w_kv_up.reshape(MLA_KV_RANK, MLA_HEADS, MLA_NOPE + MLA_VDIM)
    wkn = wkv3[..., :MLA_NOPE].reshape(MLA_KV_RANK, hn).astype(BF16)
    wvt = wkv3[..., MLA_NOPE:].reshape(MLA_KV_RANK, hv).T.astype(BF16)
    win_w = win.shape[-1]
    inv_freq = ROPE_THETA ** (-jnp.arange(0, MLA_ROPE, 2, dtype=F32) / MLA_ROPE)
    freq_row = jnp.tile(inv_freq, LANE // half).reshape(1, LANE)
    quarter = jnp.full((half,), math.pi / 2, F32)
    phase_row = jnp.concatenate([jnp.zeros((2 * half,), F32), quarter, -quarter]).reshape(1, LANE)
    offsets = jnp.arange(IN_ROW_CHUNK, dtype=F32).reshape(IN_ROW_CHUNK, 1) * freq_row
    cos_off, sin_off = jnp.cos(offsets), jnp.sin(offsets)
    return pl.pallas_call(
        _mla_in_kernel,
        grid=(b, s // IN_TILE),
        in_specs=[
            _row_spec(d, IN_TILE), _const_spec((1, d)), _mod_spec(layer, 0), _mod_spec(layer, 1),
            pl.BlockSpec((None, 1, IN_TILE), lambda bi, i: (bi, 0, i)),
            _const_spec((1, LANE)), _const_spec((1, LANE)),
            _const_spec((IN_ROW_CHUNK, LANE)), _const_spec((IN_ROW_CHUNK, LANE)),
            _const_spec((d, win_w)), _const_spec((1, MLA_Q_RANK)), _const_spec((MLA_Q_RANK, hq)),
            _const_spec((1, MLA_KV_RANK)), _const_spec((MLA_KV_RANK, hn)),
            _const_spec((hv, MLA_KV_RANK)),
        ],
        out_specs=[_row_spec(hq, IN_TILE), _row_spec(hq, IN_TILE),
                   pl.BlockSpec((None, MLA_HEADS, MLA_VT_ROWS, IN_TILE), lambda bi, i: (bi, 0, 0, i))],
        out_shape=[jax.ShapeDtypeStruct((b, s, hq), BF16), jax.ShapeDtypeStruct((b, s, hq), BF16),
                   jax.ShapeDtypeStruct((b, MLA_HEADS, MLA_VT_ROWS, s), BF16)],
        scratch_shapes=[pltpu.VMEM((IN_TILE, LANE), F32)],
        compiler_params=_cparams(("arbitrary", "arbitrary")),
        name="mla_in",
    )(x, gain.reshape(1, d), mod_all, mod_all, positions.reshape(b, 1, s), freq_row, phase_row,
      cos_off, sin_off, win, g_q.reshape(1, -1), wq, g_kv.reshape(1, -1), wkn, wvt)


def _attn_kernel(q_ref, k_ref, vt_ref, o_ref, s0_ref, s1_ref, mb0_ref, mb1_ref, m_ref, acc_ref):
    tq, tk = ATT_Q_BLOCK, ATT_KV_BLOCK
    per_q = tq // tk
    assert per_q == 2
    n_q = q_ref.shape[0] // tq
    n_strips = tq // ATT_STRIP
    all_strips = tuple(range(n_strips))

    def lanes(c):
        return slice(c * ATT_STRIP, (c + 1) * ATT_STRIP)

    def scores(qi, kj, s_ref, mb_ref, diag_offset=None, strips=all_strips):
        start = pl.multiple_of(kj * tk, tk)
        k_blk = k_ref[pl.ds(start, tk), :]
        for c in strips:
            q_lo = c * ATT_STRIP
            if diag_offset is not None and diag_offset > q_lo + ATT_STRIP - 1:
                s_ref[c] = jnp.full((tk, ATT_STRIP), -jnp.inf, F32)
                mb_ref[:, lanes(c)] = jnp.full((1, ATT_STRIP), -jnp.inf, F32)
                continue
            q_strip = q_ref[pl.ds(pl.multiple_of(qi * tq + q_lo, ATT_STRIP), ATT_STRIP), :]
            s_t = _dot_nt(k_blk, q_strip)
            if diag_offset is not None and diag_offset + tk - 1 > q_lo:
                key = lax.broadcasted_iota(jnp.int32, (tk, ATT_STRIP), 0) + diag_offset
                qry = lax.broadcasted_iota(jnp.int32, (tk, ATT_STRIP), 1) + q_lo
                s_t = jnp.where(key <= qry, s_t, -jnp.inf)
            s_ref[c] = s_t
            mb_ref[:, lanes(c)] = jnp.max(s_t, axis=0, keepdims=True)

    def accumulate(kj, s_ref, mb_ref, strips=all_strips):
        start = pl.multiple_of(kj * tk, tk)
        vt_blk = vt_ref[:, pl.ds(start, tk)]
        for c in strips:
            cols = lanes(c)
            m_prev = m_ref[:, cols]
            m_new = jnp.maximum(m_prev, mb_ref[:, cols])
            p_t = jnp.exp2((s_ref[c] - m_new).astype(BF16))
            alpha = jnp.exp2(m_prev - m_new)
            acc_ref[:, cols] = alpha * acc_ref[:, cols] + _dot(vt_blk, p_t)
            m_ref[:, cols] = m_new

    def reset():
        m_ref[...] = jnp.full_like(m_ref, -jnp.inf)
        acc_ref[...] = jnp.zeros_like(acc_ref)

    def finish(qi, pending):
        accumulate(pending, s1_ref, mb1_ref)
        acc = acc_ref[...]
        o_t = acc[:MLA_VDIM, :] / acc[MLA_VDIM:MLA_VDIM + 1, :]
        o_ref[pl.ds(pl.multiple_of(qi * tq, tq), tq), :] = o_t.T.astype(BF16)

    def run_block(qi):
        first = per_q * qi
        reset()
        scores(qi, first + 1, s1_ref, mb1_ref, diag_offset=tk)
        accumulate(first, s0_ref, mb0_ref)
        if isinstance(qi, int) and qi == 0:
            return first + 1

        for c in all_strips:
            if tk <= c * ATT_STRIP + ATT_STRIP - 1:
                accumulate(first + 1, s1_ref, mb1_ref, strips=(c,))
            scores(qi, 0, s0_ref, mb0_ref, strips=(c,))

        def pair(p, pending):
            for c in all_strips:
                accumulate(pending, s0_ref, mb0_ref, strips=(c,))
                scores(qi, 2 * p + 1, s1_ref, mb1_ref, strips=(c,))
            for c in all_strips:
                accumulate(2 * p + 1, s1_ref, mb1_ref, strips=(c,))
                scores(qi, 2 * p + 2, s0_ref, mb0_ref, strips=(c,))
            return 2 * p + 2

        def quad(u, pending):
            return pair(2 * u + 1, pair(2 * u, pending))

        n_pairs = qi - 1
        pending = lax.fori_loop(0, n_pairs // 2, quad, 0)
        pending = lax.fori_loop(0, n_pairs % 2, lambda _, pend: pair(n_pairs - 1, pend), pending)
        last = 2 * qi - 1
        for c in all_strips:
            accumulate(pending, s0_ref, mb0_ref, strips=(c,))
            scores(qi, last, s1_ref, mb1_ref, strips=(c,))
        return last

    scores(0, 0, s0_ref, mb0_ref, diag_offset=0)
    pending0 = run_block(0)

    def q_block(qi, pending):
        scores(qi, per_q * qi, s0_ref, mb0_ref, diag_offset=0)
        finish(qi - 1, pending)
        return run_block(qi)

    pending_last = lax.fori_loop(1, n_q, q_block, pending0)
    finish(n_q - 1, pending_last)


def _attention(q, kc, vt):
    b, _, _, s = vt.shape
    tq, tk = ATT_Q_BLOCK, ATT_KV_BLOCK
    return pl.pallas_call(
        _attn_kernel,
        grid=(b, MLA_HEADS),
        in_specs=[
            pl.BlockSpec((None, s, MLA_QK_PAD), lambda bi, h: (bi, 0, h)),
            pl.BlockSpec((None, s, MLA_QK_PAD), lambda bi, h: (bi, 0, h)),
            pl.BlockSpec((None, None, MLA_VT_ROWS, s), lambda bi, h: (bi, h, 0, 0)),
        ],
        out_specs=pl.BlockSpec((None, s, MLA_VDIM), lambda bi, h: (bi, 0, h)),
        out_shape=jax.ShapeDtypeStruct((b, s, MLA_HEADS * MLA_VDIM), BF16),
        scratch_shapes=[pltpu.VMEM((tq // ATT_STRIP, tk, ATT_STRIP), F32),
                        pltpu.VMEM((tq // ATT_STRIP, tk, ATT_STRIP), F32),
                        pltpu.VMEM((1, tq), F32), pltpu.VMEM((1, tq), F32),
                        pltpu.VMEM((1, tq), F32), pltpu.VMEM((MLA_VT_ROWS, tq), F32)],
        compiler_params=_cparams(("arbitrary", "arbitrary")),
        name="mla_attention",
    )(q, kc, vt)


def kernel(x, c, positions, ada_w, ada_b, norm_mix, norm_ffn, gla_w_in, gla_w_gate, gla_b_gate,
           gla_g_out, gla_w_out, mla_w_in, mla_g_q, mla_w_q_up, mla_g_kv, mla_w_kv_up, mla_w_out,
           ffn_w_in, ffn_w_out, final_norm):
    batch = x.shape[0]
    depth = ada_w.shape[0]
    assert batch <= SUBLANE and ada_w.shape[-1] == N_MOD * D_MODEL
    c_pad = jnp.pad(c, ((0, SUBLANE - batch), (0, 0)))
    mod_all = _adaln(c_pad, ada_w, ada_b)

    for i in range(depth):
        j = i // 2
        if i % 2 == 0:
            q, k, v, r, la = _gla_in(x, mod_all, i, norm_mix[i], gla_w_in, j, gla_w_gate,
                                     gla_b_gate[j])
            mixed = _gla(q, k, v, r, la, gla_g_out[j])
            w_o = gla_w_out
        else:
            q, kc, vt = _mla_in(x, mod_all, i, norm_mix[i], positions, mla_w_in[j], mla_g_q[j],
                                mla_w_q_up[j], mla_g_kv[j], mla_w_kv_up[j])
            mixed = _attention(q, kc, vt)
            w_o = mla_w_out
        x, act = _mix_out_ffn_in(mixed, w_o, j, x, mod_all, i, norm_ffn[i], ffn_w_in)
        last = i == depth - 1
        x = _out_res(act, ffn_w_out, i, x, mod_all, i, 5, final_gain=final_norm if last else None)
    return x
```

```python
import functools
import math

import jax
import jax.numpy as jnp
from jax import lax
from jax.experimental import pallas as pl
from jax.experimental.pallas import tpu as pltpu

F32 = jnp.float32
BF16 = jnp.bfloat16

D_MODEL = 1024
EPS = 1e-6

GLA_HEADS = 4
GLA_DK = 128
GLA_DV = 256
GLA_GATE_RANK = 16
GLA_TAU = 16.0
GLA_CHUNK = 64
GLA_SUB = 16

MLA_HEADS = 8
MLA_NOPE = 128
MLA_ROPE = 64
MLA_VDIM = 128
MLA_Q_RANK = 384
MLA_KV_RANK = 256
ROPE_THETA = 10000.0
MLA_QK_PAD = 256
MLA_VT_ROWS = MLA_VDIM + 16

D_FF = 2816
N_MOD = 6

LANE = 128
SUBLANE = 8
ROW_TILE = 1024
IN_TILE = 1024
GLA_TIME_BLOCK = 4096
ATT_Q_BLOCK = 1024
ATT_KV_BLOCK = 512
ATT_STRIP = 256
FFN_COL_CHUNK = 256
IN_ROW_CHUNK = 512
VMEM_LIMIT = 60 * 1024 * 1024


def _cparams(sem):
    return pltpu.CompilerParams(dimension_semantics=sem, vmem_limit_bytes=VMEM_LIMIT)


def _dot(a, b):
    return jnp.dot(a, b, preferred_element_type=F32)


def _dot_nt(a, b):
    return lax.dot_general(a, b, (((1,), (1,)), ((), ())), preferred_element_type=F32)


def _sigmoid(x):
    return 1.0 / (1.0 + jnp.exp(-x))


def _rms(x, g):
    return x * lax.rsqrt(jnp.mean(x * x, axis=-1, keepdims=True) + EPS) * g


def _norm_mod(x, g, shift, scale):
    return _rms(x, g * (1.0 + scale)) + shift


def _adaln_kernel(c_ref, w_ref, b_ref, o_ref):
    c = c_ref[...]
    o_ref[...] = _dot(c * _sigmoid(c), w_ref[...]) + b_ref[...]


def _adaln(c_pad, ada_w, ada_b):
    depth, d, n = ada_w.shape
    tn = 1536
    return pl.pallas_call(
        _adaln_kernel,
        grid=(depth, n // tn),
        in_specs=[
            pl.BlockSpec((SUBLANE, d), lambda l, j: (0, 0)),
            pl.BlockSpec((None, d, tn), lambda l, j: (l, 0, j)),
            pl.BlockSpec((None, 1, tn), lambda l, j: (l, 0, j)),
        ],
        out_specs=pl.BlockSpec((None, SUBLANE, tn), lambda l, j: (l, 0, j)),
        out_shape=jax.ShapeDtypeStruct((depth, SUBLANE, n), F32),
        compiler_params=_cparams(("arbitrary", "arbitrary")),
        name="adaln",
    )(c_pad, ada_w, ada_b.reshape(depth, 1, n))


def _row_spec(width, tile=ROW_TILE):
    return pl.BlockSpec((None, tile, width), lambda b, i: (b, i, 0))


def _mod_spec(layer, k):
    return pl.BlockSpec((None, SUBLANE, D_MODEL), lambda b, i: (layer, 0, k))


def _mod_row(ref):
    return ref[pl.ds(pl.program_id(0), 1), :]


def _const_spec(shape):
    return pl.BlockSpec(shape, lambda b, i: (0,) * len(shape))


def _layer_spec(shape, layer, col_block=0):
    return pl.BlockSpec((None,) + shape, lambda b, i: (layer, 0, col_block),
                        pipeline_mode=pl.Buffered(1))


def _gla_in_kernel(x_ref, g_ref, sh_ref, sc_ref, w_ref, wgate_ref, bgate_ref,
                   q_ref, k_ref, v_ref, r_ref, la_ref):
    hk = GLA_HEADS * GLA_DK
    hv = GLA_HEADS * GLA_DV
    shift, scale = _mod_row(sh_ref), _mod_row(sc_ref)
    chunks = [slice(r0, r0 + IN_ROW_CHUNK) for r0 in range(0, IN_TILE, IN_ROW_CHUNK)]
    hs = [_norm_mod(x_ref[rows, :], g_ref[...], shift, scale).astype(BF16) for rows in chunks]

    def proj(lo, width):
        w = w_ref[lo:lo + width, :].astype(BF16)
        return [_dot_nt(h, w) for h in hs]

    w_gate = wgate_ref[...].astype(BF16)
    for rows, q in zip(chunks, proj(0, hk)):
        q_ref[rows, :] = (q * (GLA_DK ** -0.5)).astype(BF16)
    for rows, k in zip(chunks, proj(hk, hk)):
        k_ref[rows, :] = k.astype(BF16)
    for rows, v in zip(chunks, proj(2 * hk, hv)):
        v_ref[rows, :] = v.astype(BF16)
    for rows, r in zip(chunks, proj(2 * hk + hv, hv)):
        r_ref[rows, :] = r.astype(BF16)
    for rows, low_rank in zip(chunks, proj(2 * hk + 2 * hv, GLA_GATE_RANK)):
        z = _dot(low_rank.astype(BF16), w_gate) + bgate_ref[...]
        log_sig = jnp.minimum(z, 0.0) - jnp.log(1.0 + jnp.exp(-jnp.abs(z)))
        la_ref[rows, :] = log_sig * (math.log2(math.e) / GLA_TAU)


def _gla_in(x, mod_all, layer, gain, w_in, j, w_gate, b_gate):
    b, s, d = x.shape
    hk = GLA_HEADS * GLA_DK
    hv = GLA_HEADS * GLA_DV
    return pl.pallas_call(
        _gla_in_kernel,
        grid=(b, s // IN_TILE),
        in_specs=[
            _row_spec(d, IN_TILE), _const_spec((1, d)), _mod_spec(layer, 0), _mod_spec(layer, 1),
            _layer_spec((w_in.shape[-1], d), j), _layer_spec((GLA_GATE_RANK, hk), j),
            _const_spec((1, hk)),
        ],
        out_specs=[_row_spec(hk, IN_TILE), _row_spec(hk, IN_TILE), _row_spec(hv, IN_TILE),
                   _row_spec(hv, IN_TILE), _row_spec(hk, IN_TILE)],
        out_shape=[
            jax.ShapeDtypeStruct((b, s, hk), BF16), jax.ShapeDtypeStruct((b, s, hk), BF16),
            jax.ShapeDtypeStruct((b, s, hv), BF16), jax.ShapeDtypeStruct((b, s, hv), BF16),
            jax.ShapeDtypeStruct((b, s, hk), F32),
        ],
        compiler_params=_cparams(("arbitrary", "arbitrary")),
        name="gla_in",
    )(x, gain.reshape(1, d), mod_all, mod_all, jnp.swapaxes(w_in, 1, 2), w_gate,
      b_gate.reshape(1, hk))


def _gla_kernel(q_ref, k_ref, v_ref, r_ref, la_ref, go_ref, o_ref, state_ref):
    c, sub = GLA_CHUNK, GLA_SUB
    nsub = c // sub
    assert nsub * c == 2 * LANE

    @pl.when(pl.program_id(2) == 0)
    def _():
        state_ref[...] = jnp.zeros_like(state_ref)

    row = lax.broadcasted_iota(jnp.int32, (c, c), 0)
    col = lax.broadcasted_iota(jnp.int32, (c, c), 1)
    tri = jnp.where(row >= col, 1.0, 0.0).astype(BF16)
    qrow = lax.broadcasted_iota(jnp.int32, (c, LANE), 0)
    lane = lax.broadcasted_iota(jnp.int32, (c, LANE), 1)
    causal = (lane % c) <= qrow
    keep_lo = causal & ((lane // c) == (qrow // sub))
    keep_hi = causal & ((lane // c) + LANE // c == (qrow // sub))

    n_chunks = GLA_TIME_BLOCK // c
    cs = [slice(ci * c, (ci + 1) * c) for ci in range(n_chunks)]

    bs = []
    for sl in cs:
        la = la_ref[sl, :]
        la_hi = la.astype(BF16)
        la_lo = (la - la_hi.astype(F32)).astype(BF16)
        bs.append(_dot(tri, la_hi) + _dot(tri, la_lo))

    qes, attns, kd_ts, decays = [], [], [], []
    for sl, b in zip(cs, bs):
        q = q_ref[sl, :].astype(F32)
        k = k_ref[sl, :].astype(F32)
        qes.append((q * jnp.exp2(b)).astype(BF16))
        q_parts, k_parts = [], []
        for i in range(nsub):
            rows = slice(i * sub, (i + 1) * sub)
            live = (i + 1) * sub
            if i == 0:
                q_parts.append(q[rows] * jnp.exp2(b[rows]))
                k_var = k[:live] * jnp.exp2(-b[:live])
            else:
                anchor = b[i * sub - 1:i * sub, :]
                q_parts.append(q[rows] * jnp.exp2(b[rows] - anchor))
                k_var = k[:live] * jnp.exp2(anchor - b[:live])
            k_parts.append(k_var)
            if live < c:
                k_parts.append(jnp.zeros((c - live, GLA_DK), F32))
        q_anch = jnp.concatenate(q_parts, axis=0).astype(BF16)
        k_stack = jnp.concatenate(k_parts, axis=0).astype(BF16)
        prod = _dot_nt(q_anch, k_stack)
        attns.append(jnp.where(keep_lo, prod[:, :LANE],
                               jnp.where(keep_hi, prod[:, LANE:], 0.0)).astype(BF16))
        b_last = b[c - 1:c, :]
        kd_ts.append((k * jnp.exp2(b_last - b)).T.astype(BF16))
        decays.append(jnp.broadcast_to(jnp.exp2(b_last), (LANE, GLA_DK)).T)

    state = state_ref[...]
    gain = go_ref[...]
    for ci, sl in enumerate(cs):
        v = v_ref[sl, :]
        v2 = jnp.concatenate([v, v], axis=0)
        o = _dot(qes[ci], state.astype(BF16)) + _dot(attns[ci], v2)
        decay = jnp.concatenate([decays[ci]] * (GLA_DV // LANE), axis=1)
        state = state * decay + _dot(kd_ts[ci], v)
        half_r = 0.5 * r_ref[sl, :].astype(F32)
        silu_r = half_r + half_r * jnp.tanh(half_r)
        o_ref[sl, :] = (_rms(o, gain) * silu_r).astype(BF16)
    state_ref[...] = state


def _gla(q, k, v, r, la, g_out):
    b, s, _ = q.shape
    t = GLA_TIME_BLOCK

    def spec(width):
        return pl.BlockSpec((None, t, width), lambda bi, h, ti: (bi, ti, h))

    return pl.pallas_call(
        _gla_kernel,
        grid=(b, GLA_HEADS, s // t),
        in_specs=[spec(GLA_DK), spec(GLA_DK), spec(GLA_DV), spec(GLA_DV), spec(GLA_DK),
                  pl.BlockSpec((1, GLA_DV), lambda bi, h, ti: (0, 0))],
        out_specs=spec(GLA_DV),
        out_shape=jax.ShapeDtypeStruct((b, s, GLA_HEADS * GLA_DV), BF16),
        scratch_shapes=[pltpu.VMEM((GLA_DK, GLA_DV), F32)],
        compiler_params=_cparams(("arbitrary", "arbitrary", "arbitrary")),
        name="gla_scan",
    )(q, k, v, r, la, g_out.reshape(1, GLA_DV))


def _out_res_kernel(a_ref, w_ref, x_ref, gate_ref, *rest, final_norm):
    o_ref = rest[-1]
    y = x_ref[...] + _mod_row(gate_ref) * _dot(a_ref[...], w_ref[...].astype(BF16))
    if final_norm:
        y = _rms(y, rest[0][...])
    o_ref[...] = y


def _out_res(a, w, j, x, mod_all, layer, gate_idx, final_gain=None):
    b, s, d = x.shape
    kdim = a.shape[-1]
    in_specs = [_row_spec(kdim, IN_TILE), _layer_spec((kdim, d), j), _row_spec(d, IN_TILE),
                _mod_spec(layer, gate_idx)]
    args = [a, w, x, mod_all]
    if final_gain is not None:
        in_specs.append(_const_spec((1, d)))
        args.append(final_gain.reshape(1, d))
    return pl.pallas_call(
        functools.partial(_out_res_kernel, final_norm=final_gain is not None),
        grid=(b, s // IN_TILE),
        in_specs=in_specs,
        out_specs=_row_spec(d, IN_TILE),
        out_shape=jax.ShapeDtypeStruct((b, s, d), F32),
        compiler_params=_cparams(("arbitrary", "arbitrary")),
        name="out_res",
    )(*args)


def _mix_out_ffn_in_kernel(m_ref, wo_ref, x_ref, g1_ref, g_ref, sh_ref, sc_ref, wg_ref, wu_ref,
                           x1_ref, a_ref):
    x1 = x_ref[...] + _mod_row(g1_ref) * _dot(m_ref[...], wo_ref[...].astype(BF16))
    x1_ref[...] = x1
    h = _norm_mod(x1, g_ref[...], _mod_row(sh_ref), _mod_row(sc_ref)).astype(BF16)
    for j in range(0, D_FF, FFN_COL_CHUNK):
        cols = slice(j, j + FFN_COL_CHUNK)
        gate = _dot(h, wg_ref[:, cols].astype(BF16))
        up = _dot(h, wu_ref[:, cols].astype(BF16))
        a_ref[:, cols] = (gate * _sigmoid(gate) * up).astype(BF16)


def _mix_out_ffn_in(mixed, w_o, j, x, mod_all, layer, gain, w_in):
    b, s, d = x.shape
    kdim = mixed.shape[-1]
    return pl.pallas_call(
        _mix_out_ffn_in_kernel,
        grid=(b, s // ROW_TILE),
        in_specs=[_row_spec(kdim), _layer_spec((kdim, d), j), _row_spec(d), _mod_spec(layer, 2),
                  _const_spec((1, d)), _mod_spec(layer, 3), _mod_spec(layer, 4),
                  _layer_spec((d, D_FF), layer, 0), _layer_spec((d, D_FF), layer, 1)],
        out_specs=[_row_spec(d), _row_spec(D_FF)],
        out_shape=[jax.ShapeDtypeStruct((b, s, d), F32), jax.ShapeDtypeStruct((b, s, D_FF), BF16)],
        compiler_params=_cparams(("arbitrary", "arbitrary")),
        name="mix_out_ffn_in",
    )(mixed, w_o, x, mod_all, gain.reshape(1, d), mod_all, mod_all, w_in, w_in)


def _mla_in_kernel(x_ref, g_ref, sh_ref, sc_ref, pos_ref, freq_ref, phase_ref, cosr_ref, sinr_ref,
                   win_ref, gq_ref,
                   wq_ref, gkv_ref, wkn_ref, wvt_ref, q_ref, kc_ref, vt_ref, table_ref):
    shift, scale = _mod_row(sh_ref), _mod_row(sc_ref)
    chunk_starts = range(0, IN_TILE, IN_ROW_CHUNK)

    pos_row = pos_ref[...]
    first_pos = pos_row[:, 0:1]
    ramp = lax.broadcasted_iota(jnp.int32, (1, IN_TILE), 1)
    consecutive = jnp.max(jnp.abs(pos_row - first_pos - ramp)) == 0

    @pl.when(consecutive)
    def _():
        for r0 in chunk_starts:
            a0 = (first_pos + r0).astype(F32) * freq_ref[...] + phase_ref[...]
            table_ref[r0:r0 + IN_ROW_CHUNK, :] = (jnp.cos(a0) * cosr_ref[...]
                                                  - jnp.sin(a0) * sinr_ref[...])

    @pl.when(jnp.logical_not(consecutive))
    def _():
        pos = jnp.broadcast_to(pos_row.astype(F32), (LANE, IN_TILE)).T
        table_ref[...] = jnp.cos(pos * freq_ref[...] + phase_ref[...])

    rope_lanes = lax.broadcasted_iota(jnp.int32, (IN_ROW_CHUNK, LANE), 1) < MLA_ROPE
    ones = jnp.ones((MLA_VT_ROWS - MLA_VDIM, IN_ROW_CHUNK), BF16)
    for r0 in chunk_starts:
        rows = slice(r0, r0 + IN_ROW_CHUNK)
        h = _norm_mod(x_ref[rows, :], g_ref[...], shift, scale).astype(BF16)
        proj = _dot(h, win_ref[...])
        cq = _rms(proj[:, :MLA_Q_RANK], gq_ref[...]).astype(BF16)
        ckv = _rms(proj[:, MLA_Q_RANK:MLA_Q_RANK + MLA_KV_RANK], gkv_ref[...]).astype(BF16)

        table = table_ref[rows, :]

        def rotate(t):
            u = t * table
            return u + pltpu.roll(u, MLA_ROPE, axis=1)

        k_rope = jnp.where(rope_lanes, rotate(proj[:, MLA_Q_RANK + MLA_KV_RANK:]), 0.0)
        k_rope = k_rope.astype(BF16)
        v_t = _dot_nt(wvt_ref[...], ckv)
        k_nope = _dot(ckv, wkn_ref[...])
        q_all = _dot(cq, wq_ref[...])
        for hd in range(MLA_HEADS):
            lo = hd * MLA_QK_PAD
            mid = lo + MLA_NOPE
            hi = lo + MLA_QK_PAD
            q_ref[rows, lo:mid] = q_all[:, lo:mid].astype(BF16)
            q_ref[rows, mid:hi] = rotate(q_all[:, mid:hi]).astype(BF16)
            kc_ref[rows, lo:mid] = k_nope[:, hd * MLA_NOPE:(hd + 1) * MLA_NOPE].astype(BF16)
            kc_ref[rows, mid:hi] = k_rope
            vt_ref[hd, :MLA_VDIM, rows] = v_t[hd * MLA_VDIM:(hd + 1) * MLA_VDIM, :].astype(BF16)
            vt_ref[hd, MLA_VDIM:, rows] = ones


def _half_split(w_rope):
    even, odd = w_rope[..., 0::2], w_rope[..., 1::2]
    return jnp.concatenate([even, odd, odd, even], axis=-1)


def _mla_in(x, mod_all, layer, gain, positions, w_in, g_q, w_q_up, g_kv, w_kv_up):
    b, s, d = x.shape
    hq = MLA_HEADS * MLA_QK_PAD
    hn = MLA_HEADS * MLA_NOPE
    hv = MLA_HEADS * MLA_VDIM
    lat = MLA_Q_RANK + MLA_KV_RANK
    half = MLA_ROPE // 2
    win = jnp.concatenate([w_in[:, :lat], _half_split(w_in[:, lat:])], axis=-1).astype(BF16)
    wq3 = w_q_up.reshape(MLA_Q_RANK, MLA_HEADS, MLA_NOPE + MLA_ROPE)
    wq = jnp.concatenate([wq3[..., :MLA_NOPE], _half_split(wq3[..., MLA_NOPE:])], axis=-1)
    q_scale = (MLA_NOPE + MLA_ROPE) ** -0.5 * math.log2(math.e)
    wq = (wq * q_scale).reshape(MLA_Q_RANK, hq).astype(BF16)
    wkv3 = w_kv_up.reshape(MLA_KV_RANK, MLA_HEADS, MLA_NOPE + MLA_VDIM)
    wkn = wkv3[..., :MLA_NOPE].reshape(MLA_KV_RANK, hn).astype(BF16)
    wvt = wkv3[..., MLA_NOPE:].reshape(MLA_KV_RANK, hv).T.astype(BF16)
    win_w = win.shape[-1]
    inv_freq = ROPE_THETA ** (-jnp.arange(0, MLA_ROPE, 2, dtype=F32) / MLA_ROPE)
    freq_row = jnp.tile(inv_freq, LANE // half).reshape(1, LANE)
    quarter = jnp.full((half,), math.pi / 2, F32)
    phase_row = jnp.concatenate([jnp.zeros((2 * half,), F32), quarter, -quarter]).reshape(1, LANE)
    offsets = jnp.arange(IN_ROW_CHUNK, dtype=F32).reshape(IN_ROW_CHUNK, 1) * freq_row
    cos_off, sin_off = jnp.cos(offsets), jnp.sin(offsets)
    return pl.pallas_call(
        _mla_in_kernel,
        grid=(b, s // IN_TILE),
        in_specs=[
            _row_spec(d, IN_TILE), _const_spec((1, d)), _mod_spec(layer, 0), _mod_spec(layer, 1),
            pl.BlockSpec((None, 1, IN_TILE), lambda bi, i: (bi, 0, i)),
            _const_spec((1, LANE)), _const_spec((1, LANE)),
            _const_spec((IN_ROW_CHUNK, LANE)), _const_spec((IN_ROW_CHUNK, LANE)),
            _const_spec((d, win_w)), _const_spec((1, MLA_Q_RANK)), _const_spec((MLA_Q_RANK, hq)),
            _const_spec((1, MLA_KV_RANK)), _const_spec((MLA_KV_RANK, hn)),
            _const_spec((hv, MLA_KV_RANK)),
        ],
        out_specs=[_row_spec(hq, IN_TILE), _row_spec(hq, IN_TILE),
                   pl.BlockSpec((None, MLA_HEADS, MLA_VT_ROWS, IN_TILE), lambda bi, i: (bi, 0, 0, i))],
        out_shape=[jax.ShapeDtypeStruct((b, s, hq), BF16), jax.ShapeDtypeStruct((b, s, hq), BF16),
                   jax.ShapeDtypeStruct((b, MLA_HEADS, MLA_VT_ROWS, s), BF16)],
        scratch_shapes=[pltpu.VMEM((IN_TILE, LANE), F32)],
        compiler_params=_cparams(("arbitrary", "arbitrary")),
        name="mla_in",
    )(x, gain.reshape(1, d), mod_all, mod_all, positions.reshape(b, 1, s), freq_row, phase_row,
      cos_off, sin_off, win, g_q.reshape(1, -1), wq, g_kv.reshape(1, -1), wkn, wvt)


def _attn_kernel(q_ref, k_ref, vt_ref, o_ref, s0_ref, s1_ref, mb0_ref, mb1_ref, m_ref, acc_ref):
    tq, tk = ATT_Q_BLOCK, ATT_KV_BLOCK
    per_q = tq // tk
    assert per_q == 2
    n_q = q_ref.shape[0] // tq
    n_strips = tq // ATT_STRIP
    all_strips = tuple(range(n_strips))

    def lanes(c):
        return slice(c * ATT_STRIP, (c + 1) * ATT_STRIP)

    def scores(qi, kj, s_ref, mb_ref, diag_offset=None, strips=all_strips):
        start = pl.multiple_of(kj * tk, tk)
        k_blk = k_ref[pl.ds(start, tk), :]
        for c in strips:
            q_lo = c * ATT_STRIP
            if diag_offset is not None and diag_offset > q_lo + ATT_STRIP - 1:
                s_ref[c] = jnp.full((tk, ATT_STRIP), -jnp.inf, F32)
                mb_ref[:, lanes(c)] = jnp.full((1, ATT_STRIP), -jnp.inf, F32)
                continue
            q_strip = q_ref[pl.ds(pl.multiple_of(qi * tq + q_lo, ATT_STRIP), ATT_STRIP), :]
            s_t = _dot_nt(k_blk, q_strip)
            if diag_offset is not None and diag_offset + tk - 1 > q_lo:
                key = lax.broadcasted_iota(jnp.int32, (tk, ATT_STRIP), 0) + diag_offset
                qry = lax.broadcasted_iota(jnp.int32, (tk, ATT_STRIP), 1) + q_lo
                s_t = jnp.where(key <= qry, s_t, -jnp.inf)
            s_ref[c] = s_t
            mb_ref[:, lanes(c)] = jnp.max(s_t, axis=0, keepdims=True)

    def accumulate(kj, s_ref, mb_ref, strips=all_strips):
        start = pl.multiple_of(kj * tk, tk)
        vt_blk = vt_ref[:, pl.ds(start, tk)]
        for c in strips:
            cols = lanes(c)
            m_prev = m_ref[:, cols]
            m_new = jnp.maximum(m_prev, mb_ref[:, cols])
            p_t = jnp.exp2((s_ref[c] - m_new).astype(BF16))
            alpha = jnp.exp2(m_prev - m_new)
            acc_ref[:, cols] = alpha * acc_ref[:, cols] + _dot(vt_blk, p_t)
            m_ref[:, cols] = m_new

    def reset():
        m_ref[...] = jnp.full_like(m_ref, -jnp.inf)
        acc_ref[...] = jnp.zeros_like(acc_ref)

    def finish(qi, pending):
        accumulate(pending, s1_ref, mb1_ref)
        acc = acc_ref[...]
        o_t = acc[:MLA_VDIM, :] / acc[MLA_VDIM:MLA_VDIM + 1, :]
        o_ref[pl.ds(pl.multiple_of(qi * tq, tq), tq), :] = o_t.T.astype(BF16)

    def run_block(qi):
        first = per_q * qi
        reset()
        scores(qi, first + 1, s1_ref, mb1_ref, diag_offset=tk)
        accumulate(first, s0_ref, mb0_ref)
        if isinstance(qi, int) and qi == 0:
            return first + 1

        for c in all_strips:
            if tk <= c * ATT_STRIP + ATT_STRIP - 1:
                accumulate(first + 1, s1_ref, mb1_ref, strips=(c,))
            scores(qi, 0, s0_ref, mb0_ref, strips=(c,))

        def pair(p, pending):
            for c in all_strips:
                accumulate(pending, s0_ref, mb0_ref, strips=(c,))
                scores(qi, 2 * p + 1, s1_ref, mb1_ref, strips=(c,))
            for c in all_strips:
                accumulate(2 * p + 1, s1_ref, mb1_ref, strips=(c,))
                scores(qi, 2 * p + 2, s0_ref, mb0_ref, strips=(c,))
            return 2 * p + 2

        def quad(u, pending):
            return pair(2 * u + 1, pair(2 * u, pending))

        n_pairs = qi - 1
        pending = lax.fori_loop(0, n_pairs // 2, quad, 0)
        pending = lax.fori_loop(0, n_pairs % 2, lambda _, pend: pair(n_pairs - 1, pend), pending)
        last = 2 * qi - 1
        for c in all_strips:
            accumulate(pending, s0_ref, mb0_ref, strips=(c,))
            scores(qi, last, s1_ref, mb1_ref, strips=(c,))
        return last

    scores(0, 0, s0_ref, mb0_ref, diag_offset=0)
    pending0 = run_block(0)

    def q_block(qi, pending):
        scores(qi, per_q * qi, s0_ref, mb0_ref, diag_offset=0)
        finish(qi - 1, pending)
        return run_block(qi)

    pending_last = lax.fori_loop(1, n_q, q_block, pending0)
    finish(n_q - 1, pending_last)


def _attention(q, kc, vt):
    b, _, _, s = vt.shape
    tq, tk = ATT_Q_BLOCK, ATT_KV_BLOCK
    return pl.pallas_call(
        _attn_kernel,
        grid=(b, MLA_HEADS),
        in_specs=[
            pl.BlockSpec((None, s, MLA_QK_PAD), lambda bi, h: (bi, 0, h)),
            pl.BlockSpec((None, s, MLA_QK_PAD), lambda bi, h: (bi, 0, h)),
            pl.BlockSpec((None, None, MLA_VT_ROWS, s), lambda bi, h: (bi, h, 0, 0)),
        ],
        out_specs=pl.BlockSpec((None, s, MLA_VDIM), lambda bi, h: (bi, 0, h)),
        out_shape=jax.ShapeDtypeStruct((b, s, MLA_HEADS * MLA_VDIM), BF16),
        scratch_shapes=[pltpu.VMEM((tq // ATT_STRIP, tk, ATT_STRIP), F32),
                        pltpu.VMEM((tq // ATT_STRIP, tk, ATT_STRIP), F32),
                        pltpu.VMEM((1, tq), F32), pltpu.VMEM((1, tq), F32),
                        pltpu.VMEM((1, tq), F32), pltpu.VMEM((MLA_VT_ROWS, tq), F32)],
        compiler_params=_cparams(("arbitrary", "arbitrary")),
        name="mla_attention",
    )(q, kc, vt)


def kernel(x, c, positions, ada_w, ada_b, norm_mix, norm_ffn, gla_w_in, gla_w_gate, gla_b_gate,
           gla_g_out, gla_w_out, mla_w_in, mla_g_q, mla_w_q_up, mla_g_kv, mla_w_kv_up, mla_w_out,
           ffn_w_in, ffn_w_out, final_norm):
    batch = x.shape[0]
    depth = ada_w.shape[0]
    assert batch <= SUBLANE and ada_w.shape[-1] == N_MOD * D_MODEL
    c_pad = jnp.pad(c, ((0, SUBLANE - batch), (0, 0)))
    mod_all = _adaln(c_pad, ada_w, ada_b)

    for i in range(depth):
        j = i // 2
        if i % 2 == 0:
            q, k, v, r, la = _gla_in(x, mod_all, i, norm_mix[i], gla_w_in, j, gla_w_gate,
                                     gla_b_gate[j])
            mixed = _gla(q, k, v, r, la, gla_g_out[j])
            w_o = gla_w_out
        else:
            q, kc, vt = _mla_in(x, mod_all, i, norm_mix[i], positions, mla_w_in[j], mla_g_q[j],
                                mla_w_q_up[j], mla_g_kv[j], mla_w_kv_up[j])
            mixed = _attention(q, kc, vt)
            w_o = mla_w_out
        x, act = _mix_out_ffn_in(mixed, w_o, j, x, mod_all, i, norm_ffn[i], ffn_w_in)
        last = i == depth - 1
        x = _out_res(act, ffn_w_out, i, x, mod_all, i, 5, final_gain=final_norm if last else None)
    return x
```
